```python
import math
import jax, jax.numpy as jnp
from jax import lax
import numpy as np

D_MODEL = 1024
BATCH = 2
SEQ = 8192
DEPTH = 4
DEC_BATCH = 128
DEC_SEQ = 4
PAST_LEN = 2048
PAGE_SIZE = 128

CM_WIDTH = 256
CM_GROUPS = 4
CM_GROUP_DIM = CM_WIDTH // CM_GROUPS
CM_CHUNK = 128
POOL_WIDTH = 256
POOL_WINDOWS = (2, 4, 8, 16)
POOL_GROUP_DIM = POOL_WIDTH // len(POOL_WINDOWS)
POOL_MAXW = max(POOL_WINDOWS)
POOL_HIST = POOL_MAXW - 1
FOX_HEADS = 4
FOX_HEAD_DIM = 64
FOX_WIDTH = FOX_HEADS * FOX_HEAD_DIM
Q_BLOCK = 128
SSD_HEADS = 4
SSD_HEAD_DIM = 64
SSD_WIDTH = SSD_HEADS * SSD_HEAD_DIM
SSD_GROUPS = 2
SSD_STATE = 128
SSD_CONV = 4
SSD_CHUNK = 128
SSD_CONV_DIM = SSD_WIDTH + 2 * SSD_GROUPS * SSD_STATE
D_FF = 2816
FFN_CONV = 3
N_BRANCH = 4
EPS = 1e-6
IN_SPLITS = (2 * CM_WIDTH, POOL_WIDTH, 3 * FOX_WIDTH, FOX_HEADS, SSD_WIDTH, SSD_CONV_DIM, SSD_HEADS, N_BRANCH * D_MODEL)
N_IN = sum(IN_SPLITS)

kernel_name = 'hybrid_gated_branch_decoder_step'


def rmsnorm(x, g):
    xf = x.astype(jnp.float32)
    y = xf * lax.rsqrt(jnp.mean(xf * xf, axis=-1, keepdims=True) + EPS)
    return (y * g.astype(jnp.float32)).astype(x.dtype)


def causal_dwconv(xp, w, b):
    c = xp.shape[-1]
    y = lax.conv_general_dilated(xp, w[:, None, :].astype(xp.dtype), (1,), 'VALID',
                                 dimension_numbers=('NWC', 'WIO', 'NWC'), feature_group_count=c)
    return y + b.astype(xp.dtype)


def chunk_spatial_mix(v, ws, bs):
    n, L, _ = v.shape
    Lp = -(-L // CM_CHUNK) * CM_CHUNK
    vp = jnp.pad(v, ((0, 0), (0, Lp - L), (0, 0))).reshape(n, Lp // CM_CHUNK, CM_CHUNK, CM_GROUPS, CM_GROUP_DIM)
    causal = jnp.tril(jnp.ones((CM_CHUNK, CM_CHUNK), ws.dtype))
    s = jnp.einsum('gts,ncsgd->nctgd', ws * causal, vp) + bs.T[None, None, :, :, None]
    return s.reshape(n, Lp, CM_WIDTH)[:, :L]


def multiscale_pool(seq, pos, pool_w, pool_scale):
    n, L, _ = seq.shape
    cs = jnp.cumsum(seq.astype(jnp.float32), axis=1)
    cs_pad = jnp.concatenate([jnp.zeros((n, POOL_MAXW, POOL_WIDTH), jnp.float32), cs], axis=1)
    diffs = []
    for g, w in enumerate(POOL_WINDOWS):
        cols = slice(g * POOL_GROUP_DIM, (g + 1) * POOL_GROUP_DIM)
        start = POOL_MAXW - w
        wsum = cs[:, :, cols] - cs_pad[:, start:start + L, cols]
        count = jnp.minimum(pos + 1, w).astype(jnp.float32)[None, :, None]
        diffs.append(wsum / count - seq[:, :, cols].astype(jnp.float32))
    d = jnp.stack(diffs, axis=2)
    y = jnp.einsum('nlgc,gcd->nlgd', d, pool_w.astype(jnp.float32)).reshape(n, L, POOL_WIDTH)
    return (y * pool_scale.astype(jnp.float32)).astype(seq.dtype)


def fox_attend_block(q, cq, qpos, k, v, ck, kpos):
    s = jnp.einsum('nqhd,nkhd->nhqk', q, k).astype(jnp.float32) * (FOX_HEAD_DIM ** -0.5)
    bias = jnp.swapaxes(cq, 1, 2)[..., :, None] - jnp.swapaxes(ck, 1, 2)[..., None, :]
    mask = kpos[None, :] <= qpos[:, None]
    p = jax.nn.softmax(jnp.where(mask, s + bias, -jnp.inf), axis=-1)
    return jnp.einsum('nhqk,nkhd->nqhd', p.astype(v.dtype), v)


def fox_prompt(q, k, v, c, pos):
    n, L = q.shape[:2]
    nb = L // Q_BLOCK
    qb = jnp.moveaxis(q.reshape(n, nb, Q_BLOCK, FOX_HEADS, FOX_HEAD_DIM), 1, 0)
    cb = jnp.moveaxis(c.reshape(n, nb, Q_BLOCK, FOX_HEADS), 1, 0)
    pb = pos.reshape(nb, Q_BLOCK)
    out = lax.map(lambda a: fox_attend_block(a[0], a[1], a[2], k, v, c, pos), (qb, cb, pb))
    return jnp.moveaxis(out, 0, 1).reshape(n, L, FOX_HEADS, FOX_HEAD_DIM)


def ssd_chunked(xdt, da, bm, cm, h0):
    n, L = xdt.shape[:2]
    T = math.gcd(L, SSD_CHUNK)
    nc = L // T
    R = SSD_HEADS // SSD_GROUPS
    X = xdt.reshape(n, nc, T, SSD_GROUPS, R, SSD_HEAD_DIM)
    A = da.reshape(n, nc, T, SSD_GROUPS, R)
    B = bm.reshape(n, nc, T, SSD_GROUPS, SSD_STATE)
    C = cm.reshape(n, nc, T, SSD_GROUPS, SSD_STATE)
    acs = jnp.cumsum(A, axis=2)
    causal = (jnp.arange(T)[:, None] >= jnp.arange(T)[None, :])[None, None, :, :, None, None]
    seg = acs[:, :, :, None] - acs[:, :, None, :]
    decay = jnp.exp(jnp.where(causal, seg, -jnp.inf))
    cb = jnp.einsum('nctgk,ncsgk->nctsg', C, B)
    y_diag = jnp.einsum('nctsg,nctsgr,ncsgrp->nctgrp', cb, decay, X)
    to_end = jnp.exp(acs[:, :, -1:] - acs)
    chunk_states = jnp.einsum('nctgk,nctgr,nctgrp->ncgrpk', B, to_end, X)
    chunk_decay = jnp.exp(acs[:, :, -1])

    def carry_state(h, inp):
        s_c, d_c = inp
        return h * d_c[..., None, None] + s_c, h

    h_last, h_in = lax.scan(carry_state, h0.reshape(n, SSD_GROUPS, R, SSD_HEAD_DIM, SSD_STATE),
                            (jnp.moveaxis(chunk_states, 1, 0), jnp.moveaxis(chunk_decay, 1, 0)))
    h_in = jnp.moveaxis(h_in, 0, 1)
    y_off = jnp.einsum('nctgk,ncgrpk,nctgr->nctgrp', C, h_in, jnp.exp(acs))
    y = (y_diag + y_off).reshape(n, L, SSD_HEADS, SSD_HEAD_DIM)
    return y, h_last.reshape(n, SSD_HEADS, SSD_HEAD_DIM, SSD_STATE)


def gather_pages(pool, page_table):
    g = pool[page_table]
    return g.reshape(g.shape[0], g.shape[1] * g.shape[2], *g.shape[3:])


def decoder_layer(x, w, pos, past):
    n, L, _ = x.shape
    adt = x.dtype
    f32 = jnp.float32
    h = rmsnorm(x, w['norm1_g'])
    proj = h @ w['w_in']
    split_points = [int(i) for i in np.cumsum(IN_SPLITS)[:-1]]
    a_uv, pool_in, qkv, f_raw, z, xbc, dt_raw, gate_raw = jnp.split(proj, split_points, axis=-1)

    u, va = jnp.split(jax.nn.gelu(a_uv), 2, axis=-1)
    va = rmsnorm(va, w['cm_norm_g'])
    y_a = u * chunk_spatial_mix(va, w['cm_ws'], w['cm_b']).astype(adt)

    if past is None:
        pool_seq, pool_pos = pool_in, pos
    else:
        pool_seq = jnp.concatenate([past['pool'].astype(adt), pool_in], axis=1)
        pool_pos = jnp.arange(POOL_HIST + L) + (pos[0] - POOL_HIST)
    y_b = multiscale_pool(pool_seq, pool_pos, w['pool_w'], w['pool_scale'])[:, -L:]
    new_pool = pool_seq[:, -POOL_HIST:]

    q, k, v = [t.reshape(n, L, FOX_HEADS, FOX_HEAD_DIM) for t in jnp.split(qkv, 3, axis=-1)]
    logf = jax.nn.log_sigmoid((f_raw + w['fox_bf']).astype(f32))
    if past is None:
        y_c = fox_prompt(q, k, v, jnp.cumsum(logf, axis=1), pos)
    else:
        k_all = jnp.concatenate([past['k'].astype(adt), k], axis=1)
        v_all = jnp.concatenate([past['v'].astype(adt), v], axis=1)
        c_all = jnp.cumsum(jnp.concatenate([past['logf'].astype(f32), logf], axis=1), axis=1)
        kpos = jnp.arange(k_all.shape[1])
        y_c = fox_attend_block(q, c_all[:, -L:], pos, k_all, v_all, c_all, kpos)
    y_c = y_c.reshape(n, L, FOX_WIDTH)

    conv_hist = jnp.zeros((n, SSD_CONV - 1, SSD_CONV_DIM), adt) if past is None else past['ssd_conv'].astype(adt)
    xbc_ext = jnp.concatenate([conv_hist, xbc], axis=1)
    xbc_act = jax.nn.silu(causal_dwconv(xbc_ext, w['ssd_conv_w'], w['ssd_conv_b'])).astype(f32)
    xs, bm, cm = jnp.split(xbc_act, [SSD_WIDTH, SSD_WIDTH + SSD_GROUPS * SSD_STATE], axis=-1)
    xs = xs.reshape(n, L, SSD_HEADS, SSD_HEAD_DIM)
    bm = bm.reshape(n, L, SSD_GROUPS, SSD_STATE)
    cm = cm.reshape(n, L, SSD_GROUPS, SSD_STATE)
    dt = jax.nn.softplus((dt_raw + w['ssd_dt_bias']).astype(f32))
    a = -jnp.exp(w['ssd_a_log'].astype(f32))
    h0 = jnp.zeros((n, SSD_HEADS, SSD_HEAD_DIM, SSD_STATE), f32) if past is None else past['ssd'].astype(f32)
    y_s, h_last = ssd_chunked(xs * dt[..., None], dt * a, bm, cm, h0)
    y_s = (y_s + w['ssd_d'].astype(f32)[:, None] * xs).reshape(n, L, SSD_WIDTH)
    y_d = rmsnorm(y_s * jax.nn.silu(z.astype(f32)), w['ssd_norm_g']).astype(adt)

    merged = jnp.zeros_like(x)
    for i, y_i in enumerate((y_a, y_b, y_c, y_d)):
        gate = jax.nn.sigmoid(gate_raw[..., i * D_MODEL:(i + 1) * D_MODEL])
        merged = merged + gate * (y_i.astype(adt) @ w['w_branch'][i])
    x = x + merged @ w['w_out']

    h2 = rmsnorm(x, w['norm2_g'])
    up = h2 @ w['ffn_up']
    ffn_hist = jnp.zeros((n, FFN_CONV - 1, 2 * D_FF), adt) if past is None else past['ffn_conv'].astype(adt)
    up_ext = jnp.concatenate([ffn_hist, up], axis=1)
    g_ff, u_ff = jnp.split(causal_dwconv(up_ext, w['ffn_conv_w'], w['ffn_conv_b']), 2, axis=-1)
    x = x + (jax.nn.silu(g_ff) * u_ff) @ w['ffn_down']

    new_state = dict(k=k, v=v, logf=logf, chunk_v=va, pool=new_pool,
                     ssd_conv=xbc_ext[:, -(SSD_CONV - 1):], ssd=h_last, ffn_conv=up_ext[:, -(FFN_CONV - 1):])
    return x, new_state


def setup_inputs(seed: int = 0) -> dict:
    key = jax.random.key(seed)
    ks = iter(jax.random.split(key, 48))
    f32 = jnp.float32

    def nrm(shape, scale):
        return scale * jax.random.normal(next(ks), shape, f32)

    n_pages = PAST_LEN // PAGE_SIZE
    n_phys = (DEC_BATCH * n_pages * 5 + 3) // 4
    page_table = jax.random.permutation(next(ks), n_phys)[:DEC_BATCH * n_pages].reshape(DEC_BATCH, n_pages).astype(jnp.int32)
    dt0 = jnp.exp(jax.random.uniform(next(ks), (DEPTH, SSD_HEADS), f32, math.log(1e-3), math.log(1e-1)))
    return {
        'x_prompt': nrm((BATCH, SEQ, D_MODEL), 1.0),
        'x_sample': nrm((DEC_BATCH, DEC_SEQ, D_MODEL), 1.0),
        'cache_k': nrm((DEPTH, n_phys, PAGE_SIZE, FOX_HEADS, FOX_HEAD_DIM), 1.0),
        'cache_v': nrm((DEPTH, n_phys, PAGE_SIZE, FOX_HEADS, FOX_HEAD_DIM), 1.0),
        'cache_logf': jax.nn.log_sigmoid(3.0 + nrm((DEPTH, n_phys, PAGE_SIZE, FOX_HEADS), 1.0)),
        'page_table': page_table,
        'state_pool': nrm((DEPTH, DEC_BATCH, POOL_HIST, POOL_WIDTH), 1.0),
        'state_ssd_conv': nrm((DEPTH, DEC_BATCH, SSD_CONV - 1, SSD_CONV_DIM), 1.0),
        'state_ssd': nrm((DEPTH, DEC_BATCH, SSD_HEADS, SSD_HEAD_DIM, SSD_STATE), 0.5),
        'state_ffn_conv': nrm((DEPTH, DEC_BATCH, FFN_CONV - 1, 2 * D_FF), 1.0),
        'norm1_g': 1.0 + nrm((DEPTH, D_MODEL), 0.1),
        'w_in': nrm((DEPTH, D_MODEL, N_IN), D_MODEL ** -0.5),
        'cm_norm_g': 1.0 + nrm((DEPTH, CM_WIDTH), 0.1),
        'cm_ws': nrm((DEPTH, CM_GROUPS, CM_CHUNK, CM_CHUNK), 0.5 * CM_CHUNK ** -0.5),
        'cm_b': 1.0 + nrm((DEPTH, CM_GROUPS, CM_CHUNK), 0.1),
        'pool_w': nrm((DEPTH, len(POOL_WINDOWS), POOL_GROUP_DIM, POOL_GROUP_DIM), POOL_GROUP_DIM ** -0.5),
        'pool_scale': 1.0 + nrm((DEPTH, POOL_WIDTH), 0.1),
        'fox_bf': 3.0 + nrm((DEPTH, FOX_HEADS), 0.5),
        'ssd_conv_w': nrm((DEPTH, SSD_CONV, SSD_CONV_DIM), SSD_CONV ** -0.5),
        'ssd_conv_b': nrm((DEPTH, SSD_CONV_DIM), 0.01),
        'ssd_dt_bias': dt0 + jnp.log(-jnp.expm1(-dt0)),
        'ssd_a_log': jnp.log(jax.random.uniform(next(ks), (DEPTH, SSD_HEADS), f32, 1.0, 16.0)),
        'ssd_d': 1.0 + nrm((DEPTH, SSD_HEADS), 0.1),
        'ssd_norm_g': 1.0 + nrm((DEPTH, SSD_WIDTH), 0.1),
        'w_branch': nrm((DEPTH, N_BRANCH, CM_WIDTH, D_MODEL), CM_WIDTH ** -0.5),
        'w_out': nrm((DEPTH, D_MODEL, D_MODEL), D_MODEL ** -0.5),
        'norm2_g': 1.0 + nrm((DEPTH, D_MODEL), 0.1),
        'ffn_up': nrm((DEPTH, D_MODEL, 2 * D_FF), D_MODEL ** -0.5),
        'ffn_conv_w': nrm((DEPTH, FFN_CONV, 2 * D_FF), FFN_CONV ** -0.5),
        'ffn_conv_b': nrm((DEPTH, 2 * D_FF), 0.01),
        'ffn_down': nrm((DEPTH, D_FF, D_MODEL), D_FF ** -0.5),
        'final_norm_g': 1.0 + nrm((D_MODEL,), 0.1),
    }


def reference(x_prompt, x_sample, cache_k, cache_v, cache_logf, page_table, state_pool, state_ssd_conv,
              state_ssd, state_ffn_conv, norm1_g, w_in, cm_norm_g, cm_ws, cm_b, pool_w, pool_scale, fox_bf,
              ssd_conv_w, ssd_conv_b, ssd_dt_bias, ssd_a_log, ssd_d, ssd_norm_g, w_branch, w_out, norm2_g,
              ffn_up, ffn_conv_w, ffn_conv_b, ffn_down, final_norm_g):
    pos_prompt = jnp.arange(x_prompt.shape[1])
    pos_sample = PAST_LEN + jnp.arange(x_sample.shape[1])
    xp, xs = x_prompt, x_sample
    st_p, st_s = [], []
    for l in range(DEPTH):
        w = dict(norm1_g=norm1_g[l], w_in=w_in[l], cm_norm_g=cm_norm_g[l], cm_ws=cm_ws[l], cm_b=cm_b[l],
                 pool_w=pool_w[l], pool_scale=pool_scale[l], fox_bf=fox_bf[l], ssd_conv_w=ssd_conv_w[l],
                 ssd_conv_b=ssd_conv_b[l], ssd_dt_bias=ssd_dt_bias[l], ssd_a_log=ssd_a_log[l], ssd_d=ssd_d[l],
                 ssd_norm_g=ssd_norm_g[l], w_branch=w_branch[l], w_out=w_out[l], norm2_g=norm2_g[l],
                 ffn_up=ffn_up[l], ffn_conv_w=ffn_conv_w[l], ffn_conv_b=ffn_conv_b[l], ffn_down=ffn_down[l])
        past = dict(k=gather_pages(cache_k[l], page_table), v=gather_pages(cache_v[l], page_table),
                    logf=gather_pages(cache_logf[l], page_table), pool=state_pool[l],
                    ssd_conv=state_ssd_conv[l], ssd=state_ssd[l], ffn_conv=state_ffn_conv[l])
        xp, sp = decoder_layer(xp, w, pos_prompt, None)
        xs, ss = decoder_layer(xs, w, pos_sample, past)
        st_p.append(sp)
        st_s.append(ss)
    y_prompt = rmsnorm(xp, final_norm_g)
    y_sample = rmsnorm(xs, final_norm_g)

    def stacked(states, name):
        return jnp.stack([s[name] for s in states], axis=0)

    k_prompt, v_prompt, logf_prompt = stacked(st_p, 'k'), stacked(st_p, 'v'), stacked(st_p, 'logf')
    k_sample, v_sample, logf_sample = stacked(st_s, 'k'), stacked(st_s, 'v'), stacked(st_s, 'logf')
    chunk_v_sample = stacked(st_s, 'chunk_v')
    pool_prompt, pool_sample = stacked(st_p, 'pool'), stacked(st_s, 'pool')
    ssd_conv_prompt, ssd_conv_sample = stacked(st_p, 'ssd_conv'), stacked(st_s, 'ssd_conv')
    ssd_prompt, ssd_sample = stacked(st_p, 'ssd'), stacked(st_s, 'ssd')
    ffn_conv_prompt, ffn_conv_sample = stacked(st_p, 'ffn_conv'), stacked(st_s, 'ffn_conv')
    return (y_prompt, y_sample, k_prompt, v_prompt, logf_prompt, k_sample, v_sample, logf_sample,
            chunk_v_sample, pool_prompt, pool_sample, ssd_conv_prompt, ssd_conv_sample, ssd_prompt, ssd_sample,
            ffn_conv_prompt, ffn_conv_sample)
```

```python
import functools
import math

import numpy as np
import jax
import jax.numpy as jnp
from jax import lax
from jax.experimental import pallas as pl
from jax.experimental.pallas import tpu as pltpu

F32 = jnp.float32
BF16 = jnp.bfloat16

D_MODEL = 1024
WIDTH = 256
PAGE_SIZE = 128
CM_GROUPS = 4
CM_CHUNK = 128
POOL_WINDOWS = (2, 4, 8, 16)
POOL_HIST = 15
POOL_PAD_STEPS = 32
FOX_HEADS = 4
FOX_HEAD_DIM = 64
SSD_HEADS = 4
SSD_HEAD_DIM = 64
SSD_GROUPS = 2
SSD_STATE = 128
SSD_CONV = 4
SSD_CHUNK = 128
SSD_CONV_DIM = 768
D_FF = 2816
FFN_CONV = 3
N_BRANCH = 4
EPS = 1e-6
SMALL_W = 128
P_COLS = 10 * WIDTH + SMALL_W
SMALL_BLK = 10 * WIDTH // SMALL_W
U_AU, U_AV, U_POOL, U_Q, U_K, U_V, U_Z, U_XS, U_BM, U_CM = range(10)

VMEM_LIMIT = 56 * 1024 * 1024


def _cparams(sem):
    return pltpu.CompilerParams(dimension_semantics=sem, vmem_limit_bytes=VMEM_LIMIT)


def _sigmoid(x):
    return 1.0 / (1.0 + jnp.exp(-x))


def _silu(x):
    return x * _sigmoid(x)


def _softplus(x):
    return jnp.maximum(x, 0.0) + jnp.log(1.0 + jnp.exp(-jnp.abs(x)))


def _log_sigmoid(x):
    return -_softplus(-x)


def _gelu_tanh(x):
    return 0.5 * x * (1.0 + jnp.tanh(math.sqrt(2.0 / math.pi) * (x + 0.044715 * (x * x * x))))


def _rms(x, g):
    return x * lax.rsqrt(jnp.mean(x * x, axis=-1, keepdims=True) + EPS) * g


def _split3(x):
    hi = x.astype(BF16)
    r1 = x - hi.astype(F32)
    mid = r1.astype(BF16)
    lo = (r1 - mid.astype(F32)).astype(BF16)
    return hi, mid, lo


def _dot(a, b):
    return jnp.dot(a, b, preferred_element_type=F32)


def _dot_nt(a, b):
    return lax.dot_general(a, b, (((1,), (1,)), ((), ())), preferred_element_type=F32)


def _ones_dot_left(ones_bf16, x):
    hi, mid, lo = _split3(x)
    return _dot(ones_bf16, hi) + _dot(ones_bf16, mid) + _dot(ones_bf16, lo)


def _ones_dot_right(x, ones_bf16):
    hi, mid, lo = _split3(x)
    return _dot(hi, ones_bf16) + _dot(mid, ones_bf16) + _dot(lo, ones_bf16)


def _iota(shape, dim):
    return lax.broadcasted_iota(jnp.int32, shape, dim)


def _proj_body(x_ref, g_ref, w_ref, o_ref):
    h = _rms(x_ref[...], g_ref[...]).astype(BF16)
    o_ref[...] = _dot(h, w_ref[...])


def _proj(x, g, w, tm):
    m, d = x.shape
    n = w.shape[1]
    return pl.pallas_call(
        _proj_body,
        grid=(m // tm,),
        in_specs=[pl.BlockSpec((tm, d), lambda i: (i, 0)),
                  pl.BlockSpec((1, d), lambda i: (0, 0)),
                  pl.BlockSpec((d, n), lambda i: (0, 0))],
        out_specs=pl.BlockSpec((tm, n), lambda i: (i, 0)),
        out_shape=jax.ShapeDtypeStruct((m, n), F32),
        compiler_params=_cparams(("arbitrary",)),
        name="proj",
    )(x, g, w)


def _mixa_body(u_ref, v_ref, wm_ref, b_ref, g_ref, y_ref, va_ref, *, t, n_chunks):
    lane_grp = _iota((t, WIDTH), 1) >> 6
    for c in range(n_chunks):
        rows = pl.ds(c * t, t)
        u = _gelu_tanh(u_ref[rows, :])
        va = _rms(_gelu_tanh(v_ref[rows, :]), g_ref[...])
        va_ref[rows, :] = va
        vab = va.astype(BF16)
        zero = jnp.zeros_like(vab)
        s = b_ref[...]
        for gi in range(CM_GROUPS):
            s = s + _dot(wm_ref[gi], jnp.where(lane_grp == gi, vab, zero))
        y_ref[rows, :] = u * s


def _mixa(p, wm, bias, g, t, tm):
    m = p.shape[0]
    body = functools.partial(_mixa_body, t=t, n_chunks=tm // t)
    return pl.pallas_call(
        body,
        grid=(m // tm,),
        in_specs=[pl.BlockSpec((tm, WIDTH), lambda i: (i, U_AU)),
                  pl.BlockSpec((tm, WIDTH), lambda i: (i, U_AV)),
                  pl.BlockSpec((CM_GROUPS, t, t), lambda i: (0, 0, 0)),
                  pl.BlockSpec((t, WIDTH), lambda i: (0, 0)),
                  pl.BlockSpec((1, WIDTH), lambda i: (0, 0))],
        out_specs=[pl.BlockSpec((tm, WIDTH), lambda i: (i, 0)),
                   pl.BlockSpec((tm, WIDTH), lambda i: (i, 0))],
        out_shape=[jax.ShapeDtypeStruct((m, WIDTH), F32),
                   jax.ShapeDtypeStruct((m, WIDTH), F32)],
        compiler_params=_cparams(("arbitrary",)),
        name="mixer_a",
    )(p, p, wm, bias, g)


def _pool_body(seq_ref, h0_ref, w_ref, scale_ref, y_ref, e1, e2, e4, e8, *, s, tm, pos0, steps_per_tile):
    j = pl.program_id(1)
    hr = POOL_PAD_STEPS * s
    tot = hr + tm

    @pl.when(j == 0)
    def _():
        e1[0:hr, :] = h0_ref[0]

    e1[hr:tot, :] = seq_ref[...]
    e2[8 * s:tot, :] = e1[8 * s:tot, :] + e1[7 * s:tot - s, :]
    e4[16 * s:tot, :] = e2[16 * s:tot, :] + e2[14 * s:tot - 2 * s, :]
    e8[24 * s:tot, :] = e4[24 * s:tot, :] + e4[20 * s:tot - 4 * s, :]
    cur = e1[hr:tot, :]
    s2 = e2[hr:tot, :]
    s4 = e4[hr:tot, :]
    s8 = e8[hr:tot, :]
    s16 = s8 + e8[hr - 8 * s:tot - 8 * s, :]
    lane_grp = _iota((tm, WIDTH), 1) >> 6
    pos = pos0 + j * steps_per_tile + (_iota((tm, WIDTH), 0) >> (s.bit_length() - 1))
    wsum = jnp.where(lane_grp == 0, s2, jnp.where(lane_grp == 1, s4, jnp.where(lane_grp == 2, s8, s16)))
    win = jnp.where(lane_grp == 0, 2, jnp.where(lane_grp == 1, 4, jnp.where(lane_grp == 2, 8, 16)))
    count = jnp.minimum(pos + 1, win).astype(F32)
    d = wsum / count - cur
    y_ref[...] = _dot(d.astype(BF16), w_ref[...]) * scale_ref[...]
    e1[0:hr, :] = e1[tm:tot, :]


def _pool(seq, col_blk, h0, w_bd, scale, *, n_seq, s, tm, pos0):
    m = seq.shape[0]
    tiles = m // n_seq // tm
    hr = POOL_PAD_STEPS * s
    body = functools.partial(_pool_body, s=s, tm=tm, pos0=pos0, steps_per_tile=tm // s)
    return pl.pallas_call(
        body,
        grid=(n_seq, tiles),
        in_specs=[pl.BlockSpec((tm, WIDTH), lambda n, j: (n * tiles + j, col_blk)),
                  pl.BlockSpec((1, hr, WIDTH), lambda n, j: (n, 0, 0)),
                  pl.BlockSpec((WIDTH, WIDTH), lambda n, j: (0, 0)),
                  pl.BlockSpec((1, WIDTH), lambda n, j: (0, 0))],
        out_specs=pl.BlockSpec((tm, WIDTH), lambda n, j: (n * tiles + j, 0)),
        out_shape=jax.ShapeDtypeStruct((m, WIDTH), F32),
        scratch_shapes=[pltpu.VMEM((hr + tm, WIDTH), F32) for _ in range(4)],
        compiler_params=_cparams(("arbitrary", "arbitrary")),
        name="mixer_b",
    )(seq, h0, w_bd, scale)


def _fox_prep_body(q_ref, k_ref, v_ref, sm_ref, bf_ref, selq_ref, selk_ref, selv_ref,
                   lf_ref, qa_ref, ka_ref, va_ref, carry, *, tm):
    j = pl.program_id(1)

    @pl.when(j == 0)
    def _():
        carry[...] = jnp.zeros_like(carry)

    logf = _log_sigmoid(sm_ref[...] + bf_ref[...])
    lf_ref[...] = logf
    tril = (_iota((tm, tm), 0) >= _iota((tm, tm), 1)).astype(BF16)
    c = _ones_dot_left(tril, logf) + carry[...]
    carry[...] = c[tm - 1:tm, :]
    qb = q_ref[...].astype(BF16)
    kb = k_ref[...].astype(BF16)
    vb = v_ref[...].astype(BF16)
    lane = _iota((tm, 2 * FOX_HEAD_DIM), 1)
    d = FOX_HEAD_DIM
    for h in range(FOX_HEADS):
        ch = jnp.broadcast_to(c[:, h:h + 1], (tm, 2 * d))
        hi = ch.astype(BF16).astype(F32)
        r1 = ch - hi
        mid = r1.astype(BF16).astype(F32)
        lo = (r1 - mid).astype(BF16).astype(F32)
        one = jnp.ones_like(ch)
        zero = jnp.zeros_like(ch)
        q_extra = jnp.where(lane == d, hi, jnp.where(lane == d + 1, mid, jnp.where(lane == d + 2, lo,
                  jnp.where((lane >= d + 3) & (lane < d + 6), one, zero))))
        k_extra = jnp.where(lane == d + 3, -hi, jnp.where(lane == d + 4, -mid, jnp.where(lane == d + 5, -lo,
                  jnp.where((lane >= d) & (lane < d + 3), one, zero))))
        qa_ref[0, h] = (_dot(qb, selq_ref[h]) + q_extra).astype(BF16)
        ka_ref[0, h] = (_dot(kb, selk_ref[h]) + k_extra).astype(BF16)
        va_ref[0, h] = _dot(vb, selv_ref[h]).astype(BF16)


def _fox_prep(p, bf_row, selq, selk, selv, *, n_seq, tm):
    m = p.shape[0]
    seq_len = m // n_seq
    tiles = seq_len // tm
    aug = jax.ShapeDtypeStruct((n_seq, FOX_HEADS, seq_len, 2 * FOX_HEAD_DIM), BF16)
    aug_spec = pl.BlockSpec((1, FOX_HEADS, tm, 2 * FOX_HEAD_DIM), lambda n, j: (n, 0, j, 0))
    sel_spec = pl.BlockSpec((FOX_HEADS, WIDTH, 2 * FOX_HEAD_DIM), lambda n, j: (0, 0, 0))
    return pl.pallas_call(
        functools.partial(_fox_prep_body, tm=tm),
        grid=(n_seq, tiles),
        in_specs=[pl.BlockSpec((tm, WIDTH), lambda n, j: (n * tiles + j, U_Q)),
                  pl.BlockSpec((tm, WIDTH), lambda n, j: (n * tiles + j, U_K)),
                  pl.BlockSpec((tm, WIDTH), lambda n, j: (n * tiles + j, U_V)),
                  pl.BlockSpec((tm, SMALL_W), lambda n, j: (n * tiles + j, SMALL_BLK)),
                  pl.BlockSpec((1, SMALL_W), lambda n, j: (0, 0)),
                  sel_spec, sel_spec, sel_spec],
        out_specs=[pl.BlockSpec((tm, SMALL_W), lambda n, j: (n * tiles + j, 0)),
                   aug_spec, aug_spec, aug_spec],
        out_shape=[jax.ShapeDtypeStruct((m, SMALL_W), F32), aug, aug, aug],
        scratch_shapes=[pltpu.VMEM((1, SMALL_W), F32)],
        compiler_params=_cparams(("arbitrary", "arbitrary")),
        name="fox_prep",
    )(p, p, p, p, bf_row, selq, selk, selv)


def _fox_flash_body(qi_ref, kj_ref, q_ref, k_ref, v_ref, o_ref, m_ref, l_ref, acc_ref, *, tq):
    step = pl.program_id(1)
    qi = qi_ref[step]
    kj = kj_ref[step]

    @pl.when(kj == 0)
    def _():
        m_ref[...] = jnp.full_like(m_ref, -jnp.inf)
        l_ref[...] = jnp.zeros_like(l_ref)
        acc_ref[...] = jnp.zeros_like(acc_ref)

    visible = (_iota((tq, tq), 0) - _iota((tq, tq), 1)) >= (kj - qi) * tq
    for h in range(FOX_HEADS):
        s = jnp.where(visible, _dot_nt(q_ref[0, h], k_ref[0, h]), -jnp.inf)
        m_prev = m_ref[h]
        m_new = jnp.maximum(m_prev, jnp.max(s, axis=1, keepdims=True))
        alpha = jnp.exp(m_prev - m_new)
        pexp = jnp.exp(s - m_new)
        l_ref[h] = alpha * l_ref[h] + jnp.sum(pexp, axis=1, keepdims=True)
        acc_ref[h] = alpha * acc_ref[h] + _dot(pexp.astype(BF16), v_ref[0, h])
        m_ref[h] = m_new

    @pl.when(kj == qi)
    def _():
        o = [acc_ref[h] / l_ref[h] for h in range(FOX_HEADS)]
        o_ref[...] = jnp.concatenate([o[0] + o[1], o[2] + o[3]], axis=1)


def _fox_flash(qa, ka, va, *, tq):
    n_seq, _, seq_len, w = qa.shape
    nq = seq_len // tq
    qi = np.array([i for i in range(nq) for _ in range(i + 1)], np.int32)
    kj = np.array([j for i in range(nq) for j in range(i + 1)], np.int32)
    q_spec = pl.BlockSpec((1, FOX_HEADS, tq, w), lambda n, s, qi, kj: (n, 0, qi[s], 0))
    kv_spec = pl.BlockSpec((1, FOX_HEADS, tq, w), lambda n, s, qi, kj: (n, 0, kj[s], 0))
    grid_spec = pltpu.PrefetchScalarGridSpec(
        num_scalar_prefetch=2,
        grid=(n_seq, len(qi)),
        in_specs=[q_spec, kv_spec, kv_spec],
        out_specs=pl.BlockSpec((tq, WIDTH), lambda n, s, qi, kj: (n * nq + qi[s], 0)),
        scratch_shapes=[pltpu.VMEM((FOX_HEADS, tq, 1), F32),
                        pltpu.VMEM((FOX_HEADS, tq, 1), F32),
                        pltpu.VMEM((FOX_HEADS, tq, w), F32)],
    )
    return pl.pallas_call(
        functools.partial(_fox_flash_body, tq=tq),
        grid_spec=grid_spec,
        out_shape=jax.ShapeDtypeStruct((n_seq * seq_len, WIDTH), F32),
        compiler_params=_cparams(("arbitrary", "arbitrary")),
        name="fox_flash",
    )(jnp.asarray(qi), jnp.asarray(kj), qa, ka, va)


TOK_PAD = 8
QROWS = FOX_HEADS * TOK_PAD


def _fox_decode_body(pt_ref, *refs, n_pages):
    q_ref, kn_ref, vn_ref, fr_ref, bfc_ref, rep_ref, later_ref = refs[:7]
    k_refs = refs[7:7 + n_pages]
    v_refs = refs[7 + n_pages:7 + 2 * n_pages]
    lf_refs = refs[7 + 2 * n_pages:7 + 3 * n_pages]
    o_ref = refs[7 + 3 * n_pages]
    d = FOX_HEAD_DIM
    ps = PAGE_SIZE

    q8 = q_ref[0] * (d ** -0.5)
    lane_head = _iota((TOK_PAD, WIDTH), 1) >> 6
    qbd = jnp.concatenate([jnp.where(lane_head == h, q8, 0.0) for h in range(FOX_HEADS)], axis=0).astype(BF16)

    triu = (_iota((ps, ps), 0) <= _iota((ps, ps), 1)).astype(BF16)
    lf_new = _log_sigmoid(fr_ref[0] + bfc_ref[...])
    newcum_lane = _ones_dot_right(lf_new, triu)
    row_t = _iota((QROWS, ps), 0) & (TOK_PAD - 1)
    lane_i = _iota((QROWS, ps), 1)
    newcum_col = jnp.sum(jnp.where(lane_i == row_t, newcum_lane, 0.0), axis=1, keepdims=True)

    lst = jnp.concatenate([r[0, 0] for r in lf_refs], axis=0)
    incl = _ones_dot_right(lst, triu)
    tot = jnp.broadcast_to(incl[:, ps - 1:ps], incl.shape)
    g_past = (tot - incl) + _ones_dot_left(later_ref[...], tot)
    g_rep = _ones_dot_left(rep_ref[...], g_past)

    s_new_k = jnp.concatenate([kn_ref[0], jnp.zeros((ps - TOK_PAD, WIDTH), F32)], axis=0).astype(BF16)
    s_new_v = jnp.concatenate([vn_ref[0], jnp.zeros((ps - TOK_PAD, WIDTH), F32)], axis=0).astype(BF16)
    s_new = _dot_nt(qbd, s_new_k) + newcum_col - newcum_lane
    s_new = jnp.where(lane_i <= row_t, s_new, -jnp.inf)

    s_past = []
    for pg in range(n_pages):
        kp = k_refs[pg][0, 0].astype(BF16)
        s_past.append(_dot_nt(qbd, kp) + g_rep[pg * QROWS:(pg + 1) * QROWS, :] + newcum_col)

    m = jnp.max(s_new, axis=1, keepdims=True)
    for sp in s_past:
        m = jnp.maximum(m, jnp.max(sp, axis=1, keepdims=True))
    p_new = jnp.exp(s_new - m)
    l = jnp.sum(p_new, axis=1, keepdims=True)
    acc = _dot(p_new.astype(BF16), s_new_v)
    for pg in range(n_pages):
        pp = jnp.exp(s_past[pg] - m)
        l = l + jnp.sum(pp, axis=1, keepdims=True)
        acc = acc + _dot(pp.astype(BF16), v_refs[pg][0, 0].astype(BF16))
    o = acc / l
    y = jnp.zeros((TOK_PAD, WIDTH), F32)
    for h in range(FOX_HEADS):
        y = y + jnp.where(lane_head == h, o[h * TOK_PAD:(h + 1) * TOK_PAD, :], 0.0)
    o_ref[0] = y


def _fox_decode(layer, page_table, q8, kn8, vn8, fr_lane, bf_col, rep, later, cache_k, cache_v, cache_lft):
    n_b, n_pages = page_table.shape
    ps = PAGE_SIZE

    def page_spec(pg, rows):
        return pl.BlockSpec((1, 1, rows, cache_w[rows]), lambda b, pt: (layer, pt[b, pg], 0, 0))

    cache_w = {ps: WIDTH, TOK_PAD: ps}
    tok_spec = pl.BlockSpec((1, TOK_PAD, WIDTH), lambda b, pt: (b, 0, 0))
    in_specs = [tok_spec, tok_spec, tok_spec,
                pl.BlockSpec((1, QROWS, ps), lambda b, pt: (b, 0, 0)),
                pl.BlockSpec((QROWS, ps), lambda b, pt: (0, 0)),
                pl.BlockSpec((n_pages * QROWS, n_pages * TOK_PAD), lambda b, pt: (0, 0)),
                pl.BlockSpec((n_pages * TOK_PAD, n_pages * TOK_PAD), lambda b, pt: (0, 0))]
    in_specs += [page_spec(pg, ps) for pg in range(n_pages)]
    in_specs += [page_spec(pg, ps) for pg in range(n_pages)]
    in_specs += [page_spec(pg, TOK_PAD) for pg in range(n_pages)]
    grid_spec = pltpu.PrefetchScalarGridSpec(
        num_scalar_prefetch=1,
        grid=(n_b,),
        in_specs=in_specs,
        out_specs=pl.BlockSpec((1, TOK_PAD, WIDTH), lambda b, pt: (b, 0, 0)),
    )
    args = [q8, kn8, vn8, fr_lane, bf_col, rep, later]
    args += [cache_k] * n_pages + [cache_v] * n_pages + [cache_lft] * n_pages
    return pl.pallas_call(
        functools.partial(_fox_decode_body, n_pages=n_pages),
        grid_spec=grid_spec,
        out_shape=jax.ShapeDtypeStruct((n_b, TOK_PAD, WIDTH), F32),
        compiler_params=_cparams(("arbitrary",)),
        name="fox_decode",
    )(page_table, *args)


def _ssd_prompt_body(xs_ref, bm_ref, cm_ref, z_ref, sm_ref, cw_ref, cb_ref, dtb_ref, arow_ref, drow_ref, g_ref,
                     y_ref, hl_ref, ext, state, *, t):
    j = pl.program_id(1)
    last = pl.num_programs(1) - 1
    hd = SSD_HEAD_DIM
    w3 = SSD_CONV_DIM

    @pl.when(j == 0)
    def _():
        ext[0:8, :] = jnp.zeros((8, w3), F32)
        state[...] = jnp.zeros_like(state)

    ext[8:8 + t, 0:WIDTH] = xs_ref[...]
    ext[8:8 + t, WIDTH:2 * WIDTH] = bm_ref[...]
    ext[8:8 + t, 2 * WIDTH:w3] = cm_ref[...]
    conv = cb_ref[...]
    for k in range(SSD_CONV):
        conv = conv + cw_ref[k:k + 1, :] * ext[5 + k:5 + k + t, :]
    ext[0:8, :] = ext[t:t + 8, :]
    act = _silu(conv)
    xs = act[:, 0:WIDTH]
    bm = act[:, WIDTH:2 * WIDTH].astype(BF16)
    cm = act[:, 2 * WIDTH:w3].astype(BF16)

    dt = _softplus(sm_ref[...] + dtb_ref[...])
    da = dt * arow_ref[...]
    tril = (_iota((t, t), 0) >= _iota((t, t), 1))
    acs = _ones_dot_left(tril.astype(BF16), da)
    acs_t = jnp.transpose(acs)
    lane_head = _iota((t, WIDTH), 1) >> 6
    dt_x = jnp.zeros((t, WIDTH), F32)
    end_x = jnp.zeros((t, WIDTH), F32)
    in_x = jnp.zeros((t, WIDTH), F32)
    dec_rows = []
    for h in range(SSD_HEADS):
        col = acs[:, 4 + h:5 + h]
        lastv = acs[t - 1:t, 4 + h:5 + h]
        dt_x = jnp.where(lane_head == h, dt[:, 4 + h:5 + h], dt_x)
        end_x = jnp.where(lane_head == h, jnp.exp(lastv - col), end_x)
        in_x = jnp.where(lane_head == h, jnp.exp(col), in_x)
        dec_rows.append(jnp.broadcast_to(jnp.exp(lastv), (hd, SSD_STATE)))
    xdt = xs * dt_x
    xdt_b = xdt.astype(BF16)

    y = jnp.zeros((t, WIDTH), F32)
    h_in = state[...]
    xw_t = jnp.transpose(xdt * end_x).astype(BF16)
    new_states = []
    for gi in range(SSD_GROUPS):
        bg = bm[:, gi * SSD_STATE:(gi + 1) * SSD_STATE]
        cg = cm[:, gi * SSD_STATE:(gi + 1) * SSD_STATE]
        cbm = _dot_nt(cg, bg)
        rows = slice(gi * 2 * hd, (gi + 1) * 2 * hd)
        for r in range(SSD_HEADS // SSD_GROUPS):
            h = gi * (SSD_HEADS // SSD_GROUPS) + r
            seg = acs[:, 4 + h:5 + h] - acs_t[4 + h:5 + h, :]
            decay = jnp.exp(jnp.where(tril, seg, -jnp.inf))
            yd = _dot((cbm * decay).astype(BF16), xdt_b)
            y = y + jnp.where(lane_head == h, yd, 0.0)
        y_off = _dot_nt(cg, h_in[rows, :].astype(BF16))
        y_off = jnp.concatenate([y_off, y_off], axis=1)
        grp_lane = _iota((t, WIDTH), 1) >> 7
        y = y + jnp.where(grp_lane == gi, y_off * in_x, 0.0)
        new_states.append(_dot(xw_t[rows, :], bg))
    state[...] = h_in * jnp.concatenate(dec_rows, axis=0) + jnp.concatenate(new_states, axis=0)

    yg = (y + drow_ref[...] * xs) * _silu(z_ref[...])
    y_ref[...] = _rms(yg, g_ref[...])

    @pl.when(j == last)
    def _():
        hl_ref[0] = state[...]


def _ssd_prompt(p, cw, cb, dtb_row, a_row, d_row, g, *, n_seq):
    m = p.shape[0]
    t = SSD_CHUNK
    tiles = m // n_seq // t
    hp = SSD_HEADS * SSD_HEAD_DIM

    def unit(u):
        return pl.BlockSpec((t, WIDTH), lambda n, j: (n * tiles + j, u))

    def const(shape):
        return pl.BlockSpec(shape, lambda n, j: (0,) * len(shape))

    return pl.pallas_call(
        functools.partial(_ssd_prompt_body, t=t),
        grid=(n_seq, tiles),
        in_specs=[unit(U_XS), unit(U_BM), unit(U_CM), unit(U_Z),
                  pl.BlockSpec((t, SMALL_W), lambda n, j: (n * tiles + j, SMALL_BLK)),
                  const((SSD_CONV, SSD_CONV_DIM)), const((1, SSD_CONV_DIM)),
                  const((1, SMALL_W)), const((1, SMALL_W)), const((1, WIDTH)), const((1, WIDTH))],
        out_specs=[pl.BlockSpec((t, WIDTH), lambda n, j: (n * tiles + j, 0)),
                   pl.BlockSpec((1, hp, SSD_STATE), lambda n, j: (n, 0, 0))],
        out_shape=[jax.ShapeDtypeStruct((m, WIDTH), F32),
                   jax.ShapeDtypeStruct((n_seq, hp, SSD_STATE), F32)],
        scratch_shapes=[pltpu.VMEM((8 + t, SSD_CONV_DIM), F32),
                        pltpu.VMEM((hp, SSD_STATE), F32)],
        compiler_params=_cparams(("arbitrary", "arbitrary")),
        name="ssd_prompt",
    )(p, p, p, p, p, cw, cb, dtb_row, a_row, d_row, g)


def _conv_silu_body(ext_ref, w_ref, b_ref, o_ref, *, s, taps, rows):
    acc = b_ref[...]
    for k in range(taps):
        acc = acc + w_ref[k:k + 1, :] * ext_ref[k * s:k * s + rows, :]
    o_ref[...] = _silu(acc)


def _conv_silu(ext, w, b, *, s, rows):
    taps = w.shape[0]
    c = ext.shape[1]
    return pl.pallas_call(
        functools.partial(_conv_silu_body, s=s, taps=taps, rows=rows),
        out_shape=jax.ShapeDtypeStruct((rows, c), F32),
        compiler_params=pltpu.CompilerParams(vmem_limit_bytes=VMEM_LIMIT),
        name="conv_silu",
    )(ext, w, b)


def _ssd_step_body(x_ref, b_ref, c_ref, dtr_ref, dtb_ref, a_ref, h0_ref, y_ref, hl_ref, *, n_tok):
    dt = _softplus(dtr_ref[...] + dtb_ref[...])
    da = dt * a_ref[...]
    h = h0_ref[...]
    lane_t = _iota(y_ref.shape, 2)
    y = jnp.zeros(y_ref.shape, F32)
    for t in range(n_tok):
        xdt = x_ref[:, :, t:t + 1] * dt[:, :, t:t + 1]
        h = h * jnp.exp(da[:, :, t:t + 1]) + xdt * b_ref[:, t:t + 1, :]
        yt = jnp.sum(h * c_ref[:, t:t + 1, :], axis=2, keepdims=True)
        y = jnp.where(lane_t == t, yt, y)
    y_ref[...] = y
    hl_ref[...] = h


def _ssd_step(x_bh, b_bh, c_bh, dtr_bh, dtb_bh, a_bh, h0, *, blk):
    bh, hd, n_tok = x_bh.shape
    ns = h0.shape[2]

    def spec(shape):
        return pl.BlockSpec((blk,) + shape, lambda i: (i, 0, 0))

    return pl.pallas_call(
        functools.partial(_ssd_step_body, n_tok=n_tok),
        grid=(bh // blk,),
        in_specs=[spec((hd, n_tok)), spec((n_tok, ns)), spec((n_tok, ns)), spec((1, n_tok)),
                  spec((1, 1)), spec((1, 1)), spec((hd, ns))],
        out_specs=[spec((hd, n_tok)), spec((hd, ns))],
        out_shape=[jax.ShapeDtypeStruct((bh, hd, n_tok), F32),
                   jax.ShapeDtypeStruct((bh, hd, ns), F32)],
        compiler_params=_cparams(("arbitrary",)),
        name="ssd_step",
    )(x_bh, b_bh, c_bh, dtr_bh, dtb_bh, a_bh, h0)


def _ssd_gate_body(y_ref, xs_ref, z_ref, d_ref, g_ref, o_ref):
    yg = (y_ref[...] + d_ref[...] * xs_ref[...]) * _silu(z_ref[...])
    o_ref[...] = _rms(yg, g_ref[...])


def _ssd_gate(y, xs, z, d_row, g):
    return pl.pallas_call(
        _ssd_gate_body,
        out_shape=jax.ShapeDtypeStruct(y.shape, F32),
        name="ssd_gate",
    )(y, xs, z, d_row, g)


def _merge_body(x_ref, g_ref, ya_ref, yb_ref, yc_ref, yd_ref, wg_ref, wb_ref, wo_ref, o_ref):
    x = x_ref[...]
    h = _rms(x, g_ref[...]).astype(BF16)
    merged = jnp.zeros(x.shape, F32)
    for i, y_ref in enumerate((ya_ref, yb_ref, yc_ref, yd_ref)):
        gate = _sigmoid(_dot(h, wg_ref[:, i * D_MODEL:(i + 1) * D_MODEL]))
        merged = merged + gate * _dot(y_ref[...].astype(BF16), wb_ref[i])
    o_ref[...] = x + _dot(merged.astype(BF16), wo_ref[...])


def _merge(x, g, ys, wg, wb, wo, tm):
    m = x.shape[0]
    row = lambda i: (i, 0)
    y_spec = pl.BlockSpec((tm, WIDTH), row)
    return pl.pallas_call(
        _merge_body,
        grid=(m // tm,),
        in_specs=[pl.BlockSpec((tm, D_MODEL), row),
                  pl.BlockSpec((1, D_MODEL), lambda i: (0, 0)),
                  y_spec, y_spec, y_spec, y_spec,
                  pl.BlockSpec((D_MODEL, N_BRANCH * D_MODEL), lambda i: (0, 0)),
                  pl.BlockSpec((N_BRANCH, WIDTH, D_MODEL), lambda i: (0, 0, 0)),
                  pl.BlockSpec((D_MODEL, D_MODEL), lambda i: (0, 0))],
        out_specs=pl.BlockSpec((tm, D_MODEL), row),
        out_shape=jax.ShapeDtypeStruct((m, D_MODEL), F32),
        compiler_params=_cparams(("arbitrary",)),
        name="merge",
    )(x, g, *ys, wg, wb, wo)


def _ffn_body(x_ref, g_ref, h0g_ref, h0u_ref, wug_ref, wuu_ref, cwg_ref, cwu_ref, cbg_ref, cbu_ref, wd_ref,
              o_ref, hg_ref, hu_ref, h2, acc, wkg, wku, histg, histu, *, s, tm, hr):
    j = pl.program_id(1)
    c = pl.program_id(2)
    last_c = pl.num_programs(2) - 1

    @pl.when(c == 0)
    def _():
        h2[...] = _rms(x_ref[...], g_ref[...]).astype(BF16)
        acc[...] = jnp.zeros_like(acc)

    @pl.when(j == 0)
    def _():
        histg[c] = h0g_ref[0]
        histu[c] = h0u_ref[0]

    def conv(wk, hist, wu_ref, cw_ref, cb_ref):
        wk[0:hr, :] = hist[c]
        wk[hr:hr + tm, :] = _dot(h2[...], wu_ref[...])
        out = cb_ref[...]
        for k in range(FFN_CONV):
            off = hr - (FFN_CONV - 1 - k) * s
            out = out + cw_ref[k:k + 1, :] * wk[off:off + tm, :]
        hist[c] = wk[tm:tm + hr, :]
        return out

    yg = conv(wkg, histg, wug_ref, cwg_ref, cbg_ref)
    yu = conv(wku, histu, wuu_ref, cwu_ref, cbu_ref)
    acc[...] += _dot((_silu(yg) * yu).astype(BF16), wd_ref[...])
    hg_ref[0, c] = histg[c]
    hu_ref[0, c] = histu[c]

    @pl.when(c == last_c)
    def _():
        o_ref[...] = x_ref[...] + acc[...]


def _ffn(x, g, h0, wu, cw, cb, wd, *, n_seq, s, tm, cwid):
    m = x.shape[0]
    tiles = m // n_seq // tm
    hr = h0.shape[1]
    nc = D_FF // cwid
    row = lambda n, j, c: (n * tiles + j, 0)
    hist_g = lambda n, j, c: (n, 0, c)
    hist_u = lambda n, j, c: (n, 0, nc + c)
    col_g = lambda n, j, c: (0, c)
    col_u = lambda n, j, c: (0, nc + c)
    hist_shape = jax.ShapeDtypeStruct((n_seq, nc, hr, cwid), F32)
    hist_out = pl.BlockSpec((1, nc, hr, cwid), lambda n, j, c: (n, 0, 0, 0))
    x_new, hg, hu = pl.pallas_call(
        functools.partial(_ffn_body, s=s, tm=tm, hr=hr),
        grid=(n_seq, tiles, nc),
        in_specs=[pl.BlockSpec((tm, D_MODEL), row),
                  pl.BlockSpec((1, D_MODEL), lambda n, j, c: (0, 0)),
                  pl.BlockSpec((1, hr, cwid), hist_g), pl.BlockSpec((1, hr, cwid), hist_u),
                  pl.BlockSpec((D_MODEL, cwid), col_g), pl.BlockSpec((D_MODEL, cwid), col_u),
                  pl.BlockSpec((FFN_CONV, cwid), col_g), pl.BlockSpec((FFN_CONV, cwid), col_u),
                  pl.BlockSpec((1, cwid), col_g), pl.BlockSpec((1, cwid), col_u),
                  pl.BlockSpec((cwid, D_MODEL), lambda n, j, c: (c, 0))],
        out_specs=[pl.BlockSpec((tm, D_MODEL), row), hist_out, hist_out],
        out_shape=[jax.ShapeDtypeStruct((m, D_MODEL), F32), hist_shape, hist_shape],
        scratch_shapes=[pltpu.VMEM((tm, D_MODEL), BF16),
                        pltpu.VMEM((tm, D_MODEL), F32),
                        pltpu.VMEM((hr + tm, cwid), F32),
                        pltpu.VMEM((hr + tm, cwid), F32),
                        pltpu.VMEM((nc, hr, cwid), F32),
                        pltpu.VMEM((nc, hr, cwid), F32)],
        compiler_params=_cparams(("arbitrary", "arbitrary", "arbitrary")),
        name="ffn",
    )(x, g, h0, h0, wu, wu, cw, cw, cb, cb, wd)
    to_rows = lambda a: jnp.swapaxes(a, 1, 2).reshape(n_seq, hr, D_FF)
    return x_new, jnp.concatenate([to_rows(hg), to_rows(hu)], axis=2)


def _final_norm_body(x_ref, g_ref, o_ref):
    o_ref[...] = _rms(x_ref[...], g_ref[...])


def _final_norm(x, g, tm):
    m, d = x.shape
    return pl.pallas_call(
        _final_norm_body,
        grid=(m // tm,),
        in_specs=[pl.BlockSpec((tm, d), lambda i: (i, 0)), pl.BlockSpec((1, d), lambda i: (0, 0))],
        out_specs=pl.BlockSpec((tm, d), lambda i: (i, 0)),
        out_shape=jax.ShapeDtypeStruct((m, d), F32),
        compiler_params=_cparams(("arbitrary",)),
        name="final_norm",
    )(x, g)


def _selection_matrices():
    d = FOX_HEAD_DIM
    selq = np.zeros((FOX_HEADS, WIDTH, 2 * d), np.float32)
    selk = np.zeros((FOX_HEADS, WIDTH, 2 * d), np.float32)
    selv = np.zeros((FOX_HEADS, WIDTH, 2 * d), np.float32)
    for h in range(FOX_HEADS):
        for i in range(d):
            selq[h, h * d + i, i] = d ** -0.5
            selk[h, h * d + i, i] = 1.0
            selv[h, h * d + i, (h % 2) * d + i] = 1.0
    return (jnp.asarray(selq, BF16), jnp.asarray(selk, BF16), jnp.asarray(selv, BF16))


def _decode_constants(n_pages):
    rep = np.zeros((n_pages * QROWS, n_pages * TOK_PAD), np.float32)
    later = np.zeros((n_pages * TOK_PAD, n_pages * TOK_PAD), np.float32)
    for pg in range(n_pages):
        for h in range(FOX_HEADS):
            for t in range(TOK_PAD):
                rep[pg * QROWS + h * TOK_PAD + t, pg * TOK_PAD + h] = 1.0
            for pg2 in range(pg + 1, n_pages):
                later[pg * TOK_PAD + h, pg2 * TOK_PAD + h] = 1.0
    return jnp.asarray(rep, BF16), jnp.asarray(later, BF16)


def _layer_params(l, norm1_g, w_in, cm_norm_g, cm_ws, cm_b, pool_w, pool_scale, fox_bf, ssd_conv_w, ssd_conv_b,
                  ssd_dt_bias, ssd_a_log, ssd_d, ssd_norm_g, w_branch, w_out, norm2_g, ffn_up, ffn_conv_w,
                  ffn_conv_b, ffn_down, n_dec):
    w = w_in[l]
    o_f = 2 * WIDTH + WIDTH + 3 * WIDTH
    o_z = o_f + FOX_HEADS
    o_xbc = o_z + WIDTH
    o_dt = o_xbc + SSD_CONV_DIM
    o_gate = o_dt + SSD_HEADS
    w_p = jnp.concatenate([w[:, :o_f], w[:, o_z:o_dt], w[:, o_f:o_z], w[:, o_dt:o_gate],
                           jnp.zeros((D_MODEL, SMALL_W - FOX_HEADS - SSD_HEADS), F32)], axis=1).astype(BF16)
    causal = jnp.tril(jnp.ones((CM_CHUNK, CM_CHUNK), F32))
    ws = cm_ws[l] * causal
    gd = WIDTH // CM_GROUPS
    n_tok = 4
    eye = jnp.eye(n_dec, dtype=F32)
    pd = POOL_WINDOWS.__len__()
    pw = pool_w[l].astype(F32)
    pool_bd = jnp.zeros((WIDTH, WIDTH), F32)
    pg = WIDTH // pd
    for gi in range(pd):
        pool_bd = pool_bd.at[gi * pg:(gi + 1) * pg, gi * pg:(gi + 1) * pg].set(pw[gi])
    lanes = jnp.arange(SMALL_W)
    head_of_lane = jnp.clip(lanes - FOX_HEADS, 0, SSD_HEADS - 1)
    dt_lane = (lanes >= FOX_HEADS) & (lanes < FOX_HEADS + SSD_HEADS)
    a = -jnp.exp(ssd_a_log[l].astype(F32))
    return dict(
        g1=norm1_g[l][None, :],
        w_p=w_p,
        w_gate=w[:, o_gate:].astype(BF16),
        cm_g=cm_norm_g[l][None, :],
        ws_prompt=ws.astype(BF16),
        cmb_prompt=jnp.repeat(cm_b[l].T, gd, axis=1),
        ws_sample=jnp.stack([jnp.kron(ws[gi, :n_tok, :n_tok], eye) for gi in range(CM_GROUPS)]).astype(BF16),
        cmb_sample=jnp.repeat(jnp.repeat(cm_b[l].T[:n_tok], n_dec, axis=0), gd, axis=1),
        pool_bd=pool_bd.astype(BF16),
        pool_scale=pool_scale[l][None, :].astype(F32),
        bf_row=jnp.pad(fox_bf[l], (0, SMALL_W - FOX_HEADS))[None, :],
        bf_col=jnp.broadcast_to(jnp.repeat(fox_bf[l], TOK_PAD)[:, None], (QROWS, PAGE_SIZE)),
        conv_w=ssd_conv_w[l], conv_b=ssd_conv_b[l][None, :],
        dtb_row=jnp.where(dt_lane, ssd_dt_bias[l][head_of_lane], 0.0)[None, :],
        a_row=jnp.where(dt_lane, a[head_of_lane], 0.0)[None, :],
        d_row=jnp.repeat(ssd_d[l].astype(F32), SSD_HEAD_DIM)[None, :],
        dtb_bh=jnp.tile(ssd_dt_bias[l], n_dec)[:, None, None],
        a_bh=jnp.tile(a, n_dec)[:, None, None],
        ssd_g=ssd_norm_g[l][None, :],
        w_branch=w_branch[l].astype(BF16),
        w_out=w_out[l].astype(BF16),
        g2=norm2_g[l][None, :],
        ffn_up=ffn_up[l].astype(BF16),
        ffn_cw=ffn_conv_w[l], ffn_cb=ffn_conv_b[l][None, :],
        ffn_down=ffn_down[l].astype(BF16),
    )


def _prompt_layer(x, w, consts, n_seq):
    m = x.shape[0]
    seq_len = m // n_seq
    selq, selk, selv = consts["sel"]
    p = _proj(x, w["g1"], w["w_p"], tm=512)
    y_a, _ = _mixa(p, w["ws_prompt"], w["cmb_prompt"], w["cm_g"], t=CM_CHUNK, tm=512)
    y_b = _pool(p, U_POOL, jnp.zeros((n_seq, POOL_PAD_STEPS, WIDTH), F32), w["pool_bd"], w["pool_scale"],
                n_seq=n_seq, s=1, tm=512, pos0=0)
    lf, qa, ka, va = _fox_prep(p, w["bf_row"], selq, selk, selv, n_seq=n_seq, tm=512)
    y_c = _fox_flash(qa, ka, va, tq=512)
    y_d, ssd_last = _ssd_prompt(p, w["conv_w"], w["conv_b"], w["dtb_row"], w["a_row"], w["d_row"], w["ssd_g"],
                                n_seq=n_seq)
    x = _merge(x, w["g1"], (y_a, y_b, y_c, y_d), w["w_gate"], w["w_branch"], w["w_out"], tm=256)
    x, ffn_hist = _ffn(x, w["g2"], jnp.zeros((n_seq, 8, 2 * D_FF), F32), w["ffn_up"], w["ffn_cw"],
                       w["ffn_cb"], w["ffn_down"], n_seq=n_seq, s=1, tm=512, cwid=D_FF // 2)
    p3 = p.reshape(n_seq, seq_len, P_COLS)
    unit = lambda u: p3[:, :, u * WIDTH:(u + 1) * WIDTH]
    state = dict(
        k=unit(U_K).reshape(n_seq, seq_len, FOX_HEADS, FOX_HEAD_DIM),
        v=unit(U_V).reshape(n_seq, seq_len, FOX_HEADS, FOX_HEAD_DIM),
        logf=lf.reshape(n_seq, seq_len, SMALL_W)[:, :, :FOX_HEADS],
        pool=unit(U_POOL)[:, -POOL_HIST:],
        ssd_conv=p3[:, -(SSD_CONV - 1):, U_XS * WIDTH:U_XS * WIDTH + SSD_CONV_DIM],
        ssd=ssd_last.reshape(n_seq, SSD_HEADS, SSD_HEAD_DIM, SSD_STATE),
        ffn_conv=ffn_hist[:, -(FFN_CONV - 1):],
    )
    return x, state


def _to_time_major(a):
    return jnp.swapaxes(a, 0, 1).reshape((a.shape[0] * a.shape[1],) + a.shape[2:])


def _from_time_major(a, n_b):
    return jnp.swapaxes(a.reshape((a.shape[0] // n_b, n_b) + a.shape[1:]), 0, 1)


def _sample_layer(x, w, consts, layer, past, page_table, n_b, past_len):
    m = x.shape[0]
    n_tok = m // n_b
    p = _proj(x, w["g1"], w["w_p"], tm=m)
    unit = lambda u: p[:, u * WIDTH:(u + 1) * WIDTH]
    small = p[:, 10 * WIDTH:]

    y_a, va = _mixa(p, w["ws_sample"], w["cmb_sample"], w["cm_g"], t=m, tm=m)

    pool_new = unit(U_POOL)
    pool_hist = jnp.concatenate([jnp.zeros(((POOL_PAD_STEPS - POOL_HIST) * n_b, WIDTH), F32),
                                 _to_time_major(past["pool"])], axis=0)
    y_b = _pool(pool_new, 0, pool_hist[None], w["pool_bd"], w["pool_scale"],
                n_seq=1, s=n_b, tm=m, pos0=past_len)
    new_pool = jnp.concatenate([past["pool"], _from_time_major(pool_new, n_b)], axis=1)[:, -POOL_HIST:]

    pad_tok = lambda a: jnp.pad(_from_time_major(a, n_b), ((0, 0), (0, TOK_PAD - n_tok), (0, 0)))
    f_raw = _from_time_major(small[:, :FOX_HEADS], n_b)
    fr_lane = jnp.pad(jnp.swapaxes(f_raw, 1, 2), ((0, 0), (0, 0), (0, PAGE_SIZE - n_tok)))
    fr_lane = jnp.repeat(fr_lane, TOK_PAD, axis=1)
    rep, later = consts["decode"]
    y_c8 = _fox_decode(layer, page_table, pad_tok(unit(U_Q)), pad_tok(unit(U_K)), pad_tok(unit(U_V)),
                       fr_lane, w["bf_col"], rep, later, past["cache_k"], past["cache_v"], past["cache_lft"])
    y_c = _to_time_major(y_c8[:, :n_tok])
    logf = _log_sigmoid_rows(small, w["bf_row"])[:, :FOX_HEADS]

    xbc_new = p[:, U_XS * WIDTH:U_XS * WIDTH + SSD_CONV_DIM]
    xbc_ext = jnp.concatenate([_to_time_major(past["ssd_conv"]), xbc_new], axis=0)
    act = _conv_silu(xbc_ext, w["conv_w"], w["conv_b"], s=n_b, rows=m)
    xs = act[:, :WIDTH]
    rep_heads = lambda a: jnp.repeat(_from_time_major(a, n_b).reshape(n_b, n_tok, SSD_GROUPS, SSD_STATE),
                                     SSD_HEADS // SSD_GROUPS, axis=2)
    to_bh = lambda a: jnp.transpose(a, (0, 2, 1, 3)).reshape(n_b * SSD_HEADS, n_tok, SSD_STATE)
    x_bh = jnp.transpose(_from_time_major(xs, n_b).reshape(n_b, n_tok, SSD_HEADS, SSD_HEAD_DIM),
                         (0, 2, 3, 1)).reshape(n_b * SSD_HEADS, SSD_HEAD_DIM, n_tok)
    dtr = _from_time_major(small[:, FOX_HEADS:FOX_HEADS + SSD_HEADS], n_b)
    dtr_bh = jnp.swapaxes(dtr, 1, 2).reshape(n_b * SSD_HEADS, 1, n_tok)
    h0 = past["ssd"].reshape(n_b * SSD_HEADS, SSD_HEAD_DIM, SSD_STATE)
    y_bh, h_last = _ssd_step(x_bh, to_bh(rep_heads(act[:, WIDTH:2 * WIDTH])),
                             to_bh(rep_heads(act[:, 2 * WIDTH:])), dtr_bh, w["dtb_bh"], w["a_bh"], h0, blk=32)
    y_s = _to_time_major(jnp.transpose(y_bh.reshape(n_b, SSD_HEADS, SSD_HEAD_DIM, n_tok),
                                       (0, 3, 1, 2)).reshape(n_b, n_tok, WIDTH))
    y_d = _ssd_gate(y_s, xs, unit(U_Z), w["d_row"], w["ssd_g"])

    x = _merge(x, w["g1"], (y_a, y_b, y_c, y_d), w["w_gate"], w["w_branch"], w["w_out"], tm=min(m, 256))
    ffn_h0 = _to_time_major(past["ffn_conv"])[None]
    x, ffn_hist = _ffn(x, w["g2"], ffn_h0, w["ffn_up"], w["ffn_cw"], w["ffn_cb"], w["ffn_down"],
                       n_seq=1, s=n_b, tm=m, cwid=D_FF // 2)
    state = dict(
        k=_from_time_major(unit(U_K), n_b).reshape(n_b, n_tok, FOX_HEADS, FOX_HEAD_DIM),
        v=_from_time_major(unit(U_V), n_b).reshape(n_b, n_tok, FOX_HEADS, FOX_HEAD_DIM),
        logf=_from_time_major(logf, n_b),
        chunk_v=_from_time_major(va, n_b),
        pool=new_pool,
        ssd_conv=_from_time_major(xbc_ext[-(SSD_CONV - 1) * n_b:], n_b),
        ssd=h_last.reshape(n_b, SSD_HEADS, SSD_HEAD_DIM, SSD_STATE),
        ffn_conv=_from_time_major(ffn_hist[0], n_b),
    )
    return x, state


def _log_sigmoid_body(x_ref, b_ref, o_ref):
    o_ref[...] = _log_sigmoid(x_ref[...] + b_ref[...])


def _log_sigmoid_rows(small, bf_row):
    return pl.pallas_call(
        _log_sigmoid_body,
        out_shape=jax.ShapeDtypeStruct(small.shape, F32),
        name="log_forget",
    )(small, bf_row)


def kernel(x_prompt, x_sample, cache_k, cache_v, cache_logf, page_table, state_pool, state_ssd_conv, state_ssd,
           state_ffn_conv, norm1_g, w_in, cm_norm_g, cm_ws, cm_b, pool_w, pool_scale, fox_bf, ssd_conv_w,
           ssd_conv_b, ssd_dt_bias, ssd_a_log, ssd_d, ssd_norm_g, w_branch, w_out, norm2_g, ffn_up, ffn_conv_w,
           ffn_conv_b, ffn_down, final_norm_g):
    n_seq, seq_len, _ = x_prompt.shape
    n_b, n_tok, _ = x_sample.shape
    depth = w_in.shape[0]
    n_pages = page_table.shape[1]
    past_len = n_pages * cache_k.shape[2]
    n_phys = cache_k.shape[1]

    consts = dict(sel=_selection_matrices(), decode=_decode_constants(n_pages))
    ck = cache_k.reshape(depth, n_phys, PAGE_SIZE, WIDTH)
    cv = cache_v.reshape(depth, n_phys, PAGE_SIZE, WIDTH)
    clt = jnp.pad(jnp.swapaxes(cache_logf, 2, 3), ((0, 0), (0, 0), (0, TOK_PAD - FOX_HEADS), (0, 0)))

    xp = x_prompt.reshape(n_seq * seq_len, D_MODEL)
    xs = _to_time_major(x_sample)
    st_p, st_s = [], []
    for l in range(depth):
        w = _layer_params(l, norm1_g, w_in, cm_norm_g, cm_ws, cm_b, pool_w, pool_scale, fox_bf, ssd_conv_w,
                          ssd_conv_b, ssd_dt_bias, ssd_a_log, ssd_d, ssd_norm_g, w_branch, w_out, norm2_g,
                          ffn_up, ffn_conv_w, ffn_conv_b, ffn_down, n_b)
        past = dict(cache_k=ck, cache_v=cv, cache_lft=clt, pool=state_pool[l], ssd_conv=state_ssd_conv[l],
                    ssd=state_ssd[l], ffn_conv=state_ffn_conv[l])
        xp, sp = _prompt_layer(xp, w, consts, n_seq)
        xs, ss = _sample_layer(xs, w, consts, l, past, page_table, n_b, past_len)
        st_p.append(sp)
        st_s.append(ss)
    gfin = final_norm_g[None, :]
    y_prompt = _final_norm(xp, gfin, tm=512).reshape(n_seq, seq_len, D_MODEL)
    y_sample = _from_time_major(_final_norm(xs, gfin, tm=xs.shape[0]), n_b)

    def stacked(states, name):
        return jnp.stack([s[name] for s in states], axis=0)

    return (y_prompt, y_sample,
            stacked(st_p, "k"), stacked(st_p, "v"), stacked(st_p, "logf"),
            stacked(st_s, "k"), stacked(st_s, "v"), stacked(st_s, "logf"),
            stacked(st_s, "chunk_v"),
            stacked(st_p, "pool"), stacked(st_s, "pool"),
            stacked(st_p, "ssd_conv"), stacked(st_s, "ssd_conv"),
            stacked(st_p, "ssd"), stacked(st_s, "ssd"),
            stacked(st_p, "ffn_conv"), stacked(st_s, "ffn_conv"))
```

```python
import functools
import math

import numpy as np
import jax
import jax.numpy as jnp
from jax import lax
from jax.experimental import pallas as pl
from jax.experimental.pallas import tpu as pltpu

F32 = jnp.float32
BF16 = jnp.bfloat16

D_MODEL = 1024
WIDTH = 256
PAGE_SIZE = 128
CM_GROUPS = 4
CM_CHUNK = 128
POOL_WINDOWS = (2, 4, 8, 16)
POOL_HIST = 15
POOL_PAD_STEPS = 32
FOX_HEADS = 4
FOX_HEAD_DIM = 64
SSD_HEADS = 4
SSD_HEAD_DIM = 64
SSD_GROUPS = 2
SSD_STATE = 128
SSD_CONV = 4
SSD_CHUNK = 128
SSD_CONV_DIM = 768
D_FF = 2816
FFN_CONV = 3
N_BRANCH = 4
EPS = 1e-6
SMALL_W = 128
P_COLS = 10 * WIDTH + SMALL_W
SMALL_BLK = 10 * WIDTH // SMALL_W
U_AU, U_AV, U_POOL, U_Q, U_K, U_V, U_Z, U_XS, U_BM, U_CM = range(10)

VMEM_LIMIT = 56 * 1024 * 1024


def _cparams(sem):
    return pltpu.CompilerParams(dimension_semantics=sem, vmem_limit_bytes=VMEM_LIMIT)


def _sigmoid(x):
    return 1.0 / (1.0 + jnp.exp(-x))


def _silu(x):
    return x * _sigmoid(x)


def _softplus(x):
    return jnp.maximum(x, 0.0) + jnp.log(1.0 + jnp.exp(-jnp.abs(x)))


def _log_sigmoid(x):
    return -_softplus(-x)


def _gelu_tanh(x):
    return 0.5 * x * (1.0 + jnp.tanh(math.sqrt(2.0 / math.pi) * (x + 0.044715 * (x * x * x))))


def _rms(x, g):
    return x * lax.rsqrt(jnp.mean(x * x, axis=-1, keepdims=True) + EPS) * g


def _split3(x):
    hi = x.astype(BF16)
    r1 = x - hi.astype(F32)
    mid = r1.astype(BF16)
    lo = (r1 - mid.astype(F32)).astype(BF16)
    return hi, mid, lo


def _dot(a, b):
    return jnp.dot(a, b, preferred_element_type=F32)


def _dot_nt(a, b):
    return lax.dot_general(a, b, (((1,), (1,)), ((), ())), preferred_element_type=F32)


def _ones_dot_left(ones_bf16, x):
    hi, mid, lo = _split3(x)
    return _dot(ones_bf16, hi) + _dot(ones_bf16, mid) + _dot(ones_bf16, lo)


def _ones_dot_right(x, ones_bf16):
    hi, mid, lo = _split3(x)
    return _dot(hi, ones_bf16) + _dot(mid, ones_bf16) + _dot(lo, ones_bf16)


def _iota(shape, dim):
    return lax.broadcasted_iota(jnp.int32, shape, dim)


def _proj_body(x_ref, g_ref, w_ref, o_ref):
    h = _rms(x_ref[...], g_ref[...]).astype(BF16)
    o_ref[...] = _dot(h, w_ref[...])


def _proj(x, g, w, tm):
    m, d = x.shape
    n = w.shape[1]
    return pl.pallas_call(
        _proj_body,
        grid=(m // tm,),
        in_specs=[pl.BlockSpec((tm, d), lambda i: (i, 0)),
                  pl.BlockSpec((1, d), lambda i: (0, 0)),
                  pl.BlockSpec((d, n), lambda i: (0, 0))],
        out_specs=pl.BlockSpec((tm, n), lambda i: (i, 0)),
        out_shape=jax.ShapeDtypeStruct((m, n), F32),
        compiler_params=_cparams(("arbitrary",)),
        name="proj",
    )(x, g, w)


def _mixa_body(u_ref, v_ref, wm_ref, b_ref, g_ref, y_ref, va_ref, *, t, n_chunks):
    lane_grp = _iota((t, WIDTH), 1) >> 6
    for c in range(n_chunks):
        rows = pl.ds(c * t, t)
        u = _gelu_tanh(u_ref[rows, :])
        va = _rms(_gelu_tanh(v_ref[rows, :]), g_ref[...])
        va_ref[rows, :] = va
        vab = va.astype(BF16)
        zero = jnp.zeros_like(vab)
        s = b_ref[...]
        for gi in range(CM_GROUPS):
            s = s + _dot(wm_ref[gi], jnp.where(lane_grp == gi, vab, zero))
        y_ref[rows, :] = u * s


def _mixa(p, wm, bias, g, t, tm):
    m = p.shape[0]
    body = functools.partial(_mixa_body, t=t, n_chunks=tm // t)
    return pl.pallas_call(
        body,
        grid=(m // tm,),
        in_specs=[pl.BlockSpec((tm, WIDTH), lambda i: (i, U_AU)),
                  pl.BlockSpec((tm, WIDTH), lambda i: (i, U_AV)),
                  pl.BlockSpec((CM_GROUPS, t, t), lambda i: (0, 0, 0)),
                  pl.BlockSpec((t, WIDTH), lambda i: (0, 0)),
                  pl.BlockSpec((1, WIDTH), lambda i: (0, 0))],
        out_specs=[pl.BlockSpec((tm, WIDTH), lambda i: (i, 0)),
                   pl.BlockSpec((tm, WIDTH), lambda i: (i, 0))],
        out_shape=[jax.ShapeDtypeStruct((m, WIDTH), F32),
                   jax.ShapeDtypeStruct((m, WIDTH), F32)],
        compiler_params=_cparams(("arbitrary",)),
        name="mixer_a",
    )(p, p, wm, bias, g)


def _pool_body(seq_ref, h0_ref, w_ref, scale_ref, y_ref, e1, e2, e4, e8, *, s, tm, pos0, steps_per_tile):
    j = pl.program_id(1)
    hr = POOL_PAD_STEPS * s
    tot = hr + tm

    @pl.when(j == 0)
    def _():
        e1[0:hr, :] = h0_ref[0]

    e1[hr:tot, :] = seq_ref[...]
    e2[8 * s:tot, :] = e1[8 * s:tot, :] + e1[7 * s:tot - s, :]
    e4[16 * s:tot, :] = e2[16 * s:tot, :] + e2[14 * s:tot - 2 * s, :]
    e8[24 * s:tot, :] = e4[24 * s:tot, :] + e4[20 * s:tot - 4 * s, :]
    cur = e1[hr:tot, :]
    s2 = e2[hr:tot, :]
    s4 = e4[hr:tot, :]
    s8 = e8[hr:tot, :]
    s16 = s8 + e8[hr - 8 * s:tot - 8 * s, :]
    lane_grp = _iota((tm, WIDTH), 1) >> 6
    pos = pos0 + j * steps_per_tile + (_iota((tm, WIDTH), 0) >> (s.bit_length() - 1))
    wsum = jnp.where(lane_grp == 0, s2, jnp.where(lane_grp == 1, s4, jnp.where(lane_grp == 2, s8, s16)))
    win = jnp.where(lane_grp == 0, 2, jnp.where(lane_grp == 1, 4, jnp.where(lane_grp == 2, 8, 16)))
    count = jnp.minimum(pos + 1, win).astype(F32)
    d = wsum / count - cur
    y_ref[...] = _dot(d.astype(BF16), w_ref[...]) * scale_ref[...]
    e1[0:hr, :] = e1[tm:tot, :]


def _pool(seq, col_blk, h0, w_bd, scale, *, n_seq, s, tm, pos0):
    m = seq.shape[0]
    tiles = m // n_seq // tm
    hr = POOL_PAD_STEPS * s
    body = functools.partial(_pool_body, s=s, tm=tm, pos0=pos0, steps_per_tile=tm // s)
    return pl.pallas_call(
        body,
        grid=(n_seq, tiles),
        in_specs=[pl.BlockSpec((tm, WIDTH), lambda n, j: (n * tiles + j, col_blk)),
                  pl.BlockSpec((1, hr, WIDTH), lambda n, j: (n, 0, 0)),
                  pl.BlockSpec((WIDTH, WIDTH), lambda n, j: (0, 0)),
                  pl.BlockSpec((1, WIDTH), lambda n, j: (0, 0))],
        out_specs=pl.BlockSpec((tm, WIDTH), lambda n, j: (n * tiles + j, 0)),
        out_shape=jax.ShapeDtypeStruct((m, WIDTH), F32),
        scratch_shapes=[pltpu.VMEM((hr + tm, WIDTH), F32) for _ in range(4)],
        compiler_params=_cparams(("arbitrary", "arbitrary")),
        name="mixer_b",
    )(seq, h0, w_bd, scale)


LOG2E = math.log2(math.e)


def _denominator_row(h):
    return (1 - h % 2) * FOX_HEAD_DIM


def _fox_prep_body(q_ref, k_ref, v_ref, sm_ref, bf_ref, selq_ref, selk_ref, selv_ref,
                   lf_ref, qa_ref, ka_ref, va_ref, carry, *, tm):
    j = pl.program_id(1)

    @pl.when(j == 0)
    def _():
        carry[...] = jnp.zeros_like(carry)

    logf = _log_sigmoid(sm_ref[...] + bf_ref[...])
    lf_ref[...] = logf
    tril = (_iota((tm, tm), 0) >= _iota((tm, tm), 1)).astype(BF16)
    c = _ones_dot_left(tril, logf) + carry[...]
    carry[...] = c[tm - 1:tm, :]
    c = c * LOG2E
    c_t = jnp.transpose(c)
    qb = (q_ref[...] * (FOX_HEAD_DIM ** -0.5 * LOG2E)).astype(BF16)
    kb = k_ref[...].astype(BF16)
    vb = v_ref[...].astype(BF16)
    d = FOX_HEAD_DIM
    lane = _iota((tm, 2 * d), 1)
    row = _iota((2 * d, tm), 0)

    def pieces(x):
        hi = x.astype(BF16).astype(F32)
        r1 = x - hi
        mid = r1.astype(BF16).astype(F32)
        return hi, mid, (r1 - mid).astype(BF16).astype(F32)

    for h in range(FOX_HEADS):
        hi, mid, lo = pieces(jnp.broadcast_to(c_t[h:h + 1, :], (2 * d, tm)))
        q_extra = jnp.where(row == d, hi, jnp.where(row == d + 1, mid, jnp.where(row == d + 2, lo,
                  jnp.where((row >= d + 3) & (row < d + 6), 1.0, 0.0))))
        qa_ref[0, h] = (_dot_nt(selq_ref[h], qb) + q_extra).astype(BF16)
        hi, mid, lo = pieces(jnp.broadcast_to(c[:, h:h + 1], (tm, 2 * d)))
        k_extra = jnp.where(lane == d + 3, -hi, jnp.where(lane == d + 4, -mid, jnp.where(lane == d + 5, -lo,
                  jnp.where((lane >= d) & (lane < d + 3), 1.0, 0.0))))
        ka_ref[0, h] = (_dot(kb, selk_ref[h]) + k_extra).astype(BF16)
        ones_row = jnp.where(row == _denominator_row(h), 1.0, 0.0)
        va_ref[0, h] = (_dot_nt(selv_ref[h], vb) + ones_row).astype(BF16)


def _fox_prep(p, bf_row, selq, selk, selv, *, n_seq, tm):
    m = p.shape[0]
    seq_len = m // n_seq
    tiles = seq_len // tm
    aug = jax.ShapeDtypeStruct((n_seq, FOX_HEADS, seq_len, 2 * FOX_HEAD_DIM), BF16)
    aug_spec = pl.BlockSpec((1, FOX_HEADS, tm, 2 * FOX_HEAD_DIM), lambda n, j: (n, 0, j, 0))
    aug_t = jax.ShapeDtypeStruct((n_seq, FOX_HEADS, 2 * FOX_HEAD_DIM, seq_len), BF16)
    aug_t_spec = pl.BlockSpec((1, FOX_HEADS, 2 * FOX_HEAD_DIM, tm), lambda n, j: (n, 0, 0, j))
    sel_spec = pl.BlockSpec((FOX_HEADS, WIDTH, 2 * FOX_HEAD_DIM), lambda n, j: (0, 0, 0))
    sel_t_spec = pl.BlockSpec((FOX_HEADS, 2 * FOX_HEAD_DIM, WIDTH), lambda n, j: (0, 0, 0))
    return pl.pallas_call(
        functools.partial(_fox_prep_body, tm=tm),
        grid=(n_seq, tiles),
        in_specs=[pl.BlockSpec((tm, WIDTH), lambda n, j: (n * tiles + j, U_Q)),
                  pl.BlockSpec((tm, WIDTH), lambda n, j: (n * tiles + j, U_K)),
                  pl.BlockSpec((tm, WIDTH), lambda n, j: (n * tiles + j, U_V)),
                  pl.BlockSpec((tm, SMALL_W), lambda n, j: (n * tiles + j, SMALL_BLK)),
                  pl.BlockSpec((1, SMALL_W), lambda n, j: (0, 0)),
                  sel_t_spec, sel_spec, sel_t_spec],
        out_specs=[pl.BlockSpec((tm, SMALL_W), lambda n, j: (n * tiles + j, 0)),
                   aug_t_spec, aug_spec, aug_t_spec],
        out_shape=[jax.ShapeDtypeStruct((m, SMALL_W), F32), aug_t, aug, aug_t],
        scratch_shapes=[pltpu.VMEM((1, SMALL_W), F32)],
        compiler_params=_cparams(("arbitrary", "arbitrary")),
        name="fox_prep",
    )(p, p, p, p, bf_row, selq, selk, selv)


def _fox_flash_body(qi_ref, kj_ref, q_ref, k_ref, v_ref, o_ref, m_ref, acc_ref, *, tq):
    step = pl.program_id(1)
    qi = qi_ref[step]
    kj = kj_ref[step]

    @pl.when(kj == 0)
    def _():
        m_ref[...] = jnp.full_like(m_ref, -jnp.inf)
        acc_ref[...] = jnp.zeros_like(acc_ref)

    def update(diagonal):
        scores = [_dot(k_ref[0, h], q_ref[0, h]) for h in range(FOX_HEADS)]
        for h in range(FOX_HEADS):
            s = scores[h]
            if diagonal:
                s = jnp.where(_iota((tq, tq), 0) <= _iota((tq, tq), 1), s, -jnp.inf)
            m_prev = m_ref[h]
            m_new = jnp.maximum(m_prev, jnp.max(s, axis=0, keepdims=True))
            alpha = jnp.exp2(m_prev - m_new)
            pexp = jnp.exp2(s - m_new).astype(BF16)
            acc_ref[h] = alpha * acc_ref[h] + _dot(v_ref[0, h], pexp)
            m_ref[h] = m_new

    @pl.when(kj < qi)
    def _():
        update(False)

    @pl.when(kj == qi)
    def _():
        update(True)
        o = []
        for h in range(FOX_HEADS):
            acc = acc_ref[h]
            r = _denominator_row(h)
            o.append(acc / acc[r:r + 1, :])
        lower = _iota(o[0].shape, 0) < FOX_HEAD_DIM
        pairs = [jnp.where(lower, o[0], o[1]), jnp.where(lower, o[2], o[3])]
        o_ref[...] = jnp.concatenate([jnp.transpose(pairs[0]), jnp.transpose(pairs[1])], axis=1)


def _fox_flash(qa_t, ka, va_t, *, tq):
    n_seq, _, seq_len, w = ka.shape
    nq = seq_len // tq
    qi = np.array([i for i in range(nq) for _ in range(i + 1)], np.int32)
    kj = np.array([j for i in range(nq) for j in range(i + 1)], np.int32)
    q_spec = pl.BlockSpec((1, FOX_HEADS, w, tq), lambda n, s, qi, kj: (n, 0, 0, qi[s]))
    k_spec = pl.BlockSpec((1, FOX_HEADS, tq, w), lambda n, s, qi, kj: (n, 0, kj[s], 0))
    v_spec = pl.BlockSpec((1, FOX_HEADS, w, tq), lambda n, s, qi, kj: (n, 0, 0, kj[s]))
    grid_spec = pltpu.PrefetchScalarGridSpec(
        num_scalar_prefetch=2,
        grid=(n_seq, len(qi)),
        in_specs=[q_spec, k_spec, v_spec],
        out_specs=pl.BlockSpec((tq, WIDTH), lambda n, s, qi, kj: (n * nq + qi[s], 0)),
        scratch_shapes=[pltpu.VMEM((FOX_HEADS, 1, tq), F32),
                        pltpu.VMEM((FOX_HEADS, w, tq), F32)],
    )
    return pl.pallas_call(
        functools.partial(_fox_flash_body, tq=tq),
        grid_spec=grid_spec,
        out_shape=jax.ShapeDtypeStruct((n_seq * seq_len, WIDTH), F32),
        compiler_params=_cparams(("arbitrary", "arbitrary")),
        name="fox_flash",
    )(jnp.asarray(qi), jnp.asarray(kj), qa_t, ka, va_t)


TOK_PAD = 8
HEAD_PAD = 8


def _fox_decode_body(pt_ref, *refs, n_pages, n_tok):
    q_ref, kn_ref, vn_ref, fr_ref, bfc_ref = refs[:5]
    k_refs = refs[5:5 + n_pages]
    v_refs = refs[5 + n_pages:5 + 2 * n_pages]
    lf_refs = refs[5 + 2 * n_pages:5 + 3 * n_pages]
    o_ref = refs[5 + 3 * n_pages]
    d = FOX_HEAD_DIM
    ps = PAGE_SIZE
    qrows = n_tok * HEAD_PAD

    q4 = q_ref[0] * (d ** -0.5)
    own = (_iota((HEAD_PAD, WIDTH), 1) >> 6) == _iota((HEAD_PAD, WIDTH), 0)
    qbd = jnp.concatenate([jnp.where(own, jnp.broadcast_to(q4[t:t + 1, :], (HEAD_PAD, WIDTH)), 0.0)
                           for t in range(n_tok)], axis=0).astype(BF16)

    triu = (_iota((ps, ps), 0) <= _iota((ps, ps), 1)).astype(BF16)
    ones = jnp.ones((ps, ps), BF16)
    lf_new = _log_sigmoid(fr_ref[0] + bfc_ref[...])
    newcum_lane = _ones_dot_right(lf_new, triu)
    row_t = _iota((qrows, ps), 0) >> 3
    lane_i = _iota((qrows, ps), 1)
    newcum_col = jnp.sum(jnp.where(lane_i == row_t, newcum_lane, 0.0), axis=1, keepdims=True)

    lst = jnp.concatenate([r[0, 0] for r in lf_refs], axis=0)
    incl = _ones_dot_right(lst, triu)
    tot = _ones_dot_right(lst, ones)
    later = jnp.zeros((HEAD_PAD, ps), F32)
    bias = [None] * n_pages
    for pg in reversed(range(n_pages)):
        rows = slice(pg * HEAD_PAD, (pg + 1) * HEAD_PAD)
        g = (tot[rows] - incl[rows]) + later
        bias[pg] = jnp.concatenate([g] * n_tok, axis=0) + newcum_col
        later = later + tot[rows]

    k_new = jnp.concatenate([kn_ref[0], jnp.zeros((ps - TOK_PAD, WIDTH), F32)], axis=0).astype(BF16)
    v_new = jnp.concatenate([vn_ref[0], jnp.zeros((ps - TOK_PAD, WIDTH), F32)], axis=0).astype(BF16)
    s_new = _dot_nt(qbd, k_new) + newcum_col - newcum_lane
    s_new = jnp.where(lane_i <= row_t, s_new, -jnp.inf)
    s_past = [_dot(qbd, k_refs[pg][0, 0].astype(BF16)) + bias[pg] for pg in range(n_pages)]

    m_lane = s_new
    for sp in s_past:
        m_lane = jnp.maximum(m_lane, sp)
    m = jnp.max(m_lane, axis=1, keepdims=True)
    p_new = jnp.exp(s_new - m)
    l_lane = p_new
    acc = _dot(p_new.astype(BF16), v_new)
    for pg in range(n_pages):
        pp = jnp.exp(s_past[pg] - m)
        l_lane = l_lane + pp
        acc = acc + _dot_nt(pp.astype(BF16), v_refs[pg][0, 0].astype(BF16))
    o = acc / jnp.sum(l_lane, axis=1, keepdims=True)
    keep = jnp.concatenate([own] * n_tok, axis=0)
    o_ref[0] = jnp.sum(jnp.where(keep, o, 0.0).reshape(n_tok, HEAD_PAD, WIDTH), axis=1)


def _fox_decode(layer, page_table, q, kn8, vn8, fr_lane, bf_col, cache_k, cache_v, cache_lft):
    n_b, n_pages = page_table.shape
    n_tok = q.shape[1]
    qrows = n_tok * HEAD_PAD
    ps = PAGE_SIZE

    def page_spec(pg, rows):
        return pl.BlockSpec((1, 1, rows, ps), lambda b, pt: (layer, pt[b, pg], 0, 0))

    tok_spec = pl.BlockSpec((1, TOK_PAD, WIDTH), lambda b, pt: (b, 0, 0))
    in_specs = [pl.BlockSpec((1, n_tok, WIDTH), lambda b, pt: (b, 0, 0)), tok_spec, tok_spec,
                pl.BlockSpec((1, qrows, ps), lambda b, pt: (b, 0, 0)),
                pl.BlockSpec((qrows, ps), lambda b, pt: (0, 0))]
    in_specs += [page_spec(pg, WIDTH) for pg in range(n_pages)]
    in_specs += [page_spec(pg, WIDTH) for pg in range(n_pages)]
    in_specs += [page_spec(pg, HEAD_PAD) for pg in range(n_pages)]
    grid_spec = pltpu.PrefetchScalarGridSpec(
        num_scalar_prefetch=1,
        grid=(n_b,),
        in_specs=in_specs,
        out_specs=pl.BlockSpec((1, n_tok, WIDTH), lambda b, pt: (b, 0, 0)),
    )
    args = [q, kn8, vn8, fr_lane, bf_col]
    args += [cache_k] * n_pages + [cache_v] * n_pages + [cache_lft] * n_pages
    return pl.pallas_call(
        functools.partial(_fox_decode_body, n_pages=n_pages, n_tok=n_tok),
        grid_spec=grid_spec,
        out_shape=jax.ShapeDtypeStruct((n_b, n_tok, WIDTH), F32),
        compiler_params=_cparams(("arbitrary",)),
        name="fox_decode",
    )(page_table, *args)


def _ssd_prompt_body(xs_ref, bm_ref, cm_ref, z_ref, sm_ref, cw_ref, cb_ref, dtb_ref, arow_ref, drow_ref, g_ref,
                     y_ref, hl_ref, ext, state, *, t):
    j = pl.program_id(1)
    last = pl.num_programs(1) - 1
    hd = SSD_HEAD_DIM
    w3 = SSD_CONV_DIM

    @pl.when(j == 0)
    def _():
        ext[0:8, :] = jnp.zeros((8, w3), F32)
        state[...] = jnp.zeros_like(state)

    ext[8:8 + t, 0:WIDTH] = xs_ref[...]
    ext[8:8 + t, WIDTH:2 * WIDTH] = bm_ref[...]
    ext[8:8 + t, 2 * WIDTH:w3] = cm_ref[...]
    conv = cb_ref[...]
    for k in range(SSD_CONV):
        conv = conv + cw_ref[k:k + 1, :] * ext[5 + k:5 + k + t, :]
    ext[0:8, :] = ext[t:t + 8, :]
    act = _silu(conv)
    xs = act[:, 0:WIDTH]
    bm = act[:, WIDTH:2 * WIDTH].astype(BF16)
    cm = act[:, 2 * WIDTH:w3].astype(BF16)

    dt = _softplus(sm_ref[...] + dtb_ref[...])
    da = dt * arow_ref[...]
    tril = (_iota((t, t), 0) >= _iota((t, t), 1))
    acs = _ones_dot_left(tril.astype(BF16), da)
    acs_t = jnp.transpose(acs)
    lane_head = _iota((t, WIDTH), 1) >> 6
    dt_x = jnp.zeros((t, WIDTH), F32)
    end_x = jnp.zeros((t, WIDTH), F32)
    in_x = jnp.zeros((t, WIDTH), F32)
    dec_rows = []
    for h in range(SSD_HEADS):
        col = acs[:, 4 + h:5 + h]
        lastv = acs[t - 1:t, 4 + h:5 + h]
        dt_x = jnp.where(lane_head == h, dt[:, 4 + h:5 + h], dt_x)
        end_x = jnp.where(lane_head == h, jnp.exp(lastv - col), end_x)
        in_x = jnp.where(lane_head == h, jnp.exp(col), in_x)
        dec_rows.append(jnp.broadcast_to(jnp.exp(lastv), (hd, SSD_STATE)))
    xdt = xs * dt_x
    xdt_b = xdt.astype(BF16)

    y = jnp.zeros((t, WIDTH), F32)
    h_in = state[...]
    xw_t = jnp.transpose(xdt * end_x).astype(BF16)
    new_states = []
    for gi in range(SSD_GROUPS):
        bg = bm[:, gi * SSD_STATE:(gi + 1) * SSD_STATE]
        cg = cm[:, gi * SSD_STATE:(gi + 1) * SSD_STATE]
        cbm = _dot_nt(cg, bg)
        rows = slice(gi * 2 * hd, (gi + 1) * 2 * hd)
        for r in range(SSD_HEADS // SSD_GROUPS):
            h = gi * (SSD_HEADS // SSD_GROUPS) + r
            seg = acs[:, 4 + h:5 + h] - acs_t[4 + h:5 + h, :]
            decay = jnp.exp(jnp.where(tril, seg, -jnp.inf))
            yd = _dot((cbm * decay).astype(BF16), xdt_b)
            y = y + jnp.where(lane_head == h, yd, 0.0)
        y_off = _dot_nt(cg, h_in[rows, :].astype(BF16))
        y_off = jnp.concatenate([y_off, y_off], axis=1)
        grp_lane = _iota((t, WIDTH), 1) >> 7
        y = y + jnp.where(grp_lane == gi, y_off * in_x, 0.0)
        new_states.append(_dot(xw_t[rows, :], bg))
    state[...] = h_in * jnp.concatenate(dec_rows, axis=0) + jnp.concatenate(new_states, axis=0)

    yg = (y + drow_ref[...] * xs) * _silu(z_ref[...])
    y_ref[...] = _rms(yg, g_ref[...])

    @pl.when(j == last)
    def _():
        hl_ref[0] = state[...]


def _ssd_prompt(p, cw, cb, dtb_row, a_row, d_row, g, *, n_seq):
    m = p.shape[0]
    t = SSD_CHUNK
    tiles = m // n_seq // t
    hp = SSD_HEADS * SSD_HEAD_DIM

    def unit(u):
        return pl.BlockSpec((t, WIDTH), lambda n, j: (n * tiles + j, u))

    def const(shape):
        return pl.BlockSpec(shape, lambda n, j: (0,) * len(shape))

    return pl.pallas_call(
        functools.partial(_ssd_prompt_body, t=t),
        grid=(n_seq, tiles),
        in_specs=[unit(U_XS), unit(U_BM), unit(U_CM), unit(U_Z),
                  pl.BlockSpec((t, SMALL_W), lambda n, j: (n * tiles + j, SMALL_BLK)),
                  const((SSD_CONV, SSD_CONV_DIM)), const((1, SSD_CONV_DIM)),
                  const((1, SMALL_W)), const((1, SMALL_W)), const((1, WIDTH)), const((1, WIDTH))],
        out_specs=[pl.BlockSpec((t, WIDTH), lambda n, j: (n * tiles + j, 0)),
                   pl.BlockSpec((1, hp, SSD_STATE), lambda n, j: (n, 0, 0))],
        out_shape=[jax.ShapeDtypeStruct((m, WIDTH), F32),
                   jax.ShapeDtypeStruct((n_seq, hp, SSD_STATE), F32)],
        scratch_shapes=[pltpu.VMEM((8 + t, SSD_CONV_DIM), F32),
                        pltpu.VMEM((hp, SSD_STATE), F32)],
        compiler_params=_cparams(("arbitrary", "arbitrary")),
        name="ssd_prompt",
    )(p, p, p, p, p, cw, cb, dtb_row, a_row, d_row, g)


def _conv_silu_body(ext_ref, w_ref, b_ref, o_ref, *, s, taps, rows):
    acc = b_ref[...]
    for k in range(taps):
        acc = acc + w_ref[k:k + 1, :] * ext_ref[k * s:k * s + rows, :]
    o_ref[...] = _silu(acc)


def _conv_silu(ext, w, b, *, s, rows):
    taps = w.shape[0]
    c = ext.shape[1]
    return pl.pallas_call(
        functools.partial(_conv_silu_body, s=s, taps=taps, rows=rows),
        out_shape=jax.ShapeDtypeStruct((rows, c), F32),
        compiler_params=pltpu.CompilerParams(vmem_limit_bytes=VMEM_LIMIT),
        name="conv_silu",
    )(ext, w, b)


def _ssd_step_body(x_ref, b_ref, c_ref, dtr_ref, dtb_ref, a_ref, h0_ref, y_ref, hl_ref, *, n_tok):
    dt = _softplus(dtr_ref[...] + dtb_ref[...])
    da = dt * a_ref[...]
    h = h0_ref[...]
    lane_t = _iota(y_ref.shape, 2)
    y = jnp.zeros(y_ref.shape, F32)
    for t in range(n_tok):
        xdt = x_ref[:, :, t:t + 1] * dt[:, :, t:t + 1]
        h = h * jnp.exp(da[:, :, t:t + 1]) + xdt * b_ref[:, t:t + 1, :]
        yt = jnp.sum(h * c_ref[:, t:t + 1, :], axis=2, keepdims=True)
        y = jnp.where(lane_t == t, yt, y)
    y_ref[...] = y
    hl_ref[...] = h


def _ssd_step(x_bh, b_bh, c_bh, dtr_bh, dtb_bh, a_bh, h0, *, blk):
    bh, hd, n_tok = x_bh.shape
    ns = h0.shape[2]

    def spec(shape):
        return pl.BlockSpec((blk,) + shape, lambda i: (i, 0, 0))

    return pl.pallas_call(
        functools.partial(_ssd_step_body, n_tok=n_tok),
        grid=(bh // blk,),
        in_specs=[spec((hd, n_tok)), spec((n_tok, ns)), spec((n_tok, ns)), spec((1, n_tok)),
                  spec((1, 1)), spec((1, 1)), spec((hd, ns))],
        out_specs=[spec((hd, n_tok)), spec((hd, ns))],
        out_shape=[jax.ShapeDtypeStruct((bh, hd, n_tok), F32),
                   jax.ShapeDtypeStruct((bh, hd, ns), F32)],
        compiler_params=_cparams(("arbitrary",)),
        name="ssd_step",
    )(x_bh, b_bh, c_bh, dtr_bh, dtb_bh, a_bh, h0)


def _ssd_gate_body(y_ref, xs_ref, z_ref, d_ref, g_ref, o_ref):
    yg = (y_ref[...] + d_ref[...] * xs_ref[...]) * _silu(z_ref[...])
    o_ref[...] = _rms(yg, g_ref[...])


def _ssd_gate(y, xs, z, d_row, g):
    return pl.pallas_call(
        _ssd_gate_body,
        out_shape=jax.ShapeDtypeStruct(y.shape, F32),
        name="ssd_gate",
    )(y, xs, z, d_row, g)


def _merge_body(x_ref, g_ref, ya_ref, yb_ref, yc_ref, yd_ref, wg_ref, wb_ref, wo_ref, o_ref):
    x = x_ref[...]
    h = _rms(x, g_ref[...]).astype(BF16)
    merged = jnp.zeros(x.shape, F32)
    for i, y_ref in enumerate((ya_ref, yb_ref, yc_ref, yd_ref)):
        gate = _sigmoid(_dot(h, wg_ref[:, i * D_MODEL:(i + 1) * D_MODEL]))
        merged = merged + gate * _dot(y_ref[...].astype(BF16), wb_ref[i])
    o_ref[...] = x + _dot(merged.astype(BF16), wo_ref[...])


def _merge(x, g, ys, wg, wb, wo, tm):
    m = x.shape[0]
    row = lambda i: (i, 0)
    y_spec = pl.BlockSpec((tm, WIDTH), row)
    return pl.pallas_call(
        _merge_body,
        grid=(m // tm,),
        in_specs=[pl.BlockSpec((tm, D_MODEL), row),
                  pl.BlockSpec((1, D_MODEL), lambda i: (0, 0)),
                  y_spec, y_spec, y_spec, y_spec,
                  pl.BlockSpec((D_MODEL, N_BRANCH * D_MODEL), lambda i: (0, 0)),
                  pl.BlockSpec((N_BRANCH, WIDTH, D_MODEL), lambda i: (0, 0, 0)),
                  pl.BlockSpec((D_MODEL, D_MODEL), lambda i: (0, 0))],
        out_specs=pl.BlockSpec((tm, D_MODEL), row),
        out_shape=jax.ShapeDtypeStruct((m, D_MODEL), F32),
        compiler_params=_cparams(("arbitrary",)),
        name="merge",
    )(x, g, *ys, wg, wb, wo)


def _ffn_body(x_ref, g_ref, h0g_ref, h0u_ref, wug_ref, wuu_ref, cwg_ref, cwu_ref, cbg_ref, cbu_ref, wd_ref,
              o_ref, hg_ref, hu_ref, h2, acc, wkg, wku, histg, histu, *, s, tm, hr):
    j = pl.program_id(1)
    c = pl.program_id(2)
    last_c = pl.num_programs(2) - 1

    @pl.when(c == 0)
    def _():
        h2[...] = _rms(x_ref[...], g_ref[...]).astype(BF16)
        acc[...] = jnp.zeros_like(acc)

    @pl.when(j == 0)
    def _():
        histg[c] = h0g_ref[0]
        histu[c] = h0u_ref[0]

    def conv(wk, hist, wu_ref, cw_ref, cb_ref):
        wk[0:hr, :] = hist[c]
        wk[hr:hr + tm, :] = _dot(h2[...], wu_ref[...])
        out = cb_ref[...]
        for k in range(FFN_CONV):
            off = hr - (FFN_CONV - 1 - k) * s
            out = out + cw_ref[k:k + 1, :] * wk[off:off + tm, :]
        hist[c] = wk[tm:tm + hr, :]
        return out

    yg = conv(wkg, histg, wug_ref, cwg_ref, cbg_ref)
    yu = conv(wku, histu, wuu_ref, cwu_ref, cbu_ref)
    acc[...] += _dot((_silu(yg) * yu).astype(BF16), wd_ref[...])
    hg_ref[0, c] = histg[c]
    hu_ref[0, c] = histu[c]

    @pl.when(c == last_c)
    def _():
        o_ref[...] = x_ref[...] + acc[...]


def _ffn(x, g, h0, wu, cw, cb, wd, *, n_seq, s, tm, cwid):
    m = x.shape[0]
    tiles = m // n_seq // tm
    hr = h0.shape[1]
    nc = D_FF // cwid
    row = lambda n, j, c: (n * tiles + j, 0)
    hist_g = lambda n, j, c: (n, 0, c)
    hist_u = lambda n, j, c: (n, 0, nc + c)
    col_g = lambda n, j, c: (0, c)
    col_u = lambda n, j, c: (0, nc + c)
    hist_shape = jax.ShapeDtypeStruct((n_seq, nc, hr, cwid), F32)
    hist_out = pl.BlockSpec((1, nc, hr, cwid), lambda n, j, c: (n, 0, 0, 0))
    x_new, hg, hu = pl.pallas_call(
        functools.partial(_ffn_body, s=s, tm=tm, hr=hr),
        grid=(n_seq, tiles, nc),
        in_specs=[pl.BlockSpec((tm, D_MODEL), row),
                  pl.BlockSpec((1, D_MODEL), lambda n, j, c: (0, 0)),
                  pl.BlockSpec((1, hr, cwid), hist_g), pl.BlockSpec((1, hr, cwid), hist_u),
                  pl.BlockSpec((D_MODEL, cwid), col_g), pl.BlockSpec((D_MODEL, cwid), col_u),
                  pl.BlockSpec((FFN_CONV, cwid), col_g), pl.BlockSpec((FFN_CONV, cwid), col_u),
                  pl.BlockSpec((1, cwid), col_g), pl.BlockSpec((1, cwid), col_u),
                  pl.BlockSpec((cwid, D_MODEL), lambda n, j, c: (c, 0))],
        out_specs=[pl.BlockSpec((tm, D_MODEL), row), hist_out, hist_out],
        out_shape=[jax.ShapeDtypeStruct((m, D_MODEL), F32), hist_shape, hist_shape],
        scratch_shapes=[pltpu.VMEM((tm, D_MODEL), BF16),
                        pltpu.VMEM((tm, D_MODEL), F32),
                        pltpu.VMEM((hr + tm, cwid), F32),
                        pltpu.VMEM((hr + tm, cwid), F32),
                        pltpu.VMEM((nc, hr, cwid), F32),
                        pltpu.VMEM((nc, hr, cwid), F32)],
        compiler_params=_cparams(("arbitrary", "arbitrary", "arbitrary")),
        name="ffn",
    )(x, g, h0, h0, wu, wu, cw, cw, cb, cb, wd)
    to_rows = lambda a: jnp.swapaxes(a, 1, 2).reshape(n_seq, hr, D_FF)
    return x_new, jnp.concatenate([to_rows(hg), to_rows(hu)], axis=2)


def _final_norm_body(x_ref, g_ref, o_ref):
    o_ref[...] = _rms(x_ref[...], g_ref[...])


def _final_norm(x, g, tm):
    m, d = x.shape
    return pl.pallas_call(
        _final_norm_body,
        grid=(m // tm,),
        in_specs=[pl.BlockSpec((tm, d), lambda i: (i, 0)), pl.BlockSpec((1, d), lambda i: (0, 0))],
        out_specs=pl.BlockSpec((tm, d), lambda i: (i, 0)),
        out_shape=jax.ShapeDtypeStruct((m, d), F32),
        compiler_params=_cparams(("arbitrary",)),
        name="final_norm",
    )(x, g)


def _selection_matrices():
    d = FOX_HEAD_DIM
    selq = np.zeros((FOX_HEADS, 2 * d, WIDTH), np.float32)
    selk = np.zeros((FOX_HEADS, WIDTH, 2 * d), np.float32)
    selv = np.zeros((FOX_HEADS, 2 * d, WIDTH), np.float32)
    for h in range(FOX_HEADS):
        for i in range(d):
            selq[h, i, h * d + i] = 1.0
            selk[h, h * d + i, i] = 1.0
            selv[h, (h % 2) * d + i, h * d + i] = 1.0
    return (jnp.asarray(selq, BF16), jnp.asarray(selk, BF16), jnp.asarray(selv, BF16))


def _layer_params(l, norm1_g, w_in, cm_norm_g, cm_ws, cm_b, pool_w, pool_scale, fox_bf, ssd_conv_w, ssd_conv_b,
                  ssd_dt_bias, ssd_a_log, ssd_d, ssd_norm_g, w_branch, w_out, norm2_g, ffn_up, ffn_conv_w,
                  ffn_conv_b, ffn_down, n_dec, n_tok):
    w = w_in[l]
    o_f = 2 * WIDTH + WIDTH + 3 * WIDTH
    o_z = o_f + FOX_HEADS
    o_xbc = o_z + WIDTH
    o_dt = o_xbc + SSD_CONV_DIM
    o_gate = o_dt + SSD_HEADS
    w_p = jnp.concatenate([w[:, :o_f], w[:, o_z:o_dt], w[:, o_f:o_z], w[:, o_dt:o_gate],
                           jnp.zeros((D_MODEL, SMALL_W - FOX_HEADS - SSD_HEADS), F32)], axis=1).astype(BF16)
    causal = jnp.tril(jnp.ones((CM_CHUNK, CM_CHUNK), F32))
    ws = cm_ws[l] * causal
    gd = WIDTH // CM_GROUPS
    eye = jnp.eye(n_dec, dtype=F32)
    pd = len(POOL_WINDOWS)
    pw = pool_w[l].astype(F32)
    pool_bd = jnp.zeros((WIDTH, WIDTH), F32)
    pg = WIDTH // pd
    for gi in range(pd):
        pool_bd = pool_bd.at[gi * pg:(gi + 1) * pg, gi * pg:(gi + 1) * pg].set(pw[gi])
    lanes = jnp.arange(SMALL_W)
    head_of_lane = jnp.clip(lanes - FOX_HEADS, 0, SSD_HEADS - 1)
    dt_lane = (lanes >= FOX_HEADS) & (lanes < FOX_HEADS + SSD_HEADS)
    a = -jnp.exp(ssd_a_log[l].astype(F32))
    return dict(
        g1=norm1_g[l][None, :],
        w_p=w_p,
        w_gate=w[:, o_gate:].astype(BF16),
        cm_g=cm_norm_g[l][None, :],
        ws_prompt=ws.astype(BF16),
        cmb_prompt=jnp.repeat(cm_b[l].T, gd, axis=1),
        ws_sample=jnp.stack([jnp.kron(ws[gi, :n_tok, :n_tok], eye) for gi in range(CM_GROUPS)]).astype(BF16),
        cmb_sample=jnp.repeat(jnp.repeat(cm_b[l].T[:n_tok], n_dec, axis=0), gd, axis=1),
        pool_bd=pool_bd.astype(BF16),
        pool_scale=pool_scale[l][None, :].astype(F32),
        bf_row=jnp.pad(fox_bf[l], (0, SMALL_W - FOX_HEADS))[None, :],
        bf_col=jnp.broadcast_to(jnp.tile(jnp.pad(fox_bf[l], (0, HEAD_PAD - FOX_HEADS)), n_tok)[:, None],
                                (n_tok * HEAD_PAD, PAGE_SIZE)),
        conv_w=ssd_conv_w[l], conv_b=ssd_conv_b[l][None, :],
        dtb_row=jnp.where(dt_lane, ssd_dt_bias[l][head_of_lane], 0.0)[None, :],
        a_row=jnp.where(dt_lane, a[head_of_lane], 0.0)[None, :],
        d_row=jnp.repeat(ssd_d[l].astype(F32), SSD_HEAD_DIM)[None, :],
        dtb_bh=jnp.tile(ssd_dt_bias[l], n_dec)[:, None, None],
        a_bh=jnp.tile(a, n_dec)[:, None, None],
        ssd_g=ssd_norm_g[l][None, :],
        w_branch=w_branch[l].astype(BF16),
        w_out=w_out[l].astype(BF16),
        g2=norm2_g[l][None, :],
        ffn_up=ffn_up[l].astype(BF16),
        ffn_cw=ffn_conv_w[l], ffn_cb=ffn_conv_b[l][None, :],
        ffn_down=ffn_down[l].astype(BF16),
    )


def _prompt_layer(x, w, consts, n_seq):
    m = x.shape[0]
    seq_len = m // n_seq
    selq, selk, selv = consts["sel"]
    p = _proj(x, w["g1"], w["w_p"], tm=512)
    y_a, _ = _mixa(p, w["ws_prompt"], w["cmb_prompt"], w["cm_g"], t=CM_CHUNK, tm=512)
    y_b = _pool(p, U_POOL, jnp.zeros((n_seq, POOL_PAD_STEPS, WIDTH), F32), w["pool_bd"], w["pool_scale"],
                n_seq=n_seq, s=1, tm=512, pos0=0)
    lf, qa, ka, va = _fox_prep(p, w["bf_row"], selq, selk, selv, n_seq=n_seq, tm=512)
    y_c = _fox_flash(qa, ka, va, tq=512)
    y_d, ssd_last = _ssd_prompt(p, w["conv_w"], w["conv_b"], w["dtb_row"], w["a_row"], w["d_row"], w["ssd_g"],
                                n_seq=n_seq)
    x = _merge(x, w["g1"], (y_a, y_b, y_c, y_d), w["w_gate"], w["w_branch"], w["w_out"], tm=256)
    x, ffn_hist = _ffn(x, w["g2"], jnp.zeros((n_seq, 8, 2 * D_FF), F32), w["ffn_up"], w["ffn_cw"],
                       w["ffn_cb"], w["ffn_down"], n_seq=n_seq, s=1, tm=512, cwid=D_FF // 2)
    p3 = p.reshape(n_seq, seq_len, P_COLS)
    unit = lambda u: p3[:, :, u * WIDTH:(u + 1) * WIDTH]
    state = dict(
        k=unit(U_K).reshape(n_seq, seq_len, FOX_HEADS, FOX_HEAD_DIM),
        v=unit(U_V).reshape(n_seq, seq_len, FOX_HEADS, FOX_HEAD_DIM),
        logf=lf.reshape(n_seq, seq_len, SMALL_W)[:, :, :FOX_HEADS],
        pool=unit(U_POOL)[:, -POOL_HIST:],
        ssd_conv=p3[:, -(SSD_CONV - 1):, U_XS * WIDTH:U_XS * WIDTH + SSD_CONV_DIM],
        ssd=ssd_last.reshape(n_seq, SSD_HEADS, SSD_HEAD_DIM, SSD_STATE),
        ffn_conv=ffn_hist[:, -(FFN_CONV - 1):],
    )
    return x, state


def _to_time_major(a):
    return jnp.swapaxes(a, 0, 1).reshape((a.shape[0] * a.shape[1],) + a.shape[2:])


def _from_time_major(a, n_b):
    return jnp.swapaxes(a.reshape((a.shape[0] // n_b, n_b) + a.shape[1:]), 0, 1)


def _sample_layer(x, w, consts, layer, past, page_table, n_b, past_len):
    m = x.shape[0]
    n_tok = m // n_b
    p = _proj(x, w["g1"], w["w_p"], tm=m)
    unit = lambda u: p[:, u * WIDTH:(u + 1) * WIDTH]
    small = p[:, 10 * WIDTH:]

    y_a, va = _mixa(p, w["ws_sample"], w["cmb_sample"], w["cm_g"], t=m, tm=m)

    pool_new = unit(U_POOL)
    pool_hist = jnp.concatenate([jnp.zeros(((POOL_PAD_STEPS - POOL_HIST) * n_b, WIDTH), F32),
                                 _to_time_major(past["pool"])], axis=0)
    y_b = _pool(pool_new, 0, pool_hist[None], w["pool_bd"], w["pool_scale"],
                n_seq=1, s=n_b, tm=m, pos0=past_len)
    new_pool = jnp.concatenate([past["pool"], _from_time_major(pool_new, n_b)], axis=1)[:, -POOL_HIST:]

    pad_tok = lambda a: jnp.pad(_from_time_major(a, n_b), ((0, 0), (0, TOK_PAD - n_tok), (0, 0)))
    f_raw = _from_time_major(small[:, :FOX_HEADS], n_b)
    fr_lane = jnp.pad(jnp.swapaxes(f_raw, 1, 2),
                      ((0, 0), (0, HEAD_PAD - FOX_HEADS), (0, PAGE_SIZE - n_tok)))
    fr_lane = jnp.tile(fr_lane, (1, n_tok, 1))
    y_c = _fox_decode(layer, page_table, _from_time_major(unit(U_Q), n_b), pad_tok(unit(U_K)),
                      pad_tok(unit(U_V)), fr_lane, w["bf_col"], past["cache_k"], past["cache_v"],
                      past["cache_lft"])
    y_c = _to_time_major(y_c)
    logf = _log_sigmoid_rows(small, w["bf_row"])[:, :FOX_HEADS]

    xbc_new = p[:, U_XS * WIDTH:U_XS * WIDTH + SSD_CONV_DIM]
    xbc_ext = jnp.concatenate([_to_time_major(past["ssd_conv"]), xbc_new], axis=0)
    act = _conv_silu(xbc_ext, w["conv_w"], w["conv_b"], s=n_b, rows=m)
    xs = act[:, :WIDTH]
    rep_heads = lambda a: jnp.repeat(_from_time_major(a, n_b).reshape(n_b, n_tok, SSD_GROUPS, SSD_STATE),
                                     SSD_HEADS // SSD_GROUPS, axis=2)
    to_bh = lambda a: jnp.transpose(a, (0, 2, 1, 3)).reshape(n_b * SSD_HEADS, n_tok, SSD_STATE)
    x_bh = jnp.transpose(_from_time_major(xs, n_b).reshape(n_b, n_tok, SSD_HEADS, SSD_HEAD_DIM),
                         (0, 2, 3, 1)).reshape(n_b * SSD_HEADS, SSD_HEAD_DIM, n_tok)
    dtr = _from_time_major(small[:, FOX_HEADS:FOX_HEADS + SSD_HEADS], n_b)
    dtr_bh = jnp.swapaxes(dtr, 1, 2).reshape(n_b * SSD_HEADS, 1, n_tok)
    h0 = past["ssd"].reshape(n_b * SSD_HEADS, SSD_HEAD_DIM, SSD_STATE)
    y_bh, h_last = _ssd_step(x_bh, to_bh(rep_heads(act[:, WIDTH:2 * WIDTH])),
                             to_bh(rep_heads(act[:, 2 * WIDTH:])), dtr_bh, w["dtb_bh"], w["a_bh"], h0, blk=32)
    y_s = _to_time_major(jnp.transpose(y_bh.reshape(n_b, SSD_HEADS, SSD_HEAD_DIM, n_tok),
                                       (0, 3, 1, 2)).reshape(n_b, n_tok, WIDTH))
    y_d = _ssd_gate(y_s, xs, unit(U_Z), w["d_row"], w["ssd_g"])

    x = _merge(x, w["g1"], (y_a, y_b, y_c, y_d), w["w_gate"], w["w_branch"], w["w_out"], tm=min(m, 256))
    ffn_h0 = _to_time_major(past["ffn_conv"])[None]
    x, ffn_hist = _ffn(x, w["g2"], ffn_h0, w["ffn_up"], w["ffn_cw"], w["ffn_cb"], w["ffn_down"],
                       n_seq=1, s=n_b, tm=m, cwid=D_FF // 2)
    state = dict(
        k=_from_time_major(unit(U_K), n_b).reshape(n_b, n_tok, FOX_HEADS, FOX_HEAD_DIM),
        v=_from_time_major(unit(U_V), n_b).reshape(n_b, n_tok, FOX_HEADS, FOX_HEAD_DIM),
        logf=_from_time_major(logf, n_b),
        chunk_v=_from_time_major(va, n_b),
        pool=new_pool,
        ssd_conv=_from_time_major(xbc_ext[-(SSD_CONV - 1) * n_b:], n_b),
        ssd=h_last.reshape(n_b, SSD_HEADS, SSD_HEAD_DIM, SSD_STATE),
        ffn_conv=_from_time_major(ffn_hist[0], n_b),
    )
    return x, state


def _log_sigmoid_body(x_ref, b_ref, o_ref):
    o_ref[...] = _log_sigmoid(x_ref[...] + b_ref[...])


def _log_sigmoid_rows(small, bf_row):
    return pl.pallas_call(
        _log_sigmoid_body,
        out_shape=jax.ShapeDtypeStruct(small.shape, F32),
        name="log_forget",
    )(small, bf_row)


def kernel(x_prompt, x_sample, cache_k, cache_v, cache_logf, page_table, state_pool, state_ssd_conv, state_ssd,
           state_ffn_conv, norm1_g, w_in, cm_norm_g, cm_ws, cm_b, pool_w, pool_scale, fox_bf, ssd_conv_w,
           ssd_conv_b, ssd_dt_bias, ssd_a_log, ssd_d, ssd_norm_g, w_branch, w_out, norm2_g, ffn_up, ffn_conv_w,
           ffn_conv_b, ffn_down, final_norm_g):
    n_seq, seq_len, _ = x_prompt.shape
    n_b, n_tok, _ = x_sample.shape
    depth = w_in.shape[0]
    n_pages = page_table.shape[1]
    past_len = n_pages * cache_k.shape[2]
    n_phys = cache_k.shape[1]

    consts = dict(sel=_selection_matrices())
    ck = jnp.transpose(cache_k, (0, 1, 3, 4, 2)).reshape(depth, n_phys, WIDTH, PAGE_SIZE)
    cv = jnp.transpose(cache_v, (0, 1, 3, 4, 2)).reshape(depth, n_phys, WIDTH, PAGE_SIZE)
    clt = jnp.pad(jnp.swapaxes(cache_logf, 2, 3), ((0, 0), (0, 0), (0, HEAD_PAD - FOX_HEADS), (0, 0)))

    xp = x_prompt.reshape(n_seq * seq_len, D_MODEL)
    xs = _to_time_major(x_sample)
    st_p, st_s = [], []
    for l in range(depth):
        w = _layer_params(l, norm1_g, w_in, cm_norm_g, cm_ws, cm_b, pool_w, pool_scale, fox_bf, ssd_conv_w,
                          ssd_conv_b, ssd_dt_bias, ssd_a_log, ssd_d, ssd_norm_g, w_branch, w_out, norm2_g,
                          ffn_up, ffn_conv_w, ffn_conv_b, ffn_down, n_b, n_tok)
        past = dict(cache_k=ck, cache_v=cv, cache_lft=clt, pool=state_pool[l], ssd_conv=state_ssd_conv[l],
                    ssd=state_ssd[l], ffn_conv=state_ffn_conv[l])
        xp, sp = _prompt_layer(xp, w, consts, n_seq)
        xs, ss = _sample_layer(xs, w, consts, l, past, page_table, n_b, past_len)
        st_p.append(sp)
        st_s.append(ss)
    gfin = final_norm_g[None, :]
    y_prompt = _final_norm(xp, gfin, tm=512).reshape(n_seq, seq_len, D_MODEL)
    y_sample = _from_time_major(_final_norm(xs, gfin, tm=xs.shape[0]), n_b)

    def stacked(states, name):
        return jnp.stack([s[name] for s in states], axis=0)

    return (y_prompt, y_sample,
            stacked(st_p, "k"), stacked(st_p, "v"), stacked(st_p, "logf"),
            stacked(st_s, "k"), stacked(st_s, "v"), stacked(st_s, "logf"),
            stacked(st_s, "chunk_v"),
            stacked(st_p, "pool"), stacked(st_s, "pool"),
            stacked(st_p, "ssd_conv"), stacked(st_s, "ssd_conv"),
            stacked(st_p, "ssd"), stacked(st_s, "ssd"),
            stacked(st_p, "ffn_conv"), stacked(st_s, "ffn_conv"))
```

```python
import functools
import math

import numpy as np
import jax
import jax.numpy as jnp
from jax import lax
from jax.experimental import pallas as pl
from jax.experimental.pallas import tpu as pltpu

F32 = jnp.float32
BF16 = jnp.bfloat16

D_MODEL = 1024
WIDTH = 256
PAGE_SIZE = 128
CM_GROUPS = 4
CM_CHUNK = 128
POOL_WINDOWS = (2, 4, 8, 16)
POOL_HIST = 15
POOL_PAD_STEPS = 32
FOX_HEADS = 4
FOX_HEAD_DIM = 64
SSD_HEADS = 4
SSD_HEAD_DIM = 64
SSD_GROUPS = 2
SSD_STATE = 128
SSD_CONV = 4
SSD_CHUNK = 128
SSD_CONV_DIM = 768
D_FF = 2816
FFN_CONV = 3
N_BRANCH = 4
EPS = 1e-6
SMALL_W = 128
P_COLS = 10 * WIDTH + SMALL_W
SMALL_BLK = 10 * WIDTH // SMALL_W
U_AU, U_AV, U_POOL, U_Q, U_K, U_V, U_Z, U_XS, U_BM, U_CM = range(10)

VMEM_LIMIT = 56 * 1024 * 1024


def _cparams(sem):
    return pltpu.CompilerParams(dimension_semantics=sem, vmem_limit_bytes=VMEM_LIMIT)


def _sigmoid(x):
    return 1.0 / (1.0 + jnp.exp(-x))


def _silu(x):
    return x * _sigmoid(x)


def _softplus(x):
    return jnp.maximum(x, 0.0) + jnp.log(1.0 + jnp.exp(-jnp.abs(x)))


def _log_sigmoid(x):
    return -_softplus(-x)


def _gelu_tanh(x):
    return 0.5 * x * (1.0 + jnp.tanh(math.sqrt(2.0 / math.pi) * (x + 0.044715 * (x * x * x))))


def _rms(x, g):
    return x * lax.rsqrt(jnp.mean(x * x, axis=-1, keepdims=True) + EPS) * g


def _split3(x):
    hi = x.astype(BF16)
    r1 = x - hi.astype(F32)
    mid = r1.astype(BF16)
    lo = (r1 - mid.astype(F32)).astype(BF16)
    return hi, mid, lo


def _dot(a, b):
    return jnp.dot(a, b, preferred_element_type=F32)


def _dot_nt(a, b):
    return lax.dot_general(a, b, (((1,), (1,)), ((), ())), preferred_element_type=F32)


def _ones_dot_left(ones_bf16, x):
    hi, mid, lo = _split3(x)
    return _dot(ones_bf16, hi) + _dot(ones_bf16, mid) + _dot(ones_bf16, lo)


def _ones_dot_right(x, ones_bf16):
    hi, mid, lo = _split3(x)
    return _dot(hi, ones_bf16) + _dot(mid, ones_bf16) + _dot(lo, ones_bf16)


def _iota(shape, dim):
    return lax.broadcasted_iota(jnp.int32, shape, dim)


def _proj_body(x_ref, g_ref, w_ref, o_ref):
    h = _rms(x_ref[...], g_ref[...]).astype(BF16)
    o_ref[...] = _dot(h, w_ref[...])


def _proj(x, g, w, tm):
    m, d = x.shape
    n = w.shape[1]
    return pl.pallas_call(
        _proj_body,
        grid=(m // tm,),
        in_specs=[pl.BlockSpec((tm, d), lambda i: (i, 0)),
                  pl.BlockSpec((1, d), lambda i: (0, 0)),
                  pl.BlockSpec((d, n), lambda i: (0, 0))],
        out_specs=pl.BlockSpec((tm, n), lambda i: (i, 0)),
        out_shape=jax.ShapeDtypeStruct((m, n), F32),
        compiler_params=_cparams(("arbitrary",)),
        name="proj",
    )(x, g, w)


def _mixa_body(u_ref, v_ref, wm_ref, b_ref, g_ref, y_ref, va_ref, *, t, n_chunks):
    lane_grp = _iota((t, WIDTH), 1) >> 6
    for c in range(n_chunks):
        rows = pl.ds(c * t, t)
        u = _gelu_tanh(u_ref[rows, :])
        va = _rms(_gelu_tanh(v_ref[rows, :]), g_ref[...])
        va_ref[rows, :] = va
        vab = va.astype(BF16)
        zero = jnp.zeros_like(vab)
        s = b_ref[...]
        for gi in range(CM_GROUPS):
            s = s + _dot(wm_ref[gi], jnp.where(lane_grp == gi, vab, zero))
        y_ref[rows, :] = u * s


def _mixa(p, wm, bias, g, t, tm):
    m = p.shape[0]
    body = functools.partial(_mixa_body, t=t, n_chunks=tm // t)
    return pl.pallas_call(
        body,
        grid=(m // tm,),
        in_specs=[pl.BlockSpec((tm, WIDTH), lambda i: (i, U_AU)),
                  pl.BlockSpec((tm, WIDTH), lambda i: (i, U_AV)),
                  pl.BlockSpec((CM_GROUPS, t, t), lambda i: (0, 0, 0)),
                  pl.BlockSpec((t, WIDTH), lambda i: (0, 0)),
                  pl.BlockSpec((1, WIDTH), lambda i: (0, 0))],
        out_specs=[pl.BlockSpec((tm, WIDTH), lambda i: (i, 0)),
                   pl.BlockSpec((tm, WIDTH), lambda i: (i, 0))],
        out_shape=[jax.ShapeDtypeStruct((m, WIDTH), F32),
                   jax.ShapeDtypeStruct((m, WIDTH), F32)],
        compiler_params=_cparams(("arbitrary",)),
        name="mixer_a",
    )(p, p, wm, bias, g)


def _pool_body(seq_ref, h0_ref, w_ref, scale_ref, y_ref, e1, e2, e4, e8, *, s, tm, pos0, steps_per_tile):
    j = pl.program_id(1)
    hr = POOL_PAD_STEPS * s
    tot = hr + tm

    @pl.when(j == 0)
    def _():
        e1[0:hr, :] = h0_ref[0]

    e1[hr:tot, :] = seq_ref[...]
    e2[8 * s:tot, :] = e1[8 * s:tot, :] + e1[7 * s:tot - s, :]
    e4[16 * s:tot, :] = e2[16 * s:tot, :] + e2[14 * s:tot - 2 * s, :]
    e8[24 * s:tot, :] = e4[24 * s:tot, :] + e4[20 * s:tot - 4 * s, :]
    cur = e1[hr:tot, :]
    s2 = e2[hr:tot, :]
    s4 = e4[hr:tot, :]
    s8 = e8[hr:tot, :]
    s16 = s8 + e8[hr - 8 * s:tot - 8 * s, :]
    lane_grp = _iota((tm, WIDTH), 1) >> 6
    pos = pos0 + j * steps_per_tile + (_iota((tm, WIDTH), 0) >> (s.bit_length() - 1))
    wsum = jnp.where(lane_grp == 0, s2, jnp.where(lane_grp == 1, s4, jnp.where(lane_grp == 2, s8, s16)))
    win = jnp.where(lane_grp == 0, 2, jnp.where(lane_grp == 1, 4, jnp.where(lane_grp == 2, 8, 16)))
    count = jnp.minimum(pos + 1, win).astype(F32)
    d = wsum / count - cur
    y_ref[...] = _dot(d.astype(BF16), w_ref[...]) * scale_ref[...]
    e1[0:hr, :] = e1[tm:tot, :]


def _pool(seq, col_blk, h0, w_bd, scale, *, n_seq, s, tm, pos0):
    m = seq.shape[0]
    tiles = m // n_seq // tm
    hr = POOL_PAD_STEPS * s
    body = functools.partial(_pool_body, s=s, tm=tm, pos0=pos0, steps_per_tile=tm // s)
    return pl.pallas_call(
        body,
        grid=(n_seq, tiles),
        in_specs=[pl.BlockSpec((tm, WIDTH), lambda n, j: (n * tiles + j, col_blk)),
                  pl.BlockSpec((1, hr, WIDTH), lambda n, j: (n, 0, 0)),
                  pl.BlockSpec((WIDTH, WIDTH), lambda n, j: (0, 0)),
                  pl.BlockSpec((1, WIDTH), lambda n, j: (0, 0))],
        out_specs=pl.BlockSpec((tm, WIDTH), lambda n, j: (n * tiles + j, 0)),
        out_shape=jax.ShapeDtypeStruct((m, WIDTH), F32),
        scratch_shapes=[pltpu.VMEM((hr + tm, WIDTH), F32) for _ in range(4)],
        compiler_params=_cparams(("arbitrary", "arbitrary")),
        name="mixer_b",
    )(seq, h0, w_bd, scale)


LOG2E = math.log2(math.e)


V_ROWS = 2 * FOX_HEAD_DIM


def _fox_prep_body(q_ref, k_ref, v_ref, sm_ref, bf_ref, selq_ref, selk_ref, selv_ref,
                   lf_ref, qa_ref, ka_ref, va_ref, carry, *, tm):
    j = pl.program_id(1)

    @pl.when(j == 0)
    def _():
        carry[...] = jnp.zeros_like(carry)

    logf = _log_sigmoid(sm_ref[...] + bf_ref[...])
    lf_ref[...] = logf
    tril = (_iota((tm, tm), 0) >= _iota((tm, tm), 1)).astype(BF16)
    c = _ones_dot_left(tril, logf) + carry[...]
    carry[...] = c[tm - 1:tm, :]
    c = c * LOG2E
    c_t = jnp.transpose(c)
    qb = (q_ref[...] * (FOX_HEAD_DIM ** -0.5 * LOG2E)).astype(BF16)
    kb = k_ref[...].astype(BF16)
    vb = v_ref[...].astype(BF16)
    d = FOX_HEAD_DIM
    lane = _iota((tm, 2 * d), 1)
    row = _iota((2 * d, tm), 0)

    def pieces(x):
        hi = x.astype(BF16).astype(F32)
        r1 = x - hi
        mid = r1.astype(BF16).astype(F32)
        return hi, mid, (r1 - mid).astype(BF16).astype(F32)

    for h in range(FOX_HEADS):
        hi, mid, lo = pieces(jnp.broadcast_to(c_t[h:h + 1, :], (2 * d, tm)))
        q_extra = jnp.where(row == d, hi, jnp.where(row == d + 1, mid, jnp.where(row == d + 2, lo,
                  jnp.where((row >= d + 3) & (row < d + 6), 1.0, 0.0))))
        qa_ref[0, h] = (_dot_nt(selq_ref[h], qb) + q_extra).astype(BF16)
        hi, mid, lo = pieces(jnp.broadcast_to(c[:, h:h + 1], (tm, 2 * d)))
        k_extra = jnp.where(lane == d + 3, -hi, jnp.where(lane == d + 4, -mid, jnp.where(lane == d + 5, -lo,
                  jnp.where((lane >= d) & (lane < d + 3), 1.0, 0.0))))
        ka_ref[0, h] = (_dot(kb, selk_ref[h]) + k_extra).astype(BF16)
        ones_row = jnp.where(row[:V_ROWS] == d, 1.0, 0.0)
        va_ref[0, h] = (_dot_nt(selv_ref[h], vb) + ones_row).astype(BF16)


def _fox_prep(p, bf_row, selq, selk, selv, *, n_seq, tm):
    m = p.shape[0]
    seq_len = m // n_seq
    tiles = seq_len // tm
    aug = jax.ShapeDtypeStruct((n_seq, FOX_HEADS, seq_len, 2 * FOX_HEAD_DIM), BF16)
    aug_spec = pl.BlockSpec((1, FOX_HEADS, tm, 2 * FOX_HEAD_DIM), lambda n, j: (n, 0, j, 0))
    def transposed(rows):
        return (jax.ShapeDtypeStruct((n_seq, FOX_HEADS, rows, seq_len), BF16),
                pl.BlockSpec((1, FOX_HEADS, rows, tm), lambda n, j: (n, 0, 0, j)),
                pl.BlockSpec((FOX_HEADS, rows, WIDTH), lambda n, j: (0, 0, 0)))

    aug_t, aug_t_spec, selq_spec = transposed(2 * FOX_HEAD_DIM)
    v_t, v_t_spec, selv_spec = transposed(V_ROWS)
    sel_spec = pl.BlockSpec((FOX_HEADS, WIDTH, 2 * FOX_HEAD_DIM), lambda n, j: (0, 0, 0))
    return pl.pallas_call(
        functools.partial(_fox_prep_body, tm=tm),
        grid=(n_seq, tiles),
        in_specs=[pl.BlockSpec((tm, WIDTH), lambda n, j: (n * tiles + j, U_Q)),
                  pl.BlockSpec((tm, WIDTH), lambda n, j: (n * tiles + j, U_K)),
                  pl.BlockSpec((tm, WIDTH), lambda n, j: (n * tiles + j, U_V)),
                  pl.BlockSpec((tm, SMALL_W), lambda n, j: (n * tiles + j, SMALL_BLK)),
                  pl.BlockSpec((1, SMALL_W), lambda n, j: (0, 0)),
                  selq_spec, sel_spec, selv_spec],
        out_specs=[pl.BlockSpec((tm, SMALL_W), lambda n, j: (n * tiles + j, 0)),
                   aug_t_spec, aug_spec, v_t_spec],
        out_shape=[jax.ShapeDtypeStruct((m, SMALL_W), F32), aug_t, aug, v_t],
        scratch_shapes=[pltpu.VMEM((1, SMALL_W), F32)],
        compiler_params=_cparams(("arbitrary", "arbitrary")),
        name="fox_prep",
    )(p, p, p, p, bf_row, selq, selk, selv)


def _fox_flash_body(qi_ref, kj_ref, q_ref, k_ref, v_ref, o_ref, m_ref, acc_ref, *, tq, wq):
    step = pl.program_id(1)
    qi = qi_ref[step]
    kj = kj_ref[step]
    d = FOX_HEAD_DIM

    @pl.when(kj == 0)
    def _():
        m_ref[...] = jnp.full_like(m_ref, -jnp.inf)
        acc_ref[...] = jnp.zeros_like(acc_ref)

    n_split = tq // wq
    chains = [(h, c) for h in range(FOX_HEADS) for c in range(n_split)]

    def update(diagonal):
        scores = [_dot(k_ref[0, h], q_ref[0, h, :, c * wq:(c + 1) * wq]) for h, c in chains]
        for i, ((h, c), s) in enumerate(zip(chains, scores)):
            if diagonal:
                s = jnp.where(_iota((tq, wq), 0) <= _iota((tq, wq), 1) + c * wq, s, -jnp.inf)
            m_prev = m_ref[i]
            m_new = jnp.maximum(m_prev, jnp.max(s, axis=0, keepdims=True))
            alpha = jnp.exp2(m_prev - m_new)
            pexp = jnp.exp2(s - m_new).astype(BF16)
            acc_ref[i] = alpha * acc_ref[i] + _dot(v_ref[0, h], pexp)
            m_ref[i] = m_new

    @pl.when(kj < qi)
    def _():
        update(False)

    @pl.when(kj == qi)
    def _():
        update(True)
        o = []
        for h in range(FOX_HEADS):
            parts = []
            for c in range(n_split):
                acc = acc_ref[h * n_split + c]
                parts.append(acc[0:d, :] / acc[d:d + 1, :])
            o.append(jnp.concatenate(parts, axis=1))
        o_ref[...] = jnp.concatenate([jnp.transpose(jnp.concatenate(o[0:2], axis=0)),
                                      jnp.transpose(jnp.concatenate(o[2:4], axis=0))], axis=1)


def _fox_flash(qa_t, ka, va_t, *, tq, wq):
    n_seq, _, seq_len, w = ka.shape
    nq = seq_len // tq
    qi = np.array([i for i in range(nq) for _ in range(i + 1)], np.int32)
    kj = np.array([j for i in range(nq) for j in range(i + 1)], np.int32)
    q_spec = pl.BlockSpec((1, FOX_HEADS, w, tq), lambda n, s, qi, kj: (n, 0, 0, qi[s]))
    k_spec = pl.BlockSpec((1, FOX_HEADS, tq, w), lambda n, s, qi, kj: (n, 0, kj[s], 0))
    v_spec = pl.BlockSpec((1, FOX_HEADS, V_ROWS, tq), lambda n, s, qi, kj: (n, 0, 0, kj[s]))
    grid_spec = pltpu.PrefetchScalarGridSpec(
        num_scalar_prefetch=2,
        grid=(n_seq, len(qi)),
        in_specs=[q_spec, k_spec, v_spec],
        out_specs=pl.BlockSpec((tq, WIDTH), lambda n, s, qi, kj: (n * nq + qi[s], 0)),
        scratch_shapes=[pltpu.VMEM((FOX_HEADS * (tq // wq), 1, wq), F32),
                        pltpu.VMEM((FOX_HEADS * (tq // wq), V_ROWS, wq), F32)],
    )
    return pl.pallas_call(
        functools.partial(_fox_flash_body, tq=tq, wq=wq),
        grid_spec=grid_spec,
        out_shape=jax.ShapeDtypeStruct((n_seq * seq_len, WIDTH), F32),
        compiler_params=_cparams(("arbitrary", "arbitrary")),
        name="fox_flash",
    )(jnp.asarray(qi), jnp.asarray(kj), qa_t, ka, va_t)


TOK_PAD = 8
HEAD_PAD = 8


DECODE_SLOTS = 3


def _fox_decode_body(pt_ref, q_ref, kn_ref, vn_ref, fr_ref, bfc_ref, ck_hbm, cv_hbm, cl_hbm, o_ref,
                     kbuf, vbuf, lbuf, sems, *, layer, n_pages, n_tok):
    b = pl.program_id(0)
    n_b = pl.num_programs(0)
    d = FOX_HEAD_DIM
    ps = PAGE_SIZE
    qrows = n_tok * HEAD_PAD

    def page_copies(page, slot, pg):
        return [pltpu.make_async_copy(src.at[layer, page], dst.at[slot, pg], sems.at[slot])
                for src, dst in ((ck_hbm, kbuf), (cv_hbm, vbuf), (cl_hbm, lbuf))]

    def fetch(seq):
        slot = seq % DECODE_SLOTS
        for pg in range(n_pages):
            for cp in page_copies(pt_ref[seq, pg], slot, pg):
                cp.start()

    @pl.when(b == 0)
    def _():
        for ahead in range(DECODE_SLOTS - 1):
            @pl.when(ahead < n_b)
            def _():
                fetch(ahead)

    slot = b % DECODE_SLOTS
    for pg in range(n_pages):
        for cp in page_copies(0, slot, pg):
            cp.wait()
    k_refs = [kbuf.at[slot, pg] for pg in range(n_pages)]
    v_refs = [vbuf.at[slot, pg] for pg in range(n_pages)]
    lf_refs = [lbuf.at[slot, pg] for pg in range(n_pages)]

    q4 = q_ref[0] * (d ** -0.5)
    own = (_iota((HEAD_PAD, WIDTH), 1) >> 6) == _iota((HEAD_PAD, WIDTH), 0)
    qbd = jnp.concatenate([jnp.where(own, jnp.broadcast_to(q4[t:t + 1, :], (HEAD_PAD, WIDTH)), 0.0)
                           for t in range(n_tok)], axis=0).astype(BF16)

    triu = (_iota((ps, ps), 0) <= _iota((ps, ps), 1)).astype(BF16)
    ones = jnp.ones((ps, ps), BF16)
    lf_new = _log_sigmoid(fr_ref[0] + bfc_ref[...])
    newcum_lane = _ones_dot_right(lf_new, triu)
    row_t = _iota((qrows, ps), 0) >> 3
    lane_i = _iota((qrows, ps), 1)
    newcum_col = jnp.sum(jnp.where(lane_i == row_t, newcum_lane, 0.0), axis=1, keepdims=True)

    lst = jnp.concatenate([r[...] for r in lf_refs], axis=0)
    incl = _ones_dot_right(lst, triu)
    tot = _ones_dot_right(lst, ones)
    later = jnp.zeros((HEAD_PAD, ps), F32)
    bias = [None] * n_pages
    for pg in reversed(range(n_pages)):
        rows = slice(pg * HEAD_PAD, (pg + 1) * HEAD_PAD)
        g = (tot[rows] - incl[rows]) + later
        bias[pg] = jnp.concatenate([g] * n_tok, axis=0) + newcum_col
        later = later + tot[rows]

    k_new = jnp.concatenate([kn_ref[0], jnp.zeros((ps - TOK_PAD, WIDTH), F32)], axis=0).astype(BF16)
    v_new = jnp.concatenate([vn_ref[0], jnp.zeros((ps - TOK_PAD, WIDTH), F32)], axis=0).astype(BF16)
    s_new = _dot_nt(qbd, k_new) + newcum_col - newcum_lane
    s_new = jnp.where(lane_i <= row_t, s_new, -jnp.inf)
    s_past = [_dot(qbd, k_refs[pg][...].astype(BF16)) + bias[pg] for pg in range(n_pages)]

    m_lane = s_new
    for sp in s_past:
        m_lane = jnp.maximum(m_lane, sp)
    m = jnp.max(m_lane, axis=1, keepdims=True)
    p_new = jnp.exp(s_new - m)
    l_lane = p_new
    acc = _dot(p_new.astype(BF16), v_new)
    for pg in range(n_pages):
        pp = jnp.exp(s_past[pg] - m)
        l_lane = l_lane + pp
        acc = acc + _dot_nt(pp.astype(BF16), v_refs[pg][...].astype(BF16))
    o = acc / jnp.sum(l_lane, axis=1, keepdims=True)
    keep = jnp.concatenate([own] * n_tok, axis=0)
    o_ref[0] = jnp.sum(jnp.where(keep, o, 0.0).reshape(n_tok, HEAD_PAD, WIDTH), axis=1)

    @pl.when(b + DECODE_SLOTS - 1 < n_b)
    def _():
        fetch(b + DECODE_SLOTS - 1)


def _fox_decode(layer, page_table, q, kn8, vn8, fr_lane, bf_col, cache_k, cache_v, cache_lft):
    n_b, n_pages = page_table.shape
    n_tok = q.shape[1]
    qrows = n_tok * HEAD_PAD
    ps = PAGE_SIZE

    tok_spec = pl.BlockSpec((1, TOK_PAD, WIDTH), lambda b, pt: (b, 0, 0))
    hbm = pl.BlockSpec(memory_space=pl.ANY)
    grid_spec = pltpu.PrefetchScalarGridSpec(
        num_scalar_prefetch=1,
        grid=(n_b,),
        in_specs=[pl.BlockSpec((1, n_tok, WIDTH), lambda b, pt: (b, 0, 0)), tok_spec, tok_spec,
                  pl.BlockSpec((1, qrows, ps), lambda b, pt: (b, 0, 0)),
                  pl.BlockSpec((qrows, ps), lambda b, pt: (0, 0)),
                  hbm, hbm, hbm],
        out_specs=pl.BlockSpec((1, n_tok, WIDTH), lambda b, pt: (b, 0, 0)),
        scratch_shapes=[pltpu.VMEM((DECODE_SLOTS, n_pages, WIDTH, ps), F32),
                        pltpu.VMEM((DECODE_SLOTS, n_pages, WIDTH, ps), F32),
                        pltpu.VMEM((DECODE_SLOTS, n_pages, HEAD_PAD, ps), F32),
                        pltpu.SemaphoreType.DMA((DECODE_SLOTS,))],
    )
    return pl.pallas_call(
        functools.partial(_fox_decode_body, layer=layer, n_pages=n_pages, n_tok=n_tok),
        grid_spec=grid_spec,
        out_shape=jax.ShapeDtypeStruct((n_b, n_tok, WIDTH), F32),
        compiler_params=_cparams(("arbitrary",)),
        name="fox_decode",
    )(page_table, q, kn8, vn8, fr_lane, bf_col, cache_k, cache_v, cache_lft)


def _ssd_prompt_body(xs_ref, bm_ref, cm_ref, z_ref, sm_ref, cw_ref, cb_ref, dtb_ref, arow_ref, drow_ref, g_ref,
                     y_ref, hl_ref, ext, state, *, t):
    j = pl.program_id(1)
    last = pl.num_programs(1) - 1
    hd = SSD_HEAD_DIM
    w3 = SSD_CONV_DIM

    @pl.when(j == 0)
    def _():
        ext[0:8, :] = jnp.zeros((8, w3), F32)
        state[...] = jnp.zeros_like(state)

    ext[8:8 + t, 0:WIDTH] = xs_ref[...]
    ext[8:8 + t, WIDTH:2 * WIDTH] = bm_ref[...]
    ext[8:8 + t, 2 * WIDTH:w3] = cm_ref[...]
    conv = cb_ref[...]
    for k in range(SSD_CONV):
        conv = conv + cw_ref[k:k + 1, :] * ext[5 + k:5 + k + t, :]
    ext[0:8, :] = ext[t:t + 8, :]
    act = _silu(conv)
    xs = act[:, 0:WIDTH]
    bm = act[:, WIDTH:2 * WIDTH].astype(BF16)
    cm = act[:, 2 * WIDTH:w3].astype(BF16)

    dt = _softplus(sm_ref[...] + dtb_ref[...])
    da = dt * arow_ref[...]
    tril = (_iota((t, t), 0) >= _iota((t, t), 1))
    acs = _ones_dot_left(tril.astype(BF16), da)
    acs_t = jnp.transpose(acs)
    lane_head = _iota((t, WIDTH), 1) >> 6
    dt_x = jnp.zeros((t, WIDTH), F32)
    end_x = jnp.zeros((t, WIDTH), F32)
    in_x = jnp.zeros((t, WIDTH), F32)
    dec_rows = []
    for h in range(SSD_HEADS):
        col = acs[:, 4 + h:5 + h]
        lastv = acs[t - 1:t, 4 + h:5 + h]
        dt_x = jnp.where(lane_head == h, dt[:, 4 + h:5 + h], dt_x)
        end_x = jnp.where(lane_head == h, jnp.exp(lastv - col), end_x)
        in_x = jnp.where(lane_head == h, jnp.exp(col), in_x)
        dec_rows.append(jnp.broadcast_to(jnp.exp(lastv), (hd, SSD_STATE)))
    xdt = xs * dt_x
    xdt_b = xdt.astype(BF16)

    y = jnp.zeros((t, WIDTH), F32)
    h_in = state[...]
    xw_t = jnp.transpose(xdt * end_x).astype(BF16)
    new_states = []
    for gi in range(SSD_GROUPS):
        bg = bm[:, gi * SSD_STATE:(gi + 1) * SSD_STATE]
        cg = cm[:, gi * SSD_STATE:(gi + 1) * SSD_STATE]
        cbm = _dot_nt(cg, bg)
        rows = slice(gi * 2 * hd, (gi + 1) * 2 * hd)
        for r in range(SSD_HEADS // SSD_GROUPS):
            h = gi * (SSD_HEADS // SSD_GROUPS) + r
            seg = acs[:, 4 + h:5 + h] - acs_t[4 + h:5 + h, :]
            decay = jnp.exp(jnp.where(tril, seg, -jnp.inf))
            yd = _dot((cbm * decay).astype(BF16), xdt_b)
            y = y + jnp.where(lane_head == h, yd, 0.0)
        y_off = _dot_nt(cg, h_in[rows, :].astype(BF16))
        y_off = jnp.concatenate([y_off, y_off], axis=1)
        grp_lane = _iota((t, WIDTH), 1) >> 7
        y = y + jnp.where(grp_lane == gi, y_off * in_x, 0.0)
        new_states.append(_dot(xw_t[rows, :], bg))
    state[...] = h_in * jnp.concatenate(dec_rows, axis=0) + jnp.concatenate(new_states, axis=0)

    yg = (y + drow_ref[...] * xs) * _silu(z_ref[...])
    y_ref[...] = _rms(yg, g_ref[...])

    @pl.when(j == last)
    def _():
        hl_ref[0] = state[...]


def _ssd_prompt(p, cw, cb, dtb_row, a_row, d_row, g, *, n_seq):
    m = p.shape[0]
    t = SSD_CHUNK
    tiles = m // n_seq // t
    hp = SSD_HEADS * SSD_HEAD_DIM

    def unit(u):
        return pl.BlockSpec((t, WIDTH), lambda n, j: (n * tiles + j, u))

    def const(shape):
        return pl.BlockSpec(shape, lambda n, j: (0,) * len(shape))

    return pl.pallas_call(
        functools.partial(_ssd_prompt_body, t=t),
        grid=(n_seq, tiles),
        in_specs=[unit(U_XS), unit(U_BM), unit(U_CM), unit(U_Z),
                  pl.BlockSpec((t, SMALL_W), lambda n, j: (n * tiles + j, SMALL_BLK)),
                  const((SSD_CONV, SSD_CONV_DIM)), const((1, SSD_CONV_DIM)),
                  const((1, SMALL_W)), const((1, SMALL_W)), const((1, WIDTH)), const((1, WIDTH))],
        out_specs=[pl.BlockSpec((t, WIDTH), lambda n, j: (n * tiles + j, 0)),
                   pl.BlockSpec((1, hp, SSD_STATE), lambda n, j: (n, 0, 0))],
        out_shape=[jax.ShapeDtypeStruct((m, WIDTH), F32),
                   jax.ShapeDtypeStruct((n_seq, hp, SSD_STATE), F32)],
        scratch_shapes=[pltpu.VMEM((8 + t, SSD_CONV_DIM), F32),
                        pltpu.VMEM((hp, SSD_STATE), F32)],
        compiler_params=_cparams(("arbitrary", "arbitrary")),
        name="ssd_prompt",
    )(p, p, p, p, p, cw, cb, dtb_row, a_row, d_row, g)


def _conv_silu_body(ext_ref, w_ref, b_ref, o_ref, *, s, taps, rows):
    acc = b_ref[...]
    for k in range(taps):
        acc = acc + w_ref[k:k + 1, :] * ext_ref[k * s:k * s + rows, :]
    o_ref[...] = _silu(acc)


def _conv_silu(ext, w, b, *, s, rows):
    taps = w.shape[0]
    c = ext.shape[1]
    return pl.pallas_call(
        functools.partial(_conv_silu_body, s=s, taps=taps, rows=rows),
        out_shape=jax.ShapeDtypeStruct((rows, c), F32),
        compiler_params=pltpu.CompilerParams(vmem_limit_bytes=VMEM_LIMIT),
        name="conv_silu",
    )(ext, w, b)


def _ssd_step_body(x_ref, b_ref, c_ref, dtr_ref, dtb_ref, a_ref, h0_ref, y_ref, hl_ref, *, n_tok):
    dt = _softplus(dtr_ref[...] + dtb_ref[...])
    da = dt * a_ref[...]
    h = h0_ref[...]
    lane_t = _iota(y_ref.shape, 2)
    y = jnp.zeros(y_ref.shape, F32)
    for t in range(n_tok):
        xdt = x_ref[:, :, t:t + 1] * dt[:, :, t:t + 1]
        h = h * jnp.exp(da[:, :, t:t + 1]) + xdt * b_ref[:, t:t + 1, :]
        yt = jnp.sum(h * c_ref[:, t:t + 1, :], axis=2, keepdims=True)
        y = jnp.where(lane_t == t, yt, y)
    y_ref[...] = y
    hl_ref[...] = h


def _ssd_step(x_bh, b_bh, c_bh, dtr_bh, dtb_bh, a_bh, h0, *, blk):
    bh, hd, n_tok = x_bh.shape
    ns = h0.shape[2]

    def spec(shape):
        return pl.BlockSpec((blk,) + shape, lambda i: (i, 0, 0))

    return pl.pallas_call(
        functools.partial(_ssd_step_body, n_tok=n_tok),
        grid=(bh // blk,),
        in_specs=[spec((hd, n_tok)), spec((n_tok, ns)), spec((n_tok, ns)), spec((1, n_tok)),
                  spec((1, 1)), spec((1, 1)), spec((hd, ns))],
        out_specs=[spec((hd, n_tok)), spec((hd, ns))],
        out_shape=[jax.ShapeDtypeStruct((bh, hd, n_tok), F32),
                   jax.ShapeDtypeStruct((bh, hd, ns), F32)],
        compiler_params=_cparams(("arbitrary",)),
        name="ssd_step",
    )(x_bh, b_bh, c_bh, dtr_bh, dtb_bh, a_bh, h0)


def _ssd_gate_body(y_ref, xs_ref, z_ref, d_ref, g_ref, o_ref):
    yg = (y_ref[...] + d_ref[...] * xs_ref[...]) * _silu(z_ref[...])
    o_ref[...] = _rms(yg, g_ref[...])


def _ssd_gate(y, xs, z, d_row, g):
    return pl.pallas_call(
        _ssd_gate_body,
        out_shape=jax.ShapeDtypeStruct(y.shape, F32),
        name="ssd_gate",
    )(y, xs, z, d_row, g)


def _merge_body(x_ref, g_ref, ya_ref, yb_ref, yc_ref, yd_ref, wg_ref, wb_ref, wo_ref, o_ref):
    x = x_ref[...]
    h = _rms(x, g_ref[...]).astype(BF16)
    merged = jnp.zeros(x.shape, F32)
    for i, y_ref in enumerate((ya_ref, yb_ref, yc_ref, yd_ref)):
        gate = _sigmoid(_dot(h, wg_ref[:, i * D_MODEL:(i + 1) * D_MODEL]))
        merged = merged + gate * _dot(y_ref[...].astype(BF16), wb_ref[i])
    o_ref[...] = x + _dot(merged.astype(BF16), wo_ref[...])


def _merge(x, g, ys, wg, wb, wo, tm):
    m = x.shape[0]
    row = lambda i: (i, 0)
    y_spec = pl.BlockSpec((tm, WIDTH), row)
    return pl.pallas_call(
        _merge_body,
        grid=(m // tm,),
        in_specs=[pl.BlockSpec((tm, D_MODEL), row),
                  pl.BlockSpec((1, D_MODEL), lambda i: (0, 0)),
                  y_spec, y_spec, y_spec, y_spec,
                  pl.BlockSpec((D_MODEL, N_BRANCH * D_MODEL), lambda i: (0, 0)),
                  pl.BlockSpec((N_BRANCH, WIDTH, D_MODEL), lambda i: (0, 0, 0)),
                  pl.BlockSpec((D_MODEL, D_MODEL), lambda i: (0, 0))],
        out_specs=pl.BlockSpec((tm, D_MODEL), row),
        out_shape=jax.ShapeDtypeStruct((m, D_MODEL), F32),
        compiler_params=_cparams(("arbitrary",)),
        name="merge",
    )(x, g, *ys, wg, wb, wo)


def _ffn_body(x_ref, g_ref, h0g_ref, h0u_ref, wug_ref, wuu_ref, cwg_ref, cwu_ref, cbg_ref, cbu_ref, wd_ref,
              o_ref, hg_ref, hu_ref, h2, acc, wkg, wku, histg, histu, *, s, tm, hr):
    j = pl.program_id(1)
    c = pl.program_id(2)
    last_c = pl.num_programs(2) - 1

    @pl.when(c == 0)
    def _():
        h2[...] = _rms(x_ref[...], g_ref[...]).astype(BF16)
        acc[...] = jnp.zeros_like(acc)

    @pl.when(j == 0)
    def _():
        histg[c] = h0g_ref[0]
        histu[c] = h0u_ref[0]

    def conv(wk, hist, wu_ref, cw_ref, cb_ref):
        wk[0:hr, :] = hist[c]
        wk[hr:hr + tm, :] = _dot(h2[...], wu_ref[...])
        out = cb_ref[...]
        for k in range(FFN_CONV):
            off = hr - (FFN_CONV - 1 - k) * s
            out = out + cw_ref[k:k + 1, :] * wk[off:off + tm, :]
        hist[c] = wk[tm:tm + hr, :]
        return out

    yg = conv(wkg, histg, wug_ref, cwg_ref, cbg_ref)
    yu = conv(wku, histu, wuu_ref, cwu_ref, cbu_ref)
    acc[...] += _dot((_silu(yg) * yu).astype(BF16), wd_ref[...])
    hg_ref[0, c] = histg[c]
    hu_ref[0, c] = histu[c]

    @pl.when(c == last_c)
    def _():
        o_ref[...] = x_ref[...] + acc[...]


def _ffn(x, g, h0, wu, cw, cb, wd, *, n_seq, s, tm, cwid):
    m = x.shape[0]
    tiles = m // n_seq // tm
    hr = h0.shape[1]
    nc = D_FF // cwid
    row = lambda n, j, c: (n * tiles + j, 0)
    hist_g = lambda n, j, c: (n, 0, c)
    hist_u = lambda n, j, c: (n, 0, nc + c)
    col_g = lambda n, j, c: (0, c)
    col_u = lambda n, j, c: (0, nc + c)
    hist_shape = jax.ShapeDtypeStruct((n_seq, nc, hr, cwid), F32)
    hist_out = pl.BlockSpec((1, nc, hr, cwid), lambda n, j, c: (n, 0, 0, 0))
    x_new, hg, hu = pl.pallas_call(
        functools.partial(_ffn_body, s=s, tm=tm, hr=hr),
        grid=(n_seq, tiles, nc),
        in_specs=[pl.BlockSpec((tm, D_MODEL), row),
                  pl.BlockSpec((1, D_MODEL), lambda n, j, c: (0, 0)),
                  pl.BlockSpec((1, hr, cwid), hist_g), pl.BlockSpec((1, hr, cwid), hist_u),
                  pl.BlockSpec((D_MODEL, cwid), col_g), pl.BlockSpec((D_MODEL, cwid), col_u),
                  pl.BlockSpec((FFN_CONV, cwid), col_g), pl.BlockSpec((FFN_CONV, cwid), col_u),
                  pl.BlockSpec((1, cwid), col_g), pl.BlockSpec((1, cwid), col_u),
                  pl.BlockSpec((cwid, D_MODEL), lambda n, j, c: (c, 0))],
        out_specs=[pl.BlockSpec((tm, D_MODEL), row), hist_out, hist_out],
        out_shape=[jax.ShapeDtypeStruct((m, D_MODEL), F32), hist_shape, hist_shape],
        scratch_shapes=[pltpu.VMEM((tm, D_MODEL), BF16),
                        pltpu.VMEM((tm, D_MODEL), F32),
                        pltpu.VMEM((hr + tm, cwid), F32),
                        pltpu.VMEM((hr + tm, cwid), F32),
                        pltpu.VMEM((nc, hr, cwid), F32),
                        pltpu.VMEM((nc, hr, cwid), F32)],
        compiler_params=_cparams(("arbitrary", "arbitrary", "arbitrary")),
        name="ffn",
    )(x, g, h0, h0, wu, wu, cw, cw, cb, cb, wd)
    to_rows = lambda a: jnp.swapaxes(a, 1, 2).reshape(n_seq, hr, D_FF)
    return x_new, jnp.concatenate([to_rows(hg), to_rows(hu)], axis=2)


def _final_norm_body(x_ref, g_ref, o_ref):
    o_ref[...] = _rms(x_ref[...], g_ref[...])


def _final_norm(x, g, tm):
    m, d = x.shape
    return pl.pallas_call(
        _final_norm_body,
        grid=(m // tm,),
        in_specs=[pl.BlockSpec((tm, d), lambda i: (i, 0)), pl.BlockSpec((1, d), lambda i: (0, 0))],
        out_specs=pl.BlockSpec((tm, d), lambda i: (i, 0)),
        out_shape=jax.ShapeDtypeStruct((m, d), F32),
        compiler_params=_cparams(("arbitrary",)),
        name="final_norm",
    )(x, g)


def _selection_matrices():
    d = FOX_HEAD_DIM
    selq = np.zeros((FOX_HEADS, 2 * d, WIDTH), np.float32)
    selk = np.zeros((FOX_HEADS, WIDTH, 2 * d), np.float32)
    selv = np.zeros((FOX_HEADS, V_ROWS, WIDTH), np.float32)
    for h in range(FOX_HEADS):
        for i in range(d):
            selq[h, i, h * d + i] = 1.0
            selk[h, h * d + i, i] = 1.0
            selv[h, i, h * d + i] = 1.0
    return (jnp.asarray(selq, BF16), jnp.asarray(selk, BF16), jnp.asarray(selv, BF16))


def _layer_params(l, norm1_g, w_in, cm_norm_g, cm_ws, cm_b, pool_w, pool_scale, fox_bf, ssd_conv_w, ssd_conv_b,
                  ssd_dt_bias, ssd_a_log, ssd_d, ssd_norm_g, w_branch, w_out, norm2_g, ffn_up, ffn_conv_w,
                  ffn_conv_b, ffn_down, n_dec, n_tok):
    w = w_in[l]
    o_f = 2 * WIDTH + WIDTH + 3 * WIDTH
    o_z = o_f + FOX_HEADS
    o_xbc = o_z + WIDTH
    o_dt = o_xbc + SSD_CONV_DIM
    o_gate = o_dt + SSD_HEADS
    w_p = jnp.concatenate([w[:, :o_f], w[:, o_z:o_dt], w[:, o_f:o_z], w[:, o_dt:o_gate],
                           jnp.zeros((D_MODEL, SMALL_W - FOX_HEADS - SSD_HEADS), F32)], axis=1).astype(BF16)
    causal = jnp.tril(jnp.ones((CM_CHUNK, CM_CHUNK), F32))
    ws = cm_ws[l] * causal
    gd = WIDTH // CM_GROUPS
    eye = jnp.eye(n_dec, dtype=F32)
    pd = len(POOL_WINDOWS)
    pw = pool_w[l].astype(F32)
    pool_bd = jnp.zeros((WIDTH, WIDTH), F32)
    pg = WIDTH // pd
    for gi in range(pd):
        pool_bd = pool_bd.at[gi * pg:(gi + 1) * pg, gi * pg:(gi + 1) * pg].set(pw[gi])
    lanes = jnp.arange(SMALL_W)
    head_of_lane = jnp.clip(lanes - FOX_HEADS, 0, SSD_HEADS - 1)
    dt_lane = (lanes >= FOX_HEADS) & (lanes < FOX_HEADS + SSD_HEADS)
    a = -jnp.exp(ssd_a_log[l].astype(F32))
    return dict(
        g1=norm1_g[l][None, :],
        w_p=w_p,
        w_gate=w[:, o_gate:].astype(BF16),
        cm_g=cm_norm_g[l][None, :],
        ws_prompt=ws.astype(BF16),
        cmb_prompt=jnp.repeat(cm_b[l].T, gd, axis=1),
        ws_sample=jnp.stack([jnp.kron(ws[gi, :n_tok, :n_tok], eye) for gi in range(CM_GROUPS)]).astype(BF16),
        cmb_sample=jnp.repeat(jnp.repeat(cm_b[l].T[:n_tok], n_dec, axis=0), gd, axis=1),
        pool_bd=pool_bd.astype(BF16),
        pool_scale=pool_scale[l][None, :].astype(F32),
        bf_row=jnp.pad(fox_bf[l], (0, SMALL_W - FOX_HEADS))[None, :],
        bf_col=jnp.broadcast_to(jnp.tile(jnp.pad(fox_bf[l], (0, HEAD_PAD - FOX_HEADS)), n_tok)[:, None],
                                (n_tok * HEAD_PAD, PAGE_SIZE)),
        conv_w=ssd_conv_w[l], conv_b=ssd_conv_b[l][None, :],
        dtb_row=jnp.where(dt_lane, ssd_dt_bias[l][head_of_lane], 0.0)[None, :],
        a_row=jnp.where(dt_lane, a[head_of_lane], 0.0)[None, :],
        d_row=jnp.repeat(ssd_d[l].astype(F32), SSD_HEAD_DIM)[None, :],
        dtb_bh=jnp.tile(ssd_dt_bias[l], n_dec)[:, None, None],
        a_bh=jnp.tile(a, n_dec)[:, None, None],
        ssd_g=ssd_norm_g[l][None, :],
        w_branch=w_branch[l].astype(BF16),
        w_out=w_out[l].astype(BF16),
        g2=norm2_g[l][None, :],
        ffn_up=ffn_up[l].astype(BF16),
        ffn_cw=ffn_conv_w[l], ffn_cb=ffn_conv_b[l][None, :],
        ffn_down=ffn_down[l].astype(BF16),
    )


def _prompt_layer(x, w, consts, n_seq):
    m = x.shape[0]
    seq_len = m // n_seq
    selq, selk, selv = consts["sel"]
    p = _proj(x, w["g1"], w["w_p"], tm=512)
    y_a, _ = _mixa(p, w["ws_prompt"], w["cmb_prompt"], w["cm_g"], t=CM_CHUNK, tm=512)
    y_b = _pool(p, U_POOL, jnp.zeros((n_seq, POOL_PAD_STEPS, WIDTH), F32), w["pool_bd"], w["pool_scale"],
                n_seq=n_seq, s=1, tm=512, pos0=0)
    lf, qa, ka, va = _fox_prep(p, w["bf_row"], selq, selk, selv, n_seq=n_seq, tm=512)
    tq = 1024 if seq_len % 1024 == 0 else 512
    y_c = _fox_flash(qa, ka, va, tq=tq, wq=512)
    y_d, ssd_last = _ssd_prompt(p, w["conv_w"], w["conv_b"], w["dtb_row"], w["a_row"], w["d_row"], w["ssd_g"],
                                n_seq=n_seq)
    x = _merge(x, w["g1"], (y_a, y_b, y_c, y_d), w["w_gate"], w["w_branch"], w["w_out"], tm=512)
    x, ffn_hist = _ffn(x, w["g2"], jnp.zeros((n_seq, 8, 2 * D_FF), F32), w["ffn_up"], w["ffn_cw"],
                       w["ffn_cb"], w["ffn_down"], n_seq=n_seq, s=1, tm=512, cwid=D_FF // 2)
    p3 = p.reshape(n_seq, seq_len, P_COLS)
    unit = lambda u: p3[:, :, u * WIDTH:(u + 1) * WIDTH]
    state = dict(
        k=unit(U_K).reshape(n_seq, seq_len, FOX_HEADS, FOX_HEAD_DIM),
        v=unit(U_V).reshape(n_seq, seq_len, FOX_HEADS, FOX_HEAD_DIM),
        logf=lf.reshape(n_seq, seq_len, SMALL_W)[:, :, :FOX_HEADS],
        pool=unit(U_POOL)[:, -POOL_HIST:],
        ssd_conv=p3[:, -(SSD_CONV - 1):, U_XS * WIDTH:U_XS * WIDTH + SSD_CONV_DIM],
        ssd=ssd_last.reshape(n_seq, SSD_HEADS, SSD_HEAD_DIM, SSD_STATE),
        ffn_conv=ffn_hist[:, -(FFN_CONV - 1):],
    )
    return x, state


def _to_time_major(a):
    return jnp.swapaxes(a, 0, 1).reshape((a.shape[0] * a.shape[1],) + a.shape[2:])


def _from_time_major(a, n_b):
    return jnp.swapaxes(a.reshape((a.shape[0] // n_b, n_b) + a.shape[1:]), 0, 1)


def _sample_layer(x, w, consts, layer, past, page_table, n_b, past_len):
    m = x.shape[0]
    n_tok = m // n_b
    p = _proj(x, w["g1"], w["w_p"], tm=m)
    unit = lambda u: p[:, u * WIDTH:(u + 1) * WIDTH]
    small = p[:, 10 * WIDTH:]

    y_a, va = _mixa(p, w["ws_sample"], w["cmb_sample"], w["cm_g"], t=m, tm=m)

    pool_new = unit(U_POOL)
    pool_hist = jnp.concatenate([jnp.zeros(((POOL_PAD_STEPS - POOL_HIST) * n_b, WIDTH), F32),
                                 _to_time_major(past["pool"])], axis=0)
    y_b = _pool(pool_new, 0, pool_hist[None], w["pool_bd"], w["pool_scale"],
                n_seq=1, s=n_b, tm=m, pos0=past_len)
    new_pool = jnp.concatenate([past["pool"], _from_time_major(pool_new, n_b)], axis=1)[:, -POOL_HIST:]

    pad_tok = lambda a: jnp.pad(_from_time_major(a, n_b), ((0, 0), (0, TOK_PAD - n_tok), (0, 0)))
    f_raw = _from_time_major(small[:, :FOX_HEADS], n_b)
    fr_lane = jnp.pad(jnp.swapaxes(f_raw, 1, 2),
                      ((0, 0), (0, HEAD_PAD - FOX_HEADS), (0, PAGE_SIZE - n_tok)))
    fr_lane = jnp.tile(fr_lane, (1, n_tok, 1))
    y_c = _fox_decode(layer, page_table, _from_time_major(unit(U_Q), n_b), pad_tok(unit(U_K)),
                      pad_tok(unit(U_V)), fr_lane, w["bf_col"], past["cache_k"], past["cache_v"],
                      past["cache_lft"])
    y_c = _to_time_major(y_c)
    logf = _log_sigmoid_rows(small, w["bf_row"])[:, :FOX_HEADS]

    xbc_new = p[:, U_XS * WIDTH:U_XS * WIDTH + SSD_CONV_DIM]
    xbc_ext = jnp.concatenate([_to_time_major(past["ssd_conv"]), xbc_new], axis=0)
    act = _conv_silu(xbc_ext, w["conv_w"], w["conv_b"], s=n_b, rows=m)
    xs = act[:, :WIDTH]
    rep_heads = lambda a: jnp.repeat(_from_time_major(a, n_b).reshape(n_b, n_tok, SSD_GROUPS, SSD_STATE),
                                     SSD_HEADS // SSD_GROUPS, axis=2)
    to_bh = lambda a: jnp.transpose(a, (0, 2, 1, 3)).reshape(n_b * SSD_HEADS, n_tok, SSD_STATE)
    x_bh = jnp.transpose(_from_time_major(xs, n_b).reshape(n_b, n_tok, SSD_HEADS, SSD_HEAD_DIM),
                         (0, 2, 3, 1)).reshape(n_b * SSD_HEADS, SSD_HEAD_DIM, n_tok)
    dtr = _from_time_major(small[:, FOX_HEADS:FOX_HEADS + SSD_HEADS], n_b)
    dtr_bh = jnp.swapaxes(dtr, 1, 2).reshape(n_b * SSD_HEADS, 1, n_tok)
    h0 = past["ssd"].reshape(n_b * SSD_HEADS, SSD_HEAD_DIM, SSD_STATE)
    y_bh, h_last = _ssd_step(x_bh, to_bh(rep_heads(act[:, WIDTH:2 * WIDTH])),
                             to_bh(rep_heads(act[:, 2 * WIDTH:])), dtr_bh, w["dtb_bh"], w["a_bh"], h0, blk=32)
    y_s = _to_time_major(jnp.transpose(y_bh.reshape(n_b, SSD_HEADS, SSD_HEAD_DIM, n_tok),
                                       (0, 3, 1, 2)).reshape(n_b, n_tok, WIDTH))
    y_d = _ssd_gate(y_s, xs, unit(U_Z), w["d_row"], w["ssd_g"])

    x = _merge(x, w["g1"], (y_a, y_b, y_c, y_d), w["w_gate"], w["w_branch"], w["w_out"], tm=min(m, 256))
    ffn_h0 = _to_time_major(past["ffn_conv"])[None]
    x, ffn_hist = _ffn(x, w["g2"], ffn_h0, w["ffn_up"], w["ffn_cw"], w["ffn_cb"], w["ffn_down"],
                       n_seq=1, s=n_b, tm=m, cwid=D_FF // 2)
    state = dict(
        k=_from_time_major(unit(U_K), n_b).reshape(n_b, n_tok, FOX_HEADS, FOX_HEAD_DIM),
        v=_from_time_major(unit(U_V), n_b).reshape(n_b, n_tok, FOX_HEADS, FOX_HEAD_DIM),
        logf=_from_time_major(logf, n_b),
        chunk_v=_from_time_major(va, n_b),
        pool=new_pool,
        ssd_conv=_from_time_major(xbc_ext[-(SSD_CONV - 1) * n_b:], n_b),
        ssd=h_last.reshape(n_b, SSD_HEADS, SSD_HEAD_DIM, SSD_STATE),
        ffn_conv=_from_time_major(ffn_hist[0], n_b),
    )
    return x, state


def _log_sigmoid_body(x_ref, b_ref, o_ref):
    o_ref[...] = _log_sigmoid(x_ref[...] + b_ref[...])


def _log_sigmoid_rows(small, bf_row):
    return pl.pallas_call(
        _log_sigmoid_body,
        out_shape=jax.ShapeDtypeStruct(small.shape, F32),
        name="log_forget",
    )(small, bf_row)


def kernel(x_prompt, x_sample, cache_k, cache_v, cache_logf, page_table, state_pool, state_ssd_conv, state_ssd,
           state_ffn_conv, norm1_g, w_in, cm_norm_g, cm_ws, cm_b, pool_w, pool_scale, fox_bf, ssd_conv_w,
           ssd_conv_b, ssd_dt_bias, ssd_a_log, ssd_d, ssd_norm_g, w_branch, w_out, norm2_g, ffn_up, ffn_conv_w,
           ffn_conv_b, ffn_down, final_norm_g):
    n_seq, seq_len, _ = x_prompt.shape
    n_b, n_tok, _ = x_sample.shape
    depth = w_in.shape[0]
    n_pages = page_table.shape[1]
    past_len = n_pages * cache_k.shape[2]
    n_phys = cache_k.shape[1]

    consts = dict(sel=_selection_matrices())
    ck = jnp.transpose(cache_k, (0, 1, 3, 4, 2)).reshape(depth, n_phys, WIDTH, PAGE_SIZE)
    cv = jnp.transpose(cache_v, (0, 1, 3, 4, 2)).reshape(depth, n_phys, WIDTH, PAGE_SIZE)
    clt = jnp.pad(jnp.swapaxes(cache_logf, 2, 3), ((0, 0), (0, 0), (0, HEAD_PAD - FOX_HEADS), (0, 0)))

    xp = x_prompt.reshape(n_seq * seq_len, D_MODEL)
    xs = _to_time_major(x_sample)
    st_p, st_s = [], []
    for l in range(depth):
        w = _layer_params(l, norm1_g, w_in, cm_norm_g, cm_ws, cm_b, pool_w, pool_scale, fox_bf, ssd_conv_w,
                          ssd_conv_b, ssd_dt_bias, ssd_a_log, ssd_d, ssd_norm_g, w_branch, w_out, norm2_g,
                          ffn_up, ffn_conv_w, ffn_conv_b, ffn_down, n_b, n_tok)
        past = dict(cache_k=ck, cache_v=cv, cache_lft=clt, pool=state_pool[l], ssd_conv=state_ssd_conv[l],
                    ssd=state_ssd[l], ffn_conv=state_ffn_conv[l])
        xp, sp = _prompt_layer(xp, w, consts, n_seq)
        xs, ss = _sample_layer(xs, w, consts, l, past, page_table, n_b, past_len)
        st_p.append(sp)
        st_s.append(ss)
    gfin = final_norm_g[None, :]
    y_prompt = _final_norm(xp, gfin, tm=512).reshape(n_seq, seq_len, D_MODEL)
    y_sample = _from_time_major(_final_norm(xs, gfin, tm=xs.shape[0]), n_b)

    def stacked(states, name):
        return jnp.stack([s[name] for s in states], axis=0)

    return (y_prompt, y_sample,
            stacked(st_p, "k"), stacked(st_p, "v"), stacked(st_p, "logf"),
            stacked(st_s, "k"), stacked(st_s, "v"), stacked(st_s, "logf"),
            stacked(st_s, "chunk_v"),
            stacked(st_p, "pool"), stacked(st_s, "pool"),
            stacked(st_p, "ssd_conv"), stacked(st_s, "ssd_conv"),
            stacked(st_p, "ssd"), stacked(st_s, "ssd"),
            stacked(st_p, "ffn_conv"), stacked(st_s, "ffn_conv"))
```

```python
import functools
import math

import numpy as np
import jax
import jax.numpy as jnp
from jax import lax
from jax.experimental import pallas as pl
from jax.experimental.pallas import tpu as pltpu

F32 = jnp.float32
BF16 = jnp.bfloat16

D_MODEL = 1024
WIDTH = 256
PAGE_SIZE = 128
CM_GROUPS = 4
CM_CHUNK = 128
POOL_WINDOWS = (2, 4, 8, 16)
POOL_HIST = 15
POOL_PAD_STEPS = 32
FOX_HEADS = 4
FOX_HEAD_DIM = 64
SSD_HEADS = 4
SSD_HEAD_DIM = 64
SSD_GROUPS = 2
SSD_STATE = 128
SSD_CONV = 4
SSD_CHUNK = 128
SSD_CONV_DIM = 768
D_FF = 2816
FFN_CONV = 3
N_BRANCH = 4
EPS = 1e-6
SMALL_W = 128
P_COLS = 10 * WIDTH + SMALL_W
SMALL_BLK = 10 * WIDTH // SMALL_W
U_AU, U_AV, U_POOL, U_Q, U_K, U_V, U_Z, U_XS, U_BM, U_CM = range(10)

VMEM_LIMIT = 56 * 1024 * 1024


def _cparams(sem):
    return pltpu.CompilerParams(dimension_semantics=sem, vmem_limit_bytes=VMEM_LIMIT)


def _sigmoid(x):
    return 1.0 / (1.0 + jnp.exp(-x))


def _silu(x):
    return x * _sigmoid(x)


def _softplus(x):
    return jnp.maximum(x, 0.0) + jnp.log(1.0 + jnp.exp(-jnp.abs(x)))


def _log_sigmoid(x):
    return -_softplus(-x)


def _gelu_tanh(x):
    return 0.5 * x * (1.0 + jnp.tanh(math.sqrt(2.0 / math.pi) * (x + 0.044715 * (x * x * x))))


def _rms(x, g):
    return x * lax.rsqrt(jnp.mean(x * x, axis=-1, keepdims=True) + EPS) * g


def _split3(x):
    hi = x.astype(BF16)
    r1 = x - hi.astype(F32)
    mid = r1.astype(BF16)
    lo = (r1 - mid.astype(F32)).astype(BF16)
    return hi, mid, lo


def _dot(a, b):
    return jnp.dot(a, b, preferred_element_type=F32)


def _dot_nt(a, b):
    return lax.dot_general(a, b, (((1,), (1,)), ((), ())), preferred_element_type=F32)


def _ones_dot_left(ones_bf16, x):
    hi, mid, lo = _split3(x)
    return _dot(ones_bf16, hi) + _dot(ones_bf16, mid) + _dot(ones_bf16, lo)


def _ones_dot_right(x, ones_bf16):
    hi, mid, lo = _split3(x)
    return _dot(hi, ones_bf16) + _dot(mid, ones_bf16) + _dot(lo, ones_bf16)


def _iota(shape, dim):
    return lax.broadcasted_iota(jnp.int32, shape, dim)


def _proj_body(x_ref, g_ref, w_ref, o_ref):
    h = _rms(x_ref[...], g_ref[...]).astype(BF16)
    o_ref[...] = _dot(h, w_ref[...])


def _proj(x, g, w, tm):
    m, d = x.shape
    n = w.shape[1]
    return pl.pallas_call(
        _proj_body,
        grid=(m // tm,),
        in_specs=[pl.BlockSpec((tm, d), lambda i: (i, 0)),
                  pl.BlockSpec((1, d), lambda i: (0, 0)),
                  pl.BlockSpec((d, n), lambda i: (0, 0))],
        out_specs=pl.BlockSpec((tm, n), lambda i: (i, 0)),
        out_shape=jax.ShapeDtypeStruct((m, n), F32),
        compiler_params=_cparams(("arbitrary",)),
        name="proj",
    )(x, g, w)


def _mixa_body(u_ref, v_ref, wm_ref, b_ref, g_ref, y_ref, va_ref, *, t, n_chunks):
    lane_grp = _iota((t, WIDTH), 1) >> 6
    for c in range(n_chunks):
        rows = pl.ds(c * t, t)
        u = _gelu_tanh(u_ref[rows, :])
        va = _rms(_gelu_tanh(v_ref[rows, :]), g_ref[...])
        va_ref[rows, :] = va
        vab = va.astype(BF16)
        zero = jnp.zeros_like(vab)
        s = b_ref[...]
        for gi in range(CM_GROUPS):
            s = s + _dot(wm_ref[gi], jnp.where(lane_grp == gi, vab, zero))
        y_ref[rows, :] = u * s


def _mixa(p, wm, bias, g, t, tm):
    m = p.shape[0]
    body = functools.partial(_mixa_body, t=t, n_chunks=tm // t)
    return pl.pallas_call(
        body,
        grid=(m // tm,),
        in_specs=[pl.BlockSpec((tm, WIDTH), lambda i: (i, U_AU)),
                  pl.BlockSpec((tm, WIDTH), lambda i: (i, U_AV)),
                  pl.BlockSpec((CM_GROUPS, t, t), lambda i: (0, 0, 0)),
                  pl.BlockSpec((t, WIDTH), lambda i: (0, 0)),
                  pl.BlockSpec((1, WIDTH), lambda i: (0, 0))],
        out_specs=[pl.BlockSpec((tm, WIDTH), lambda i: (i, 0)),
                   pl.BlockSpec((tm, WIDTH), lambda i: (i, 0))],
        out_shape=[jax.ShapeDtypeStruct((m, WIDTH), F32),
                   jax.ShapeDtypeStruct((m, WIDTH), F32)],
        compiler_params=_cparams(("arbitrary",)),
        name="mixer_a",
    )(p, p, wm, bias, g)


def _pool_body(seq_ref, h0_ref, w_ref, scale_ref, y_ref, e1, e2, e4, e8, *, s, tm, pos0, steps_per_tile):
    j = pl.program_id(1)
    hr = POOL_PAD_STEPS * s
    tot = hr + tm

    @pl.when(j == 0)
    def _():
        e1[0:hr, :] = h0_ref[0]

    e1[hr:tot, :] = seq_ref[...]
    e2[8 * s:tot, :] = e1[8 * s:tot, :] + e1[7 * s:tot - s, :]
    e4[16 * s:tot, :] = e2[16 * s:tot, :] + e2[14 * s:tot - 2 * s, :]
    e8[24 * s:tot, :] = e4[24 * s:tot, :] + e4[20 * s:tot - 4 * s, :]
    cur = e1[hr:tot, :]
    s2 = e2[hr:tot, :]
    s4 = e4[hr:tot, :]
    s8 = e8[hr:tot, :]
    s16 = s8 + e8[hr - 8 * s:tot - 8 * s, :]
    lane_grp = _iota((tm, WIDTH), 1) >> 6
    pos = pos0 + j * steps_per_tile + (_iota((tm, WIDTH), 0) >> (s.bit_length() - 1))
    wsum = jnp.where(lane_grp == 0, s2, jnp.where(lane_grp == 1, s4, jnp.where(lane_grp == 2, s8, s16)))
    win = jnp.where(lane_grp == 0, 2, jnp.where(lane_grp == 1, 4, jnp.where(lane_grp == 2, 8, 16)))
    count = jnp.minimum(pos + 1, win).astype(F32)
    d = wsum / count - cur
    y_ref[...] = _dot(d.astype(BF16), w_ref[...]) * scale_ref[...]
    e1[0:hr, :] = e1[tm:tot, :]


def _pool(seq, col_blk, h0, w_bd, scale, *, n_seq, s, tm, pos0):
    m = seq.shape[0]
    tiles = m // n_seq // tm
    hr = POOL_PAD_STEPS * s
    body = functools.partial(_pool_body, s=s, tm=tm, pos0=pos0, steps_per_tile=tm // s)
    return pl.pallas_call(
        body,
        grid=(n_seq, tiles),
        in_specs=[pl.BlockSpec((tm, WIDTH), lambda n, j: (n * tiles + j, col_blk)),
                  pl.BlockSpec((1, hr, WIDTH), lambda n, j: (n, 0, 0)),
                  pl.BlockSpec((WIDTH, WIDTH), lambda n, j: (0, 0)),
                  pl.BlockSpec((1, WIDTH), lambda n, j: (0, 0))],
        out_specs=pl.BlockSpec((tm, WIDTH), lambda n, j: (n * tiles + j, 0)),
        out_shape=jax.ShapeDtypeStruct((m, WIDTH), F32),
        scratch_shapes=[pltpu.VMEM((hr + tm, WIDTH), F32) for _ in range(4)],
        compiler_params=_cparams(("arbitrary", "arbitrary")),
        name="mixer_b",
    )(seq, h0, w_bd, scale)


LOG2E = math.log2(math.e)


V_ROWS = 2 * FOX_HEAD_DIM


def _fox_prep_body(q_ref, k_ref, v_ref, sm_ref, bf_ref, selq_ref, selk_ref, selv_ref,
                   lf_ref, qa_ref, ka_ref, va_ref, carry, *, tm):
    j = pl.program_id(1)

    @pl.when(j == 0)
    def _():
        carry[...] = jnp.zeros_like(carry)

    logf = _log_sigmoid(sm_ref[...] + bf_ref[...])
    lf_ref[...] = logf
    tril = (_iota((tm, tm), 0) >= _iota((tm, tm), 1)).astype(BF16)
    c = _ones_dot_left(tril, logf) + carry[...]
    carry[...] = c[tm - 1:tm, :]
    c = c * LOG2E
    c_t = jnp.transpose(c)
    qb = (q_ref[...] * (FOX_HEAD_DIM ** -0.5 * LOG2E)).astype(BF16)
    kb = k_ref[...].astype(BF16)
    vb = v_ref[...].astype(BF16)
    d = FOX_HEAD_DIM
    lane = _iota((tm, 2 * d), 1)
    row = _iota((2 * d, tm), 0)

    def pieces(x):
        hi = x.astype(BF16).astype(F32)
        r1 = x - hi
        mid = r1.astype(BF16).astype(F32)
        return hi, mid, (r1 - mid).astype(BF16).astype(F32)

    for h in range(FOX_HEADS):
        hi, mid, lo = pieces(jnp.broadcast_to(c_t[h:h + 1, :], (2 * d, tm)))
        q_extra = jnp.where(row == d, hi, jnp.where(row == d + 1, mid, jnp.where(row == d + 2, lo,
                  jnp.where((row >= d + 3) & (row < d + 6), 1.0, 0.0))))
        qa_ref[0, h] = (_dot_nt(selq_ref[h], qb) + q_extra).astype(BF16)
        hi, mid, lo = pieces(jnp.broadcast_to(c[:, h:h + 1], (tm, 2 * d)))
        k_extra = jnp.where(lane == d + 3, -hi, jnp.where(lane == d + 4, -mid, jnp.where(lane == d + 5, -lo,
                  jnp.where((lane >= d) & (lane < d + 3), 1.0, 0.0))))
        ka_ref[0, h] = (_dot(kb, selk_ref[h]) + k_extra).astype(BF16)
        ones_row = jnp.where(row[:V_ROWS] == d, 1.0, 0.0)
        va_ref[0, h] = (_dot_nt(selv_ref[h], vb) + ones_row).astype(BF16)


def _fox_prep(p, bf_row, selq, selk, selv, *, n_seq, tm):
    m = p.shape[0]
    seq_len = m // n_seq
    tiles = seq_len // tm
    aug = jax.ShapeDtypeStruct((n_seq, FOX_HEADS, seq_len, 2 * FOX_HEAD_DIM), BF16)
    aug_spec = pl.BlockSpec((1, FOX_HEADS, tm, 2 * FOX_HEAD_DIM), lambda n, j: (n, 0, j, 0))
    def transposed(rows):
        return (jax.ShapeDtypeStruct((n_seq, FOX_HEADS, rows, seq_len), BF16),
                pl.BlockSpec((1, FOX_HEADS, rows, tm), lambda n, j: (n, 0, 0, j)),
                pl.BlockSpec((FOX_HEADS, rows, WIDTH), lambda n, j: (0, 0, 0)))

    aug_t, aug_t_spec, selq_spec = transposed(2 * FOX_HEAD_DIM)
    v_t, v_t_spec, selv_spec = transposed(V_ROWS)
    sel_spec = pl.BlockSpec((FOX_HEADS, WIDTH, 2 * FOX_HEAD_DIM), lambda n, j: (0, 0, 0))
    return pl.pallas_call(
        functools.partial(_fox_prep_body, tm=tm),
        grid=(n_seq, tiles),
        in_specs=[pl.BlockSpec((tm, WIDTH), lambda n, j: (n * tiles + j, U_Q)),
                  pl.BlockSpec((tm, WIDTH), lambda n, j: (n * tiles + j, U_K)),
                  pl.BlockSpec((tm, WIDTH), lambda n, j: (n * tiles + j, U_V)),
                  pl.BlockSpec((tm, SMALL_W), lambda n, j: (n * tiles + j, SMALL_BLK)),
                  pl.BlockSpec((1, SMALL_W), lambda n, j: (0, 0)),
                  selq_spec, sel_spec, selv_spec],
        out_specs=[pl.BlockSpec((tm, SMALL_W), lambda n, j: (n * tiles + j, 0)),
                   aug_t_spec, aug_spec, v_t_spec],
        out_shape=[jax.ShapeDtypeStruct((m, SMALL_W), F32), aug_t, aug, v_t],
        scratch_shapes=[pltpu.VMEM((1, SMALL_W), F32)],
        compiler_params=_cparams(("arbitrary", "arbitrary")),
        name="fox_prep",
    )(p, p, p, p, bf_row, selq, selk, selv)


def _fox_flash_body(qi_ref, kj_ref, q_ref, k_ref, v_ref, o_ref, m_ref, acc_ref, *, tq, wq):
    step = pl.program_id(1)
    qi = qi_ref[step]
    kj = kj_ref[step]
    d = FOX_HEAD_DIM

    @pl.when(kj == 0)
    def _():
        m_ref[...] = jnp.full_like(m_ref, -jnp.inf)
        acc_ref[...] = jnp.zeros_like(acc_ref)

    n_split = tq // wq
    chains = [(h, c) for h in range(FOX_HEADS) for c in range(n_split)]

    def update(diagonal):
        n_keys = [(c + 1) * wq if diagonal else tq for _, c in chains]
        scores = [_dot(k_ref[0, h, 0:nk, :], q_ref[0, h, :, c * wq:(c + 1) * wq])
                  for (h, c), nk in zip(chains, n_keys)]
        for i, ((h, c), nk, s) in enumerate(zip(chains, n_keys, scores)):
            if diagonal:
                s = jnp.where(_iota((nk, wq), 0) <= _iota((nk, wq), 1) + c * wq, s, -jnp.inf)
            m_prev = m_ref[i]
            m_new = jnp.maximum(m_prev, jnp.max(s, axis=0, keepdims=True))
            alpha = jnp.exp2(m_prev - m_new)
            pexp = jnp.exp2(s - m_new).astype(BF16)
            acc_ref[i] = alpha * acc_ref[i] + _dot(v_ref[0, h, :, 0:nk], pexp)
            m_ref[i] = m_new

    @pl.when(kj < qi)
    def _():
        update(False)

    @pl.when(kj == qi)
    def _():
        update(True)
        o = []
        for h in range(FOX_HEADS):
            parts = []
            for c in range(n_split):
                acc = acc_ref[h * n_split + c]
                parts.append(acc[0:d, :] / acc[d:d + 1, :])
            o.append(jnp.concatenate(parts, axis=1))
        o_ref[...] = jnp.concatenate([jnp.transpose(jnp.concatenate(o[0:2], axis=0)),
                                      jnp.transpose(jnp.concatenate(o[2:4], axis=0))], axis=1)


def _fox_flash(qa_t, ka, va_t, *, tq, wq):
    n_seq, _, seq_len, w = ka.shape
    nq = seq_len // tq
    qi = np.array([i for i in range(nq) for _ in range(i + 1)], np.int32)
    kj = np.array([j for i in range(nq) for j in range(i + 1)], np.int32)
    q_spec = pl.BlockSpec((1, FOX_HEADS, w, tq), lambda n, s, qi, kj: (n, 0, 0, qi[s]))
    k_spec = pl.BlockSpec((1, FOX_HEADS, tq, w), lambda n, s, qi, kj: (n, 0, kj[s], 0))
    v_spec = pl.BlockSpec((1, FOX_HEADS, V_ROWS, tq), lambda n, s, qi, kj: (n, 0, 0, kj[s]))
    grid_spec = pltpu.PrefetchScalarGridSpec(
        num_scalar_prefetch=2,
        grid=(n_seq, len(qi)),
        in_specs=[q_spec, k_spec, v_spec],
        out_specs=pl.BlockSpec((tq, WIDTH), lambda n, s, qi, kj: (n * nq + qi[s], 0)),
        scratch_shapes=[pltpu.VMEM((FOX_HEADS * (tq // wq), 1, wq), F32),
                        pltpu.VMEM((FOX_HEADS * (tq // wq), V_ROWS, wq), F32)],
    )
    return pl.pallas_call(
        functools.partial(_fox_flash_body, tq=tq, wq=wq),
        grid_spec=grid_spec,
        out_shape=jax.ShapeDtypeStruct((n_seq * seq_len, WIDTH), F32),
        compiler_params=_cparams(("arbitrary", "arbitrary")),
        name="fox_flash",
    )(jnp.asarray(qi), jnp.asarray(kj), qa_t, ka, va_t)


TOK_PAD = 8
HEAD_PAD = 8


DECODE_SEQS = 2
DECODE_SLOTS = 2 * DECODE_SEQS


def _fox_decode_body(pt_ref, q_ref, kn_ref, vn_ref, fr_ref, bfc_ref, ck_hbm, cv_hbm, cl_hbm, o_ref,
                     kbuf, vbuf, lbuf, sems, *, layer, n_seq, n_pages, n_tok):
    b = pl.program_id(0)

    def page_copies(page, slot, pg):
        return [pltpu.make_async_copy(src.at[layer, page], dst.at[slot, pg], sems.at[slot])
                for src, dst in ((ck_hbm, kbuf), (cv_hbm, vbuf), (cl_hbm, lbuf))]

    def fetch(seq):
        @pl.when(seq < n_seq)
        def _():
            for pg in range(n_pages):
                for cp in page_copies(pt_ref[seq, pg], seq % DECODE_SLOTS, pg):
                    cp.start()

    @pl.when(b == 0)
    def _():
        for seq in range(DECODE_SLOTS):
            fetch(seq)

    seqs = [b * DECODE_SEQS + u for u in range(DECODE_SEQS)]
    for seq in seqs:
        for pg in range(n_pages):
            for cp in page_copies(0, seq % DECODE_SLOTS, pg):
                cp.wait()
    for u, seq in enumerate(seqs):
        slot = seq % DECODE_SLOTS
        o_ref[u] = _fox_decode_one(q_ref[u], kn_ref[u], vn_ref[u], fr_ref[u], bfc_ref[...],
                                   [kbuf.at[slot, pg] for pg in range(n_pages)],
                                   [vbuf.at[slot, pg] for pg in range(n_pages)],
                                   [lbuf.at[slot, pg] for pg in range(n_pages)], n_tok=n_tok)
    for seq in seqs:
        fetch(seq + DECODE_SLOTS)


def _fox_decode_one(q, k_new8, v_new8, fr, bfc, k_refs, v_refs, lf_refs, *, n_tok):
    n_pages = len(k_refs)
    d = FOX_HEAD_DIM
    ps = PAGE_SIZE
    qrows = n_tok * HEAD_PAD

    q4 = q * (d ** -0.5)
    own = (_iota((HEAD_PAD, WIDTH), 1) >> 6) == _iota((HEAD_PAD, WIDTH), 0)
    qbd = jnp.concatenate([jnp.where(own, jnp.broadcast_to(q4[t:t + 1, :], (HEAD_PAD, WIDTH)), 0.0)
                           for t in range(n_tok)], axis=0).astype(BF16)

    triu = (_iota((ps, ps), 0) <= _iota((ps, ps), 1)).astype(BF16)
    ones = jnp.ones((ps, ps), BF16)
    lf_new = _log_sigmoid(fr + bfc)
    newcum_lane = _ones_dot_right(lf_new, triu)
    row_t = _iota((qrows, ps), 0) >> 3
    lane_i = _iota((qrows, ps), 1)
    newcum_col = jnp.sum(jnp.where(lane_i == row_t, newcum_lane, 0.0), axis=1, keepdims=True)

    lst = jnp.concatenate([r[...] for r in lf_refs], axis=0)
    incl = _ones_dot_right(lst, triu)
    tot = _ones_dot_right(lst, ones)
    later = jnp.zeros((HEAD_PAD, ps), F32)
    bias = [None] * n_pages
    for pg in reversed(range(n_pages)):
        rows = slice(pg * HEAD_PAD, (pg + 1) * HEAD_PAD)
        g = (tot[rows] - incl[rows]) + later
        bias[pg] = jnp.concatenate([g] * n_tok, axis=0) + newcum_col
        later = later + tot[rows]

    k_new = jnp.concatenate([k_new8, jnp.zeros((ps - TOK_PAD, WIDTH), F32)], axis=0).astype(BF16)
    v_new = jnp.concatenate([v_new8, jnp.zeros((ps - TOK_PAD, WIDTH), F32)], axis=0).astype(BF16)
    s_new = _dot_nt(qbd, k_new) + newcum_col - newcum_lane
    s_new = jnp.where(lane_i <= row_t, s_new, -jnp.inf)
    s_past = [_dot(qbd, k_refs[pg][...].astype(BF16)) + bias[pg] for pg in range(n_pages)]

    m_lane = s_new
    for sp in s_past:
        m_lane = jnp.maximum(m_lane, sp)
    m = jnp.max(m_lane, axis=1, keepdims=True)
    p_new = jnp.exp(s_new - m)
    l_lane = p_new
    acc = _dot(p_new.astype(BF16), v_new)
    for pg in range(n_pages):
        pp = jnp.exp(s_past[pg] - m)
        l_lane = l_lane + pp
        acc = acc + _dot_nt(pp.astype(BF16), v_refs[pg][...].astype(BF16))
    o = acc / jnp.sum(l_lane, axis=1, keepdims=True)
    keep = jnp.concatenate([own] * n_tok, axis=0)
    return jnp.sum(jnp.where(keep, o, 0.0).reshape(n_tok, HEAD_PAD, WIDTH), axis=1)


def _fox_decode(layer, page_table, q, kn8, vn8, fr_lane, bf_col, cache_k, cache_v, cache_lft):
    n_b, n_pages = page_table.shape
    n_tok = q.shape[1]
    qrows = n_tok * HEAD_PAD
    ps = PAGE_SIZE

    assert n_b % DECODE_SEQS == 0
    per_seq = lambda rows, cols: pl.BlockSpec((DECODE_SEQS, rows, cols), lambda b, pt: (b, 0, 0))
    hbm = pl.BlockSpec(memory_space=pl.ANY)
    grid_spec = pltpu.PrefetchScalarGridSpec(
        num_scalar_prefetch=1,
        grid=(n_b // DECODE_SEQS,),
        in_specs=[per_seq(n_tok, WIDTH), per_seq(TOK_PAD, WIDTH), per_seq(TOK_PAD, WIDTH),
                  per_seq(qrows, ps),
                  pl.BlockSpec((qrows, ps), lambda b, pt: (0, 0)),
                  hbm, hbm, hbm],
        out_specs=per_seq(n_tok, WIDTH),
        scratch_shapes=[pltpu.VMEM((DECODE_SLOTS, n_pages, WIDTH, ps), F32),
                        pltpu.VMEM((DECODE_SLOTS, n_pages, WIDTH, ps), F32),
                        pltpu.VMEM((DECODE_SLOTS, n_pages, HEAD_PAD, ps), F32),
                        pltpu.SemaphoreType.DMA((DECODE_SLOTS,))],
    )
    return pl.pallas_call(
        functools.partial(_fox_decode_body, layer=layer, n_seq=n_b, n_pages=n_pages, n_tok=n_tok),
        grid_spec=grid_spec,
        out_shape=jax.ShapeDtypeStruct((n_b, n_tok, WIDTH), F32),
        compiler_params=_cparams(("arbitrary",)),
        name="fox_decode",
    )(page_table, q, kn8, vn8, fr_lane, bf_col, cache_k, cache_v, cache_lft)


def _ssd_prompt_body(xs_ref, bm_ref, cm_ref, z_ref, sm_ref, cw_ref, cb_ref, dtb_ref, arow_ref, drow_ref, g_ref,
                     y_ref, hl_ref, ext, state, *, t, n_seq):
    j = pl.program_id(0)

    @pl.when(j == 0)
    def _():
        ext[:, 0:8, :] = jnp.zeros((n_seq, 8, SSD_CONV_DIM), F32)
        state[...] = jnp.zeros_like(state)

    for s in range(n_seq):
        _ssd_chunk(xs_ref.at[s], bm_ref.at[s], cm_ref.at[s], z_ref.at[s], sm_ref.at[s], cw_ref, cb_ref, dtb_ref,
                   arow_ref, drow_ref, g_ref, y_ref.at[s], ext.at[s], state.at[s], t=t)

    @pl.when(j == pl.num_programs(0) - 1)
    def _():
        hl_ref[...] = state[...]


def _ssd_chunk(xs_ref, bm_ref, cm_ref, z_ref, sm_ref, cw_ref, cb_ref, dtb_ref, arow_ref, drow_ref, g_ref,
               y_ref, ext, state, *, t):
    hd = SSD_HEAD_DIM
    w3 = SSD_CONV_DIM

    ext[8:8 + t, 0:WIDTH] = xs_ref[...]
    ext[8:8 + t, WIDTH:2 * WIDTH] = bm_ref[...]
    ext[8:8 + t, 2 * WIDTH:w3] = cm_ref[...]
    conv = cb_ref[...]
    for k in range(SSD_CONV):
        conv = conv + cw_ref[k:k + 1, :] * ext[5 + k:5 + k + t, :]
    ext[0:8, :] = ext[t:t + 8, :]
    act = _silu(conv)
    xs = act[:, 0:WIDTH]
    bm = act[:, WIDTH:2 * WIDTH].astype(BF16)
    cm = act[:, 2 * WIDTH:w3].astype(BF16)

    dt = _softplus(sm_ref[...] + dtb_ref[...])
    da = dt * arow_ref[...]
    tril = (_iota((t, t), 0) >= _iota((t, t), 1))
    acs = _ones_dot_left(tril.astype(BF16), da)
    acs_t = jnp.transpose(acs)
    lane_head = _iota((t, WIDTH), 1) >> 6
    dt_x = jnp.zeros((t, WIDTH), F32)
    end_x = jnp.zeros((t, WIDTH), F32)
    in_x = jnp.zeros((t, WIDTH), F32)
    dec_rows = []
    for h in range(SSD_HEADS):
        col = acs[:, 4 + h:5 + h]
        lastv = acs[t - 1:t, 4 + h:5 + h]
        dt_x = jnp.where(lane_head == h, dt[:, 4 + h:5 + h], dt_x)
        end_x = jnp.where(lane_head == h, jnp.exp(lastv - col), end_x)
        in_x = jnp.where(lane_head == h, jnp.exp(col), in_x)
        dec_rows.append(jnp.broadcast_to(jnp.exp(lastv), (hd, SSD_STATE)))
    xdt = xs * dt_x
    xdt_b = xdt.astype(BF16)

    y = jnp.zeros((t, WIDTH), F32)
    h_in = state[...]
    xw_t = jnp.transpose(xdt * end_x).astype(BF16)
    new_states = []
    for gi in range(SSD_GROUPS):
        bg = bm[:, gi * SSD_STATE:(gi + 1) * SSD_STATE]
        cg = cm[:, gi * SSD_STATE:(gi + 1) * SSD_STATE]
        cbm = _dot_nt(cg, bg)
        rows = slice(gi * 2 * hd, (gi + 1) * 2 * hd)
        for r in range(SSD_HEADS // SSD_GROUPS):
            h = gi * (SSD_HEADS // SSD_GROUPS) + r
            seg = acs[:, 4 + h:5 + h] - acs_t[4 + h:5 + h, :]
            decay = jnp.exp(jnp.where(tril, seg, -jnp.inf))
            yd = _dot((cbm * decay).astype(BF16), xdt_b)
            y = y + jnp.where(lane_head == h, yd, 0.0)
        y_off = _dot_nt(cg, h_in[rows, :].astype(BF16))
        y_off = jnp.concatenate([y_off, y_off], axis=1)
        grp_lane = _iota((t, WIDTH), 1) >> 7
        y = y + jnp.where(grp_lane == gi, y_off * in_x, 0.0)
        new_states.append(_dot(xw_t[rows, :], bg))
    state[...] = h_in * jnp.concatenate(dec_rows, axis=0) + jnp.concatenate(new_states, axis=0)

    yg = (y + drow_ref[...] * xs) * _silu(z_ref[...])
    y_ref[...] = _rms(yg, g_ref[...])


def _ssd_prompt(p, cw, cb, dtb_row, a_row, d_row, g, *, n_seq):
    m = p.shape[0]
    t = SSD_CHUNK
    seq_len = m // n_seq
    hp = SSD_HEADS * SSD_HEAD_DIM
    p3 = p.reshape(n_seq, seq_len, P_COLS)

    def unit(u):
        return pl.BlockSpec((n_seq, t, WIDTH), lambda j: (0, j, u))

    def const(shape):
        return pl.BlockSpec(shape, lambda j: (0,) * len(shape))

    y, h_last = pl.pallas_call(
        functools.partial(_ssd_prompt_body, t=t, n_seq=n_seq),
        grid=(seq_len // t,),
        in_specs=[unit(U_XS), unit(U_BM), unit(U_CM), unit(U_Z),
                  pl.BlockSpec((n_seq, t, SMALL_W), lambda j: (0, j, SMALL_BLK)),
                  const((SSD_CONV, SSD_CONV_DIM)), const((1, SSD_CONV_DIM)),
                  const((1, SMALL_W)), const((1, SMALL_W)), const((1, WIDTH)), const((1, WIDTH))],
        out_specs=[pl.BlockSpec((n_seq, t, WIDTH), lambda j: (0, j, 0)),
                   const((n_seq, hp, SSD_STATE))],
        out_shape=[jax.ShapeDtypeStruct((n_seq, seq_len, WIDTH), F32),
                   jax.ShapeDtypeStruct((n_seq, hp, SSD_STATE), F32)],
        scratch_shapes=[pltpu.VMEM((n_seq, 8 + t, SSD_CONV_DIM), F32),
                        pltpu.VMEM((n_seq, hp, SSD_STATE), F32)],
        compiler_params=_cparams(("arbitrary",)),
        name="ssd_prompt",
    )(p3, p3, p3, p3, p3, cw, cb, dtb_row, a_row, d_row, g)
    return y.reshape(m, WIDTH), h_last


def _conv_silu_body(ext_ref, w_ref, b_ref, o_ref, *, s, taps, rows):
    acc = b_ref[...]
    for k in range(taps):
        acc = acc + w_ref[k:k + 1, :] * ext_ref[k * s:k * s + rows, :]
    o_ref[...] = _silu(acc)


def _conv_silu(ext, w, b, *, s, rows):
    taps = w.shape[0]
    c = ext.shape[1]
    return pl.pallas_call(
        functools.partial(_conv_silu_body, s=s, taps=taps, rows=rows),
        out_shape=jax.ShapeDtypeStruct((rows, c), F32),
        compiler_params=pltpu.CompilerParams(vmem_limit_bytes=VMEM_LIMIT),
        name="conv_silu",
    )(ext, w, b)


def _ssd_step_body(x_ref, b_ref, c_ref, dtr_ref, dtb_ref, a_ref, h0_ref, y_ref, hl_ref, *, n_tok):
    dt = _softplus(dtr_ref[...] + dtb_ref[...])
    da = dt * a_ref[...]
    h = h0_ref[...]
    lane_t = _iota(y_ref.shape, 2)
    y = jnp.zeros(y_ref.shape, F32)
    for t in range(n_tok):
        xdt = x_ref[:, :, t:t + 1] * dt[:, :, t:t + 1]
        h = h * jnp.exp(da[:, :, t:t + 1]) + xdt * b_ref[:, t:t + 1, :]
        yt = jnp.sum(h * c_ref[:, t:t + 1, :], axis=2, keepdims=True)
        y = jnp.where(lane_t == t, yt, y)
    y_ref[...] = y
    hl_ref[...] = h


def _ssd_step(x_bh, b_bh, c_bh, dtr_bh, dtb_bh, a_bh, h0, *, blk):
    bh, hd, n_tok = x_bh.shape
    ns = h0.shape[2]

    def spec(shape):
        return pl.BlockSpec((blk,) + shape, lambda i: (i, 0, 0))

    return pl.pallas_call(
        functools.partial(_ssd_step_body, n_tok=n_tok),
        grid=(bh // blk,),
        in_specs=[spec((hd, n_tok)), spec((n_tok, ns)), spec((n_tok, ns)), spec((1, n_tok)),
                  spec((1, 1)), spec((1, 1)), spec((hd, ns))],
        out_specs=[spec((hd, n_tok)), spec((hd, ns))],
        out_shape=[jax.ShapeDtypeStruct((bh, hd, n_tok), F32),
                   jax.ShapeDtypeStruct((bh, hd, ns), F32)],
        compiler_params=_cparams(("arbitrary",)),
        name="ssd_step",
    )(x_bh, b_bh, c_bh, dtr_bh, dtb_bh, a_bh, h0)


def _ssd_gate_body(y_ref, xs_ref, z_ref, d_ref, g_ref, o_ref):
    yg = (y_ref[...] + d_ref[...] * xs_ref[...]) * _silu(z_ref[...])
    o_ref[...] = _rms(yg, g_ref[...])


def _ssd_gate(y, xs, z, d_row, g):
    return pl.pallas_call(
        _ssd_gate_body,
        out_shape=jax.ShapeDtypeStruct(y.shape, F32),
        name="ssd_gate",
    )(y, xs, z, d_row, g)


def _merge_body(x_ref, g_ref, ya_ref, yb_ref, yc_ref, yd_ref, wg_ref, wb_ref, wo_ref, o_ref):
    x = x_ref[...]
    h = _rms(x, g_ref[...]).astype(BF16)
    merged = jnp.zeros(x.shape, F32)
    for i, y_ref in enumerate((ya_ref, yb_ref, yc_ref, yd_ref)):
        gate = _sigmoid(_dot(h, wg_ref[:, i * D_MODEL:(i + 1) * D_MODEL]))
        merged = merged + gate * _dot(y_ref[...].astype(BF16), wb_ref[i])
    o_ref[...] = x + _dot(merged.astype(BF16), wo_ref[...])


def _merge(x, g, ys, wg, wb, wo, tm):
    m = x.shape[0]
    row = lambda i: (i, 0)
    y_spec = pl.BlockSpec((tm, WIDTH), row)
    return pl.pallas_call(
        _merge_body,
        grid=(m // tm,),
        in_specs=[pl.BlockSpec((tm, D_MODEL), row),
                  pl.BlockSpec((1, D_MODEL), lambda i: (0, 0)),
                  y_spec, y_spec, y_spec, y_spec,
                  pl.BlockSpec((D_MODEL, N_BRANCH * D_MODEL), lambda i: (0, 0)),
                  pl.BlockSpec((N_BRANCH, WIDTH, D_MODEL), lambda i: (0, 0, 0)),
                  pl.BlockSpec((D_MODEL, D_MODEL), lambda i: (0, 0))],
        out_specs=pl.BlockSpec((tm, D_MODEL), row),
        out_shape=jax.ShapeDtypeStruct((m, D_MODEL), F32),
        compiler_params=_cparams(("arbitrary",)),
        name="merge",
    )(x, g, *ys, wg, wb, wo)


def _ffn_body(x_ref, g_ref, h0g_ref, h0u_ref, wug_ref, wuu_ref, cwg_ref, cwu_ref, cbg_ref, cbu_ref, wd_ref,
              o_ref, hg_ref, hu_ref, h2, acc, wkg, wku, histg, histu, *, s, tm, hr):
    j = pl.program_id(1)
    c = pl.program_id(2)
    last_c = pl.num_programs(2) - 1

    @pl.when(c == 0)
    def _():
        h2[...] = _rms(x_ref[...], g_ref[...]).astype(BF16)
        acc[...] = jnp.zeros_like(acc)

    @pl.when(j == 0)
    def _():
        histg[c] = h0g_ref[0]
        histu[c] = h0u_ref[0]

    def conv(wk, hist, wu_ref, cw_ref, cb_ref):
        wk[0:hr, :] = hist[c]
        wk[hr:hr + tm, :] = _dot(h2[...], wu_ref[...])
        out = cb_ref[...]
        for k in range(FFN_CONV):
            off = hr - (FFN_CONV - 1 - k) * s
            out = out + cw_ref[k:k + 1, :] * wk[off:off + tm, :]
        hist[c] = wk[tm:tm + hr, :]
        return out

    yg = conv(wkg, histg, wug_ref, cwg_ref, cbg_ref)
    yu = conv(wku, histu, wuu_ref, cwu_ref, cbu_ref)
    acc[...] += _dot((_silu(yg) * yu).astype(BF16), wd_ref[...])
    hg_ref[0, c] = histg[c]
    hu_ref[0, c] = histu[c]

    @pl.when(c == last_c)
    def _():
        o_ref[...] = x_ref[...] + acc[...]


def _ffn(x, g, h0, wu, cw, cb, wd, *, n_seq, s, tm, cwid):
    m = x.shape[0]
    tiles = m // n_seq // tm
    hr = h0.shape[1]
    nc = D_FF // cwid
    row = lambda n, j, c: (n * tiles + j, 0)
    hist_g = lambda n, j, c: (n, 0, c)
    hist_u = lambda n, j, c: (n, 0, nc + c)
    col_g = lambda n, j, c: (0, c)
    col_u = lambda n, j, c: (0, nc + c)
    hist_shape = jax.ShapeDtypeStruct((n_seq, nc, hr, cwid), F32)
    hist_out = pl.BlockSpec((1, nc, hr, cwid), lambda n, j, c: (n, 0, 0, 0))
    once = dict(pipeline_mode=pl.Buffered(1)) if nc == 1 else {}
    x_new, hg, hu = pl.pallas_call(
        functools.partial(_ffn_body, s=s, tm=tm, hr=hr),
        grid=(n_seq, tiles, nc),
        in_specs=[pl.BlockSpec((tm, D_MODEL), row),
                  pl.BlockSpec((1, D_MODEL), lambda n, j, c: (0, 0)),
                  pl.BlockSpec((1, hr, cwid), hist_g), pl.BlockSpec((1, hr, cwid), hist_u),
                  pl.BlockSpec((D_MODEL, cwid), col_g, **once), pl.BlockSpec((D_MODEL, cwid), col_u, **once),
                  pl.BlockSpec((FFN_CONV, cwid), col_g), pl.BlockSpec((FFN_CONV, cwid), col_u),
                  pl.BlockSpec((1, cwid), col_g), pl.BlockSpec((1, cwid), col_u),
                  pl.BlockSpec((cwid, D_MODEL), lambda n, j, c: (c, 0), **once)],
        out_specs=[pl.BlockSpec((tm, D_MODEL), row), hist_out, hist_out],
        out_shape=[jax.ShapeDtypeStruct((m, D_MODEL), F32), hist_shape, hist_shape],
        scratch_shapes=[pltpu.VMEM((tm, D_MODEL), BF16),
                        pltpu.VMEM((tm, D_MODEL), F32),
                        pltpu.VMEM((hr + tm, cwid), F32),
                        pltpu.VMEM((hr + tm, cwid), F32),
                        pltpu.VMEM((nc, hr, cwid), F32),
                        pltpu.VMEM((nc, hr, cwid), F32)],
        compiler_params=_cparams(("arbitrary", "arbitrary", "arbitrary")),
        name="ffn",
    )(x, g, h0, h0, wu, wu, cw, cw, cb, cb, wd)
    to_rows = lambda a: jnp.swapaxes(a, 1, 2).reshape(n_seq, hr, D_FF)
    return x_new, jnp.concatenate([to_rows(hg), to_rows(hu)], axis=2)


def _final_norm_body(x_ref, g_ref, o_ref):
    o_ref[...] = _rms(x_ref[...], g_ref[...])


def _final_norm(x, g, tm):
    m, d = x.shape
    return pl.pallas_call(
        _final_norm_body,
        grid=(m // tm,),
        in_specs=[pl.BlockSpec((tm, d), lambda i: (i, 0)), pl.BlockSpec((1, d), lambda i: (0, 0))],
        out_specs=pl.BlockSpec((tm, d), lambda i: (i, 0)),
        out_shape=jax.ShapeDtypeStruct((m, d), F32),
        compiler_params=_cparams(("arbitrary",)),
        name="final_norm",
    )(x, g)


def _selection_matrices():
    d = FOX_HEAD_DIM
    selq = np.zeros((FOX_HEADS, 2 * d, WIDTH), np.float32)
    selk = np.zeros((FOX_HEADS, WIDTH, 2 * d), np.float32)
    selv = np.zeros((FOX_HEADS, V_ROWS, WIDTH), np.float32)
    for h in range(FOX_HEADS):
        for i in range(d):
            selq[h, i, h * d + i] = 1.0
            selk[h, h * d + i, i] = 1.0
            selv[h, i, h * d + i] = 1.0
    return (jnp.asarray(selq, BF16), jnp.asarray(selk, BF16), jnp.asarray(selv, BF16))


def _layer_params(l, norm1_g, w_in, cm_norm_g, cm_ws, cm_b, pool_w, pool_scale, fox_bf, ssd_conv_w, ssd_conv_b,
                  ssd_dt_bias, ssd_a_log, ssd_d, ssd_norm_g, w_branch, w_out, norm2_g, ffn_up, ffn_conv_w,
                  ffn_conv_b, ffn_down, n_dec, n_tok):
    w = w_in[l]
    o_f = 2 * WIDTH + WIDTH + 3 * WIDTH
    o_z = o_f + FOX_HEADS
    o_xbc = o_z + WIDTH
    o_dt = o_xbc + SSD_CONV_DIM
    o_gate = o_dt + SSD_HEADS
    w_p = jnp.concatenate([w[:, :o_f], w[:, o_z:o_dt], w[:, o_f:o_z], w[:, o_dt:o_gate],
                           jnp.zeros((D_MODEL, SMALL_W - FOX_HEADS - SSD_HEADS), F32)], axis=1).astype(BF16)
    causal = jnp.tril(jnp.ones((CM_CHUNK, CM_CHUNK), F32))
    ws = cm_ws[l] * causal
    gd = WIDTH // CM_GROUPS
    eye = jnp.eye(n_dec, dtype=F32)
    pd = len(POOL_WINDOWS)
    pw = pool_w[l].astype(F32)
    pool_bd = jnp.zeros((WIDTH, WIDTH), F32)
    pg = WIDTH // pd
    for gi in range(pd):
        pool_bd = pool_bd.at[gi * pg:(gi + 1) * pg, gi * pg:(gi + 1) * pg].set(pw[gi])
    lanes = jnp.arange(SMALL_W)
    head_of_lane = jnp.clip(lanes - FOX_HEADS, 0, SSD_HEADS - 1)
    dt_lane = (lanes >= FOX_HEADS) & (lanes < FOX_HEADS + SSD_HEADS)
    a = -jnp.exp(ssd_a_log[l].astype(F32))
    return dict(
        g1=norm1_g[l][None, :],
        w_p=w_p,
        w_gate=w[:, o_gate:].astype(BF16),
        cm_g=cm_norm_g[l][None, :],
        ws_prompt=ws.astype(BF16),
        cmb_prompt=jnp.repeat(cm_b[l].T, gd, axis=1),
        ws_sample=jnp.stack([jnp.kron(ws[gi, :n_tok, :n_tok], eye) for gi in range(CM_GROUPS)]).astype(BF16),
        cmb_sample=jnp.repeat(jnp.repeat(cm_b[l].T[:n_tok], n_dec, axis=0), gd, axis=1),
        pool_bd=pool_bd.astype(BF16),
        pool_scale=pool_scale[l][None, :].astype(F32),
        bf_row=jnp.pad(fox_bf[l], (0, SMALL_W - FOX_HEADS))[None, :],
        bf_col=jnp.broadcast_to(jnp.tile(jnp.pad(fox_bf[l], (0, HEAD_PAD - FOX_HEADS)), n_tok)[:, None],
                                (n_tok * HEAD_PAD, PAGE_SIZE)),
        conv_w=ssd_conv_w[l], conv_b=ssd_conv_b[l][None, :],
        dtb_row=jnp.where(dt_lane, ssd_dt_bias[l][head_of_lane], 0.0)[None, :],
        a_row=jnp.where(dt_lane, a[head_of_lane], 0.0)[None, :],
        d_row=jnp.repeat(ssd_d[l].astype(F32), SSD_HEAD_DIM)[None, :],
        dtb_bh=jnp.tile(ssd_dt_bias[l], n_dec)[:, None, None],
        a_bh=jnp.tile(a, n_dec)[:, None, None],
        ssd_g=ssd_norm_g[l][None, :],
        w_branch=w_branch[l].astype(BF16),
        w_out=w_out[l].astype(BF16),
        g2=norm2_g[l][None, :],
        ffn_up=ffn_up[l].astype(BF16),
        ffn_cw=ffn_conv_w[l], ffn_cb=ffn_conv_b[l][None, :],
        ffn_down=ffn_down[l].astype(BF16),
    )


def _prompt_layer(x, w, consts, n_seq):
    m = x.shape[0]
    seq_len = m // n_seq
    selq, selk, selv = consts["sel"]
    p = _proj(x, w["g1"], w["w_p"], tm=512)
    y_a, _ = _mixa(p, w["ws_prompt"], w["cmb_prompt"], w["cm_g"], t=CM_CHUNK, tm=512)
    y_b = _pool(p, U_POOL, jnp.zeros((n_seq, POOL_PAD_STEPS, WIDTH), F32), w["pool_bd"], w["pool_scale"],
                n_seq=n_seq, s=1, tm=512, pos0=0)
    lf, qa, ka, va = _fox_prep(p, w["bf_row"], selq, selk, selv, n_seq=n_seq, tm=512)
    tq = 1024 if seq_len % 1024 == 0 else 512
    y_c = _fox_flash(qa, ka, va, tq=tq, wq=512)
    y_d, ssd_last = _ssd_prompt(p, w["conv_w"], w["conv_b"], w["dtb_row"], w["a_row"], w["d_row"], w["ssd_g"],
                                n_seq=n_seq)
    x = _merge(x, w["g1"], (y_a, y_b, y_c, y_d), w["w_gate"], w["w_branch"], w["w_out"], tm=512)
    x, ffn_hist = _ffn(x, w["g2"], jnp.zeros((n_seq, 8, 2 * D_FF), F32), w["ffn_up"], w["ffn_cw"],
                       w["ffn_cb"], w["ffn_down"], n_seq=n_seq, s=1, tm=512, cwid=D_FF)
    p3 = p.reshape(n_seq, seq_len, P_COLS)
    unit = lambda u: p3[:, :, u * WIDTH:(u + 1) * WIDTH]
    state = dict(
        k=unit(U_K).reshape(n_seq, seq_len, FOX_HEADS, FOX_HEAD_DIM),
        v=unit(U_V).reshape(n_seq, seq_len, FOX_HEADS, FOX_HEAD_DIM),
        logf=lf.reshape(n_seq, seq_len, SMALL_W)[:, :, :FOX_HEADS],
        pool=unit(U_POOL)[:, -POOL_HIST:],
        ssd_conv=p3[:, -(SSD_CONV - 1):, U_XS * WIDTH:U_XS * WIDTH + SSD_CONV_DIM],
        ssd=ssd_last.reshape(n_seq, SSD_HEADS, SSD_HEAD_DIM, SSD_STATE),
        ffn_conv=ffn_hist[:, -(FFN_CONV - 1):],
    )
    return x, state


def _to_time_major(a):
    return jnp.swapaxes(a, 0, 1).reshape((a.shape[0] * a.shape[1],) + a.shape[2:])


def _from_time_major(a, n_b):
    return jnp.swapaxes(a.reshape((a.shape[0] // n_b, n_b) + a.shape[1:]), 0, 1)


def _sample_layer(x, w, consts, layer, past, page_table, n_b, past_len):
    m = x.shape[0]
    n_tok = m // n_b
    p = _proj(x, w["g1"], w["w_p"], tm=m)
    unit = lambda u: p[:, u * WIDTH:(u + 1) * WIDTH]
    small = p[:, 10 * WIDTH:]

    y_a, va = _mixa(p, w["ws_sample"], w["cmb_sample"], w["cm_g"], t=m, tm=m)

    pool_new = unit(U_POOL)
    pool_hist = jnp.concatenate([jnp.zeros(((POOL_PAD_STEPS - POOL_HIST) * n_b, WIDTH), F32),
                                 _to_time_major(past["pool"])], axis=0)
    y_b = _pool(pool_new, 0, pool_hist[None], w["pool_bd"], w["pool_scale"],
                n_seq=1, s=n_b, tm=m, pos0=past_len)
    new_pool = jnp.concatenate([past["pool"], _from_time_major(pool_new, n_b)], axis=1)[:, -POOL_HIST:]

    pad_tok = lambda a: jnp.pad(_from_time_major(a, n_b), ((0, 0), (0, TOK_PAD - n_tok), (0, 0)))
    f_raw = _from_time_major(small[:, :FOX_HEADS], n_b)
    fr_lane = jnp.pad(jnp.swapaxes(f_raw, 1, 2),
                      ((0, 0), (0, HEAD_PAD - FOX_HEADS), (0, PAGE_SIZE - n_tok)))
    fr_lane = jnp.tile(fr_lane, (1, n_tok, 1))
    y_c = _fox_decode(layer, page_table, _from_time_major(unit(U_Q), n_b), pad_tok(unit(U_K)),
                      pad_tok(unit(U_V)), fr_lane, w["bf_col"], past["cache_k"], past["cache_v"],
                      past["cache_lft"])
    y_c = _to_time_major(y_c)
    logf = _log_sigmoid_rows(small, w["bf_row"])[:, :FOX_HEADS]

    xbc_new = p[:, U_XS * WIDTH:U_XS * WIDTH + SSD_CONV_DIM]
    xbc_ext = jnp.concatenate([_to_time_major(past["ssd_conv"]), xbc_new], axis=0)
    act = _conv_silu(xbc_ext, w["conv_w"], w["conv_b"], s=n_b, rows=m)
    xs = act[:, :WIDTH]
    rep_heads = lambda a: jnp.repeat(_from_time_major(a, n_b).reshape(n_b, n_tok, SSD_GROUPS, SSD_STATE),
                                     SSD_HEADS // SSD_GROUPS, axis=2)
    to_bh = lambda a: jnp.transpose(a, (0, 2, 1, 3)).reshape(n_b * SSD_HEADS, n_tok, SSD_STATE)
    x_bh = jnp.transpose(_from_time_major(xs, n_b).reshape(n_b, n_tok, SSD_HEADS, SSD_HEAD_DIM),
                         (0, 2, 3, 1)).reshape(n_b * SSD_HEADS, SSD_HEAD_DIM, n_tok)
    dtr = _from_time_major(small[:, FOX_HEADS:FOX_HEADS + SSD_HEADS], n_b)
    dtr_bh = jnp.swapaxes(dtr, 1, 2).reshape(n_b * SSD_HEADS, 1, n_tok)
    h0 = past["ssd"].reshape(n_b * SSD_HEADS, SSD_HEAD_DIM, SSD_STATE)
    y_bh, h_last = _ssd_step(x_bh, to_bh(rep_heads(act[:, WIDTH:2 * WIDTH])),
                             to_bh(rep_heads(act[:, 2 * WIDTH:])), dtr_bh, w["dtb_bh"], w["a_bh"], h0, blk=32)
    y_s = _to_time_major(jnp.transpose(y_bh.reshape(n_b, SSD_HEADS, SSD_HEAD_DIM, n_tok),
                                       (0, 3, 1, 2)).reshape(n_b, n_tok, WIDTH))
    y_d = _ssd_gate(y_s, xs, unit(U_Z), w["d_row"], w["ssd_g"])

    x = _merge(x, w["g1"], (y_a, y_b, y_c, y_d), w["w_gate"], w["w_branch"], w["w_out"], tm=min(m, 256))
    ffn_h0 = _to_time_major(past["ffn_conv"])[None]
    x, ffn_hist = _ffn(x, w["g2"], ffn_h0, w["ffn_up"], w["ffn_cw"], w["ffn_cb"], w["ffn_down"],
                       n_seq=1, s=n_b, tm=m, cwid=D_FF // 2)
    state = dict(
        k=_from_time_major(unit(U_K), n_b).reshape(n_b, n_tok, FOX_HEADS, FOX_HEAD_DIM),
        v=_from_time_major(unit(U_V), n_b).reshape(n_b, n_tok, FOX_HEADS, FOX_HEAD_DIM),
        logf=_from_time_major(logf, n_b),
        chunk_v=_from_time_major(va, n_b),
        pool=new_pool,
        ssd_conv=_from_time_major(xbc_ext[-(SSD_CONV - 1) * n_b:], n_b),
        ssd=h_last.reshape(n_b, SSD_HEADS, SSD_HEAD_DIM, SSD_STATE),
        ffn_conv=_from_time_major(ffn_hist[0], n_b),
    )
    return x, state


def _log_sigmoid_body(x_ref, b_ref, o_ref):
    o_ref[...] = _log_sigmoid(x_ref[...] + b_ref[...])


def _log_sigmoid_rows(small, bf_row):
    return pl.pallas_call(
        _log_sigmoid_body,
        out_shape=jax.ShapeDtypeStruct(small.shape, F32),
        name="log_forget",
    )(small, bf_row)


def kernel(x_prompt, x_sample, cache_k, cache_v, cache_logf, page_table, state_pool, state_ssd_conv, state_ssd,
           state_ffn_conv, norm1_g, w_in, cm_norm_g, cm_ws, cm_b, pool_w, pool_scale, fox_bf, ssd_conv_w,
           ssd_conv_b, ssd_dt_bias, ssd_a_log, ssd_d, ssd_norm_g, w_branch, w_out, norm2_g, ffn_up, ffn_conv_w,
           ffn_conv_b, ffn_down, final_norm_g):
    n_seq, seq_len, _ = x_prompt.shape
    n_b, n_tok, _ = x_sample.shape
    depth = w_in.shape[0]
    n_pages = page_table.shape[1]
    past_len = n_pages * cache_k.shape[2]
    n_phys = cache_k.shape[1]

    consts = dict(sel=_selection_matrices())
    ck = jnp.transpose(cache_k, (0, 1, 3, 4, 2)).reshape(depth, n_phys, WIDTH, PAGE_SIZE)
    cv = jnp.transpose(cache_v, (0, 1, 3, 4, 2)).reshape(depth, n_phys, WIDTH, PAGE_SIZE)
    clt = jnp.pad(jnp.swapaxes(cache_logf, 2, 3), ((0, 0), (0, 0), (0, HEAD_PAD - FOX_HEADS), (0, 0)))

    xp = x_prompt.reshape(n_seq * seq_len, D_MODEL)
    xs = _to_time_major(x_sample)
    st_p, st_s = [], []
    for l in range(depth):
        w = _layer_params(l, norm1_g, w_in, cm_norm_g, cm_ws, cm_b, pool_w, pool_scale, fox_bf, ssd_conv_w,
                          ssd_conv_b, ssd_dt_bias, ssd_a_log, ssd_d, ssd_norm_g, w_branch, w_out, norm2_g,
                          ffn_up, ffn_conv_w, ffn_conv_b, ffn_down, n_b, n_tok)
        past = dict(cache_k=ck, cache_v=cv, cache_lft=clt, pool=state_pool[l], ssd_conv=state_ssd_conv[l],
                    ssd=state_ssd[l], ffn_conv=state_ffn_conv[l])
        xp, sp = _prompt_layer(xp, w, consts, n_seq)
        xs, ss = _sample_layer(xs, w, consts, l, past, page_table, n_b, past_len)
        st_p.append(sp)
        st_s.append(ss)
    gfin = final_norm_g[None, :]
    y_prompt = _final_norm(xp, gfin, tm=512).reshape(n_seq, seq_len, D_MODEL)
    y_sample = _from_time_major(_final_norm(xs, gfin, tm=xs.shape[0]), n_b)

    def stacked(states, name):
        return jnp.stack([s[name] for s in states], axis=0)

    return (y_prompt, y_sample,
            stacked(st_p, "k"), stacked(st_p, "v"), stacked(st_p, "logf"),
            stacked(st_s, "k"), stacked(st_s, "v"), stacked(st_s, "logf"),
            stacked(st_s, "chunk_v"),
            stacked(st_p, "pool"), stacked(st_s, "pool"),
            stacked(st_p, "ssd_conv"), stacked(st_s, "ssd_conv"),
            stacked(st_p, "ssd"), stacked(st_s, "ssd"),
            stacked(st_p, "ffn_conv"), stacked(st_s, "ffn_conv"))
```

```python
import functools
import math

import numpy as np
import jax
import jax.numpy as jnp
from jax import lax
from jax.experimental import pallas as pl
from jax.experimental.pallas import tpu as pltpu

F32 = jnp.float32
BF16 = jnp.bfloat16

D_MODEL = 1024
WIDTH = 256
PAGE_SIZE = 128
CM_GROUPS = 4
CM_CHUNK = 128
POOL_WINDOWS = (2, 4, 8, 16)
POOL_HIST = 15
POOL_PAD_STEPS = 32
FOX_HEADS = 4
FOX_HEAD_DIM = 64
SSD_HEADS = 4
SSD_HEAD_DIM = 64
SSD_GROUPS = 2
SSD_STATE = 128
SSD_CONV = 4
SSD_CHUNK = 128
SSD_CONV_DIM = 768
D_FF = 2816
FFN_CONV = 3
N_BRANCH = 4
EPS = 1e-6
SMALL_W = 128
P_COLS = 10 * WIDTH + SMALL_W
SMALL_BLK = 10 * WIDTH // SMALL_W
U_AU, U_AV, U_POOL, U_Q, U_K, U_V, U_Z, U_XS, U_BM, U_CM = range(10)

VMEM_LIMIT = 56 * 1024 * 1024


def _cparams(sem):
    return pltpu.CompilerParams(dimension_semantics=sem, vmem_limit_bytes=VMEM_LIMIT)


def _sigmoid(x):
    return 1.0 / (1.0 + jnp.exp(-x))


def _silu(x):
    return x * _sigmoid(x)


def _softplus(x):
    return jnp.maximum(x, 0.0) + jnp.log(1.0 + jnp.exp(-jnp.abs(x)))


def _log_sigmoid(x):
    return -_softplus(-x)


def _gelu_tanh(x):
    return 0.5 * x * (1.0 + jnp.tanh(math.sqrt(2.0 / math.pi) * (x + 0.044715 * (x * x * x))))


def _rms(x, g):
    return x * lax.rsqrt(jnp.mean(x * x, axis=-1, keepdims=True) + EPS) * g


def _split3(x):
    hi = x.astype(BF16)
    r1 = x - hi.astype(F32)
    mid = r1.astype(BF16)
    lo = (r1 - mid.astype(F32)).astype(BF16)
    return hi, mid, lo


def _dot(a, b):
    return jnp.dot(a, b, preferred_element_type=F32)


def _dot_nt(a, b):
    return lax.dot_general(a, b, (((1,), (1,)), ((), ())), preferred_element_type=F32)


def _ones_dot_left(ones_bf16, x):
    hi, mid, lo = _split3(x)
    return _dot(ones_bf16, hi) + _dot(ones_bf16, mid) + _dot(ones_bf16, lo)


def _ones_dot_right(x, ones_bf16):
    hi, mid, lo = _split3(x)
    return _dot(hi, ones_bf16) + _dot(mid, ones_bf16) + _dot(lo, ones_bf16)


def _iota(shape, dim):
    return lax.broadcasted_iota(jnp.int32, shape, dim)


def _proj_body(x_ref, g_ref, w_ref, o_ref, k_ref, v_ref):
    h = _rms(x_ref[...], g_ref[...]).astype(BF16)
    p = _dot(h, w_ref[...])
    o_ref[...] = p
    k_ref[...] = p[:, U_K * WIDTH:(U_K + 1) * WIDTH]
    v_ref[...] = p[:, U_V * WIDTH:(U_V + 1) * WIDTH]


def _proj(x, g, w, tm):
    m, d = x.shape
    n = w.shape[1]
    kv_spec = pl.BlockSpec((tm, WIDTH), lambda i: (i, 0))
    kv_shape = jax.ShapeDtypeStruct((m, WIDTH), F32)
    return pl.pallas_call(
        _proj_body,
        grid=(m // tm,),
        in_specs=[pl.BlockSpec((tm, d), lambda i: (i, 0)),
                  pl.BlockSpec((1, d), lambda i: (0, 0)),
                  pl.BlockSpec((d, n), lambda i: (0, 0))],
        out_specs=[pl.BlockSpec((tm, n), lambda i: (i, 0)), kv_spec, kv_spec],
        out_shape=[jax.ShapeDtypeStruct((m, n), F32), kv_shape, kv_shape],
        compiler_params=_cparams(("arbitrary",)),
        name="proj",
    )(x, g, w)


def _mixa_body(u_ref, v_ref, wm_ref, b_ref, g_ref, y_ref, *maybe_va_ref, t, n_chunks):
    lane_grp = _iota((t, WIDTH), 1) >> 6
    for c in range(n_chunks):
        rows = pl.ds(c * t, t)
        u = _gelu_tanh(u_ref[rows, :])
        va = _rms(_gelu_tanh(v_ref[rows, :]), g_ref[...])
        for va_ref in maybe_va_ref:
            va_ref[rows, :] = va
        vab = va.astype(BF16)
        zero = jnp.zeros_like(vab)
        s = b_ref[...]
        for gi in range(CM_GROUPS):
            s = s + _dot(wm_ref[gi], jnp.where(lane_grp == gi, vab, zero))
        y_ref[rows, :] = u * s


def _mixa(p, wm, bias, g, t, tm, with_va):
    m = p.shape[0]
    n_out = 2 if with_va else 1
    body = functools.partial(_mixa_body, t=t, n_chunks=tm // t)
    return pl.pallas_call(
        body,
        grid=(m // tm,),
        in_specs=[pl.BlockSpec((tm, WIDTH), lambda i: (i, U_AU)),
                  pl.BlockSpec((tm, WIDTH), lambda i: (i, U_AV)),
                  pl.BlockSpec((CM_GROUPS, t, t), lambda i: (0, 0, 0)),
                  pl.BlockSpec((t, WIDTH), lambda i: (0, 0)),
                  pl.BlockSpec((1, WIDTH), lambda i: (0, 0))],
        out_specs=[pl.BlockSpec((tm, WIDTH), lambda i: (i, 0))] * n_out,
        out_shape=[jax.ShapeDtypeStruct((m, WIDTH), F32)] * n_out,
        compiler_params=_cparams(("arbitrary",)),
        name="mixer_a",
    )(p, p, wm, bias, g)


def _pool_body(seq_ref, h0_ref, w_ref, scale_ref, y_ref, e1, e2, e4, e8, *, s, tm, pos0, steps_per_tile):
    j = pl.program_id(1)
    hr = POOL_PAD_STEPS * s
    tot = hr + tm

    @pl.when(j == 0)
    def _():
        e1[0:hr, :] = h0_ref[0]

    e1[hr:tot, :] = seq_ref[...]
    e2[8 * s:tot, :] = e1[8 * s:tot, :] + e1[7 * s:tot - s, :]
    e4[16 * s:tot, :] = e2[16 * s:tot, :] + e2[14 * s:tot - 2 * s, :]
    e8[24 * s:tot, :] = e4[24 * s:tot, :] + e4[20 * s:tot - 4 * s, :]
    cur = e1[hr:tot, :]
    s2 = e2[hr:tot, :]
    s4 = e4[hr:tot, :]
    s8 = e8[hr:tot, :]
    s16 = s8 + e8[hr - 8 * s:tot - 8 * s, :]
    lane_grp = _iota((tm, WIDTH), 1) >> 6
    pos = pos0 + j * steps_per_tile + (_iota((tm, WIDTH), 0) >> (s.bit_length() - 1))
    wsum = jnp.where(lane_grp == 0, s2, jnp.where(lane_grp == 1, s4, jnp.where(lane_grp == 2, s8, s16)))
    win = jnp.where(lane_grp == 0, 2, jnp.where(lane_grp == 1, 4, jnp.where(lane_grp == 2, 8, 16)))
    count = jnp.minimum(pos + 1, win).astype(F32)
    d = wsum / count - cur
    y_ref[...] = _dot(d.astype(BF16), w_ref[...]) * scale_ref[...]
    e1[0:hr, :] = e1[tm:tot, :]


def _pool(seq, col_blk, h0, w_bd, scale, *, n_seq, s, tm, pos0):
    m = seq.shape[0]
    tiles = m // n_seq // tm
    hr = POOL_PAD_STEPS * s
    body = functools.partial(_pool_body, s=s, tm=tm, pos0=pos0, steps_per_tile=tm // s)
    return pl.pallas_call(
        body,
        grid=(n_seq, tiles),
        in_specs=[pl.BlockSpec((tm, WIDTH), lambda n, j: (n * tiles + j, col_blk)),
                  pl.BlockSpec((1, hr, WIDTH), lambda n, j: (n, 0, 0)),
                  pl.BlockSpec((WIDTH, WIDTH), lambda n, j: (0, 0)),
                  pl.BlockSpec((1, WIDTH), lambda n, j: (0, 0))],
        out_specs=pl.BlockSpec((tm, WIDTH), lambda n, j: (n * tiles + j, 0)),
        out_shape=jax.ShapeDtypeStruct((m, WIDTH), F32),
        scratch_shapes=[pltpu.VMEM((hr + tm, WIDTH), F32) for _ in range(4)],
        compiler_params=_cparams(("arbitrary", "arbitrary")),
        name="mixer_b",
    )(seq, h0, w_bd, scale)


LOG2E = math.log2(math.e)


V_ROWS = 2 * FOX_HEAD_DIM


def _fox_prep_body(q_ref, k_ref, v_ref, sm_ref, bf_ref, selq_ref, selk_ref, selv_ref,
                   lf_ref, qa_ref, ka_ref, va_ref, carry, *, tm):
    j = pl.program_id(1)

    @pl.when(j == 0)
    def _():
        carry[...] = jnp.zeros_like(carry)

    logf = _log_sigmoid(sm_ref[...] + bf_ref[...])
    lf_ref[...] = logf
    tril = (_iota((tm, tm), 0) >= _iota((tm, tm), 1)).astype(BF16)
    c = _ones_dot_left(tril, logf) + carry[...]
    carry[...] = c[tm - 1:tm, :]
    c = c * LOG2E
    c_t = jnp.transpose(c)
    qb = (q_ref[...] * (FOX_HEAD_DIM ** -0.5 * LOG2E)).astype(BF16)
    kb = k_ref[...].astype(BF16)
    vb = v_ref[...].astype(BF16)
    d = FOX_HEAD_DIM
    lane = _iota((tm, 2 * d), 1)
    row = _iota((2 * d, tm), 0)

    def pieces(x):
        hi = x.astype(BF16).astype(F32)
        r1 = x - hi
        mid = r1.astype(BF16).astype(F32)
        return hi, mid, (r1 - mid).astype(BF16).astype(F32)

    for h in range(FOX_HEADS):
        hi, mid, lo = pieces(jnp.broadcast_to(c_t[h:h + 1, :], (2 * d, tm)))
        q_extra = jnp.where(row == d, hi, jnp.where(row == d + 1, mid, jnp.where(row == d + 2, lo,
                  jnp.where((row >= d + 3) & (row < d + 6), 1.0, 0.0))))
        qa_ref[0, h] = (_dot_nt(selq_ref[h], qb) + q_extra).astype(BF16)
        hi, mid, lo = pieces(jnp.broadcast_to(c[:, h:h + 1], (tm, 2 * d)))
        k_extra = jnp.where(lane == d + 3, -hi, jnp.where(lane == d + 4, -mid, jnp.where(lane == d + 5, -lo,
                  jnp.where((lane >= d) & (lane < d + 3), 1.0, 0.0))))
        ka_ref[0, h] = (_dot(kb, selk_ref[h]) + k_extra).astype(BF16)
        ones_row = jnp.where(row[:V_ROWS] == d, 1.0, 0.0)
        va_ref[0, h] = (_dot_nt(selv_ref[h], vb) + ones_row).astype(BF16)


def _fox_prep(p, bf_row, selq, selk, selv, *, n_seq, tm):
    m = p.shape[0]
    seq_len = m // n_seq
    tiles = seq_len // tm
    aug = jax.ShapeDtypeStruct((n_seq, FOX_HEADS, seq_len, 2 * FOX_HEAD_DIM), BF16)
    aug_spec = pl.BlockSpec((1, FOX_HEADS, tm, 2 * FOX_HEAD_DIM), lambda n, j: (n, 0, j, 0))
    def transposed(rows):
        return (jax.ShapeDtypeStruct((n_seq, FOX_HEADS, rows, seq_len), BF16),
                pl.BlockSpec((1, FOX_HEADS, rows, tm), lambda n, j: (n, 0, 0, j)),
                pl.BlockSpec((FOX_HEADS, rows, WIDTH), lambda n, j: (0, 0, 0)))

    aug_t, aug_t_spec, selq_spec = transposed(2 * FOX_HEAD_DIM)
    v_t, v_t_spec, selv_spec = transposed(V_ROWS)
    sel_spec = pl.BlockSpec((FOX_HEADS, WIDTH, 2 * FOX_HEAD_DIM), lambda n, j: (0, 0, 0))
    return pl.pallas_call(
        functools.partial(_fox_prep_body, tm=tm),
        grid=(n_seq, tiles),
        in_specs=[pl.BlockSpec((tm, WIDTH), lambda n, j: (n * tiles + j, U_Q)),
                  pl.BlockSpec((tm, WIDTH), lambda n, j: (n * tiles + j, U_K)),
                  pl.BlockSpec((tm, WIDTH), lambda n, j: (n * tiles + j, U_V)),
                  pl.BlockSpec((tm, SMALL_W), lambda n, j: (n * tiles + j, SMALL_BLK)),
                  pl.BlockSpec((1, SMALL_W), lambda n, j: (0, 0)),
                  selq_spec, sel_spec, selv_spec],
        out_specs=[pl.BlockSpec((tm, SMALL_W), lambda n, j: (n * tiles + j, 0)),
                   aug_t_spec, aug_spec, v_t_spec],
        out_shape=[jax.ShapeDtypeStruct((m, SMALL_W), F32), aug_t, aug, v_t],
        scratch_shapes=[pltpu.VMEM((1, SMALL_W), F32)],
        compiler_params=_cparams(("arbitrary", "arbitrary")),
        name="fox_prep",
    )(p, p, p, p, bf_row, selq, selk, selv)


def _fox_flash_body(qi_ref, kj_ref, q_ref, k_ref, v_ref, o_ref, m_ref, acc_ref, *, tq, wq):
    step = pl.program_id(1)
    qi = qi_ref[step]
    kj = kj_ref[step]
    d = FOX_HEAD_DIM

    @pl.when(kj == 0)
    def _():
        m_ref[...] = jnp.full_like(m_ref, -jnp.inf)
        acc_ref[...] = jnp.zeros_like(acc_ref)

    n_split = tq // wq
    chains = [(h, c) for h in range(FOX_HEADS) for c in range(n_split)]

    def update(diagonal):
        n_keys = [(c + 1) * wq if diagonal else tq for _, c in chains]
        scores = [_dot(k_ref[0, h, 0:nk, :], q_ref[0, h, :, c * wq:(c + 1) * wq])
                  for (h, c), nk in zip(chains, n_keys)]
        for i, ((h, c), nk, s) in enumerate(zip(chains, n_keys, scores)):
            if diagonal:
                s = jnp.where(_iota((nk, wq), 0) <= _iota((nk, wq), 1) + c * wq, s, -jnp.inf)
            m_prev = m_ref[i]
            m_new = jnp.maximum(m_prev, jnp.max(s, axis=0, keepdims=True))
            alpha = jnp.exp2(m_prev - m_new)
            pexp = jnp.exp2(s - m_new).astype(BF16)
            acc_ref[i] = alpha * acc_ref[i] + _dot(v_ref[0, h, :, 0:nk], pexp)
            m_ref[i] = m_new

    @pl.when(kj < qi)
    def _():
        update(False)

    @pl.when(kj == qi)
    def _():
        update(True)
        o = []
        for h in range(FOX_HEADS):
            parts = []
            for c in range(n_split):
                acc = acc_ref[h * n_split + c]
                parts.append(acc[0:d, :] / acc[d:d + 1, :])
            o.append(jnp.concatenate(parts, axis=1))
        o_ref[...] = jnp.concatenate([jnp.transpose(jnp.concatenate(o[0:2], axis=0)),
                                      jnp.transpose(jnp.concatenate(o[2:4], axis=0))], axis=1)


def _fox_flash(qa_t, ka, va_t, *, tq, wq):
    n_seq, _, seq_len, w = ka.shape
    nq = seq_len // tq
    qi = np.array([i for i in range(nq) for _ in range(i + 1)], np.int32)
    kj = np.array([j for i in range(nq) for j in range(i + 1)], np.int32)
    q_spec = pl.BlockSpec((1, FOX_HEADS, w, tq), lambda n, s, qi, kj: (n, 0, 0, qi[s]))
    k_spec = pl.BlockSpec((1, FOX_HEADS, tq, w), lambda n, s, qi, kj: (n, 0, kj[s], 0))
    v_spec = pl.BlockSpec((1, FOX_HEADS, V_ROWS, tq), lambda n, s, qi, kj: (n, 0, 0, kj[s]))
    grid_spec = pltpu.PrefetchScalarGridSpec(
        num_scalar_prefetch=2,
        grid=(n_seq, len(qi)),
        in_specs=[q_spec, k_spec, v_spec],
        out_specs=pl.BlockSpec((tq, WIDTH), lambda n, s, qi, kj: (n * nq + qi[s], 0)),
        scratch_shapes=[pltpu.VMEM((FOX_HEADS * (tq // wq), 1, wq), F32),
                        pltpu.VMEM((FOX_HEADS * (tq // wq), V_ROWS, wq), F32)],
    )
    return pl.pallas_call(
        functools.partial(_fox_flash_body, tq=tq, wq=wq),
        grid_spec=grid_spec,
        out_shape=jax.ShapeDtypeStruct((n_seq * seq_len, WIDTH), F32),
        compiler_params=_cparams(("arbitrary", "arbitrary")),
        name="fox_flash",
    )(jnp.asarray(qi), jnp.asarray(kj), qa_t, ka, va_t)


TOK_PAD = 8
HEAD_PAD = 8


DECODE_SEQS = 2
DECODE_SLOTS = 2 * DECODE_SEQS


def _fox_decode_body(pt_ref, q_ref, kn_ref, vn_ref, fr_ref, bfc_ref, ck_hbm, cv_hbm, cl_hbm, o_ref,
                     kbuf, vbuf, lbuf, sems, *, layer, n_seq, n_pages, n_tok):
    b = pl.program_id(0)

    def page_copies(page, slot, pg):
        return [pltpu.make_async_copy(src.at[layer, page], dst.at[slot, pg], sems.at[slot])
                for src, dst in ((ck_hbm, kbuf), (cv_hbm, vbuf), (cl_hbm, lbuf))]

    def fetch(seq):
        @pl.when(seq < n_seq)
        def _():
            for pg in range(n_pages):
                for cp in page_copies(pt_ref[seq, pg], seq % DECODE_SLOTS, pg):
                    cp.start()

    @pl.when(b == 0)
    def _():
        for seq in range(DECODE_SLOTS):
            fetch(seq)

    seqs = [b * DECODE_SEQS + u for u in range(DECODE_SEQS)]
    for seq in seqs:
        for pg in range(n_pages):
            for cp in page_copies(0, seq % DECODE_SLOTS, pg):
                cp.wait()
    for u, seq in enumerate(seqs):
        slot = seq % DECODE_SLOTS
        o_ref[u] = _fox_decode_one(q_ref[u], kn_ref[u], vn_ref[u], fr_ref[u], bfc_ref[...],
                                   [kbuf.at[slot, pg] for pg in range(n_pages)],
                                   [vbuf.at[slot, pg] for pg in range(n_pages)],
                                   [lbuf.at[slot, pg] for pg in range(n_pages)], n_tok=n_tok)
    for seq in seqs:
        fetch(seq + DECODE_SLOTS)


def _fox_decode_one(q, k_new8, v_new8, fr, bfc, k_refs, v_refs, lf_refs, *, n_tok):
    n_pages = len(k_refs)
    d = FOX_HEAD_DIM
    ps = PAGE_SIZE
    qrows = n_tok * HEAD_PAD

    q4 = q * (d ** -0.5)
    own = (_iota((HEAD_PAD, WIDTH), 1) >> 6) == _iota((HEAD_PAD, WIDTH), 0)
    qbd = jnp.concatenate([jnp.where(own, jnp.broadcast_to(q4[t:t + 1, :], (HEAD_PAD, WIDTH)), 0.0)
                           for t in range(n_tok)], axis=0).astype(BF16)

    triu = (_iota((ps, ps), 0) <= _iota((ps, ps), 1)).astype(BF16)
    ones = jnp.ones((ps, ps), BF16)
    lf_new = _log_sigmoid(fr + bfc)
    newcum_lane = _ones_dot_right(lf_new, triu)
    row_t = _iota((qrows, ps), 0) >> 3
    lane_i = _iota((qrows, ps), 1)
    newcum_col = jnp.sum(jnp.where(lane_i == row_t, newcum_lane, 0.0), axis=1, keepdims=True)

    lst = jnp.concatenate([r[...] for r in lf_refs], axis=0)
    incl = _ones_dot_right(lst, triu)
    tot = _ones_dot_right(lst, ones)
    later = jnp.zeros((HEAD_PAD, ps), F32)
    bias = [None] * n_pages
    for pg in reversed(range(n_pages)):
        rows = slice(pg * HEAD_PAD, (pg + 1) * HEAD_PAD)
        g = (tot[rows] - incl[rows]) + later
        bias[pg] = jnp.concatenate([g] * n_tok, axis=0) + newcum_col
        later = later + tot[rows]

    k_new = jnp.concatenate([k_new8, jnp.zeros((ps - TOK_PAD, WIDTH), F32)], axis=0).astype(BF16)
    v_new = jnp.concatenate([v_new8, jnp.zeros((ps - TOK_PAD, WIDTH), F32)], axis=0).astype(BF16)
    s_new = _dot_nt(qbd, k_new) + newcum_col - newcum_lane
    s_new = jnp.where(lane_i <= row_t, s_new, -jnp.inf)
    s_past = [_dot(qbd, k_refs[pg][...].astype(BF16)) + bias[pg] for pg in range(n_pages)]

    m_lane = s_new
    for sp in s_past:
        m_lane = jnp.maximum(m_lane, sp)
    m = jnp.max(m_lane, axis=1, keepdims=True)
    p_new = jnp.exp(s_new - m)
    l_lane = p_new
    acc = _dot(p_new.astype(BF16), v_new)
    for pg in range(n_pages):
        pp = jnp.exp(s_past[pg] - m)
        l_lane = l_lane + pp
        acc = acc + _dot_nt(pp.astype(BF16), v_refs[pg][...].astype(BF16))
    o = acc / jnp.sum(l_lane, axis=1, keepdims=True)
    keep = jnp.concatenate([own] * n_tok, axis=0)
    return jnp.sum(jnp.where(keep, o, 0.0).reshape(n_tok, HEAD_PAD, WIDTH), axis=1)


def _fox_decode(layer, page_table, q, kn8, vn8, fr_lane, bf_col, cache_k, cache_v, cache_lft):
    n_b, n_pages = page_table.shape
    n_tok = q.shape[1]
    qrows = n_tok * HEAD_PAD
    ps = PAGE_SIZE

    assert n_b % DECODE_SEQS == 0
    per_seq = lambda rows, cols: pl.BlockSpec((DECODE_SEQS, rows, cols), lambda b, pt: (b, 0, 0))
    hbm = pl.BlockSpec(memory_space=pl.ANY)
    grid_spec = pltpu.PrefetchScalarGridSpec(
        num_scalar_prefetch=1,
        grid=(n_b // DECODE_SEQS,),
        in_specs=[per_seq(n_tok, WIDTH), per_seq(TOK_PAD, WIDTH), per_seq(TOK_PAD, WIDTH),
                  per_seq(qrows, ps),
                  pl.BlockSpec((qrows, ps), lambda b, pt: (0, 0)),
                  hbm, hbm, hbm],
        out_specs=per_seq(n_tok, WIDTH),
        scratch_shapes=[pltpu.VMEM((DECODE_SLOTS, n_pages, WIDTH, ps), F32),
                        pltpu.VMEM((DECODE_SLOTS, n_pages, WIDTH, ps), F32),
                        pltpu.VMEM((DECODE_SLOTS, n_pages, HEAD_PAD, ps), F32),
                        pltpu.SemaphoreType.DMA((DECODE_SLOTS,))],
    )
    return pl.pallas_call(
        functools.partial(_fox_decode_body, layer=layer, n_seq=n_b, n_pages=n_pages, n_tok=n_tok),
        grid_spec=grid_spec,
        out_shape=jax.ShapeDtypeStruct((n_b, n_tok, WIDTH), F32),
        compiler_params=_cparams(("arbitrary",)),
        name="fox_decode",
    )(page_table, q, kn8, vn8, fr_lane, bf_col, cache_k, cache_v, cache_lft)


def _ssd_prompt_body(xs_ref, bm_ref, cm_ref, z_ref, sm_ref, cw_ref, cb_ref, dtb_ref, arow_ref, drow_ref, g_ref,
                     y_ref, hl_ref, ext, state, *, t, n_seq):
    j = pl.program_id(0)

    @pl.when(j == 0)
    def _():
        ext[:, 0:8, :] = jnp.zeros((n_seq, 8, SSD_CONV_DIM), F32)
        state[...] = jnp.zeros_like(state)

    for s in range(n_seq):
        _ssd_chunk(xs_ref.at[s], bm_ref.at[s], cm_ref.at[s], z_ref.at[s], sm_ref.at[s], cw_ref, cb_ref, dtb_ref,
                   arow_ref, drow_ref, g_ref, y_ref.at[s], ext.at[s], state.at[s], t=t)

    @pl.when(j == pl.num_programs(0) - 1)
    def _():
        hl_ref[...] = state[...]


def _ssd_chunk(xs_ref, bm_ref, cm_ref, z_ref, sm_ref, cw_ref, cb_ref, dtb_ref, arow_ref, drow_ref, g_ref,
               y_ref, ext, state, *, t):
    hd = SSD_HEAD_DIM
    w3 = SSD_CONV_DIM

    ext[8:8 + t, 0:WIDTH] = xs_ref[...]
    ext[8:8 + t, WIDTH:2 * WIDTH] = bm_ref[...]
    ext[8:8 + t, 2 * WIDTH:w3] = cm_ref[...]
    conv = cb_ref[...]
    for k in range(SSD_CONV):
        conv = conv + cw_ref[k:k + 1, :] * ext[5 + k:5 + k + t, :]
    ext[0:8, :] = ext[t:t + 8, :]
    act = _silu(conv)
    xs = act[:, 0:WIDTH]
    bm = act[:, WIDTH:2 * WIDTH].astype(BF16)
    cm = act[:, 2 * WIDTH:w3].astype(BF16)

    dt = _softplus(sm_ref[...] + dtb_ref[...])
    da = dt * arow_ref[...]
    tril = (_iota((t, t), 0) >= _iota((t, t), 1))
    acs = _ones_dot_left(tril.astype(BF16), da)
    acs_t = jnp.transpose(acs)
    lane_head = _iota((t, WIDTH), 1) >> 6
    dt_x = jnp.zeros((t, WIDTH), F32)
    end_x = jnp.zeros((t, WIDTH), F32)
    in_x = jnp.zeros((t, WIDTH), F32)
    dec_rows = []
    for h in range(SSD_HEADS):
        col = acs[:, 4 + h:5 + h]
        lastv = acs[t - 1:t, 4 + h:5 + h]
        dt_x = jnp.where(lane_head == h, dt[:, 4 + h:5 + h], dt_x)
        end_x = jnp.where(lane_head == h, jnp.exp(lastv - col), end_x)
        in_x = jnp.where(lane_head == h, jnp.exp(col), in_x)
        dec_rows.append(jnp.broadcast_to(jnp.exp(lastv), (hd, SSD_STATE)))
    xdt = xs * dt_x
    xdt_b = xdt.astype(BF16)

    y = jnp.zeros((t, WIDTH), F32)
    h_in = state[...]
    xw_t = jnp.transpose(xdt * end_x).astype(BF16)
    new_states = []
    for gi in range(SSD_GROUPS):
        bg = bm[:, gi * SSD_STATE:(gi + 1) * SSD_STATE]
        cg = cm[:, gi * SSD_STATE:(gi + 1) * SSD_STATE]
        cbm = _dot_nt(cg, bg)
        rows = slice(gi * 2 * hd, (gi + 1) * 2 * hd)
        for r in range(SSD_HEADS // SSD_GROUPS):
            h = gi * (SSD_HEADS // SSD_GROUPS) + r
            seg = acs[:, 4 + h:5 + h] - acs_t[4 + h:5 + h, :]
            decay = jnp.exp(jnp.where(tril, seg, -jnp.inf))
            yd = _dot((cbm * decay).astype(BF16), xdt_b)
            y = y + jnp.where(lane_head == h, yd, 0.0)
        y_off = _dot_nt(cg, h_in[rows, :].astype(BF16))
        y_off = jnp.concatenate([y_off, y_off], axis=1)
        grp_lane = _iota((t, WIDTH), 1) >> 7
        y = y + jnp.where(grp_lane == gi, y_off * in_x, 0.0)
        new_states.append(_dot(xw_t[rows, :], bg))
    state[...] = h_in * jnp.concatenate(dec_rows, axis=0) + jnp.concatenate(new_states, axis=0)

    yg = (y + drow_ref[...] * xs) * _silu(z_ref[...])
    y_ref[...] = _rms(yg, g_ref[...])


def _ssd_prompt(p, cw, cb, dtb_row, a_row, d_row, g, *, n_seq):
    m = p.shape[0]
    t = SSD_CHUNK
    seq_len = m // n_seq
    hp = SSD_HEADS * SSD_HEAD_DIM
    p3 = p.reshape(n_seq, seq_len, P_COLS)

    def unit(u):
        return pl.BlockSpec((n_seq, t, WIDTH), lambda j: (0, j, u))

    def const(shape):
        return pl.BlockSpec(shape, lambda j: (0,) * len(shape))

    y, h_last = pl.pallas_call(
        functools.partial(_ssd_prompt_body, t=t, n_seq=n_seq),
        grid=(seq_len // t,),
        in_specs=[unit(U_XS), unit(U_BM), unit(U_CM), unit(U_Z),
                  pl.BlockSpec((n_seq, t, SMALL_W), lambda j: (0, j, SMALL_BLK)),
                  const((SSD_CONV, SSD_CONV_DIM)), const((1, SSD_CONV_DIM)),
                  const((1, SMALL_W)), const((1, SMALL_W)), const((1, WIDTH)), const((1, WIDTH))],
        out_specs=[pl.BlockSpec((n_seq, t, WIDTH), lambda j: (0, j, 0)),
                   const((n_seq, hp, SSD_STATE))],
        out_shape=[jax.ShapeDtypeStruct((n_seq, seq_len, WIDTH), F32),
                   jax.ShapeDtypeStruct((n_seq, hp, SSD_STATE), F32)],
        scratch_shapes=[pltpu.VMEM((n_seq, 8 + t, SSD_CONV_DIM), F32),
                        pltpu.VMEM((n_seq, hp, SSD_STATE), F32)],
        compiler_params=_cparams(("arbitrary",)),
        name="ssd_prompt",
    )(p3, p3, p3, p3, p3, cw, cb, dtb_row, a_row, d_row, g)
    return y.reshape(m, WIDTH), h_last


def _conv_silu_body(ext_ref, w_ref, b_ref, o_ref, *, s, taps, rows):
    acc = b_ref[...]
    for k in range(taps):
        acc = acc + w_ref[k:k + 1, :] * ext_ref[k * s:k * s + rows, :]
    o_ref[...] = _silu(acc)


def _conv_silu(ext, w, b, *, s, rows):
    taps = w.shape[0]
    c = ext.shape[1]
    return pl.pallas_call(
        functools.partial(_conv_silu_body, s=s, taps=taps, rows=rows),
        out_shape=jax.ShapeDtypeStruct((rows, c), F32),
        compiler_params=pltpu.CompilerParams(vmem_limit_bytes=VMEM_LIMIT),
        name="conv_silu",
    )(ext, w, b)


def _ssd_step_body(x_ref, b_ref, c_ref, dtr_ref, dtb_ref, a_ref, h0_ref, y_ref, hl_ref, *, n_tok):
    dt = _softplus(dtr_ref[...] + dtb_ref[...])
    da = dt * a_ref[...]
    h = h0_ref[...]
    lane_t = _iota(y_ref.shape, 2)
    y = jnp.zeros(y_ref.shape, F32)
    for t in range(n_tok):
        xdt = x_ref[:, :, t:t + 1] * dt[:, :, t:t + 1]
        h = h * jnp.exp(da[:, :, t:t + 1]) + xdt * b_ref[:, t:t + 1, :]
        yt = jnp.sum(h * c_ref[:, t:t + 1, :], axis=2, keepdims=True)
        y = jnp.where(lane_t == t, yt, y)
    y_ref[...] = y
    hl_ref[...] = h


def _ssd_step(x_bh, b_bh, c_bh, dtr_bh, dtb_bh, a_bh, h0, *, blk):
    bh, hd, n_tok = x_bh.shape
    ns = h0.shape[2]

    def spec(shape):
        return pl.BlockSpec((blk,) + shape, lambda i: (i, 0, 0))

    return pl.pallas_call(
        functools.partial(_ssd_step_body, n_tok=n_tok),
        grid=(bh // blk,),
        in_specs=[spec((hd, n_tok)), spec((n_tok, ns)), spec((n_tok, ns)), spec((1, n_tok)),
                  spec((1, 1)), spec((1, 1)), spec((hd, ns))],
        out_specs=[spec((hd, n_tok)), spec((hd, ns))],
        out_shape=[jax.ShapeDtypeStruct((bh, hd, n_tok), F32),
                   jax.ShapeDtypeStruct((bh, hd, ns), F32)],
        compiler_params=_cparams(("arbitrary",)),
        name="ssd_step",
    )(x_bh, b_bh, c_bh, dtr_bh, dtb_bh, a_bh, h0)


def _ssd_gate_body(y_ref, xs_ref, z_ref, d_ref, g_ref, o_ref):
    yg = (y_ref[...] + d_ref[...] * xs_ref[...]) * _silu(z_ref[...])
    o_ref[...] = _rms(yg, g_ref[...])


def _ssd_gate(y, xs, z, d_row, g):
    return pl.pallas_call(
        _ssd_gate_body,
        out_shape=jax.ShapeDtypeStruct(y.shape, F32),
        name="ssd_gate",
    )(y, xs, z, d_row, g)


def _merge_body(x_ref, g_ref, ya_ref, yb_ref, yc_ref, yd_ref, wg_ref, wb_ref, wo_ref, o_ref):
    x = x_ref[...]
    h = _rms(x, g_ref[...]).astype(BF16)
    merged = jnp.zeros(x.shape, F32)
    for i, y_ref in enumerate((ya_ref, yb_ref, yc_ref, yd_ref)):
        gate = _sigmoid(_dot(h, wg_ref[:, i * D_MODEL:(i + 1) * D_MODEL]))
        merged = merged + gate * _dot(y_ref[...].astype(BF16), wb_ref[i])
    o_ref[...] = x + _dot(merged.astype(BF16), wo_ref[...])


def _merge(x, g, ys, wg, wb, wo, tm):
    m = x.shape[0]
    row = lambda i: (i, 0)
    y_spec = pl.BlockSpec((tm, WIDTH), row)
    return pl.pallas_call(
        _merge_body,
        grid=(m // tm,),
        in_specs=[pl.BlockSpec((tm, D_MODEL), row),
                  pl.BlockSpec((1, D_MODEL), lambda i: (0, 0)),
                  y_spec, y_spec, y_spec, y_spec,
                  pl.BlockSpec((D_MODEL, N_BRANCH * D_MODEL), lambda i: (0, 0)),
                  pl.BlockSpec((N_BRANCH, WIDTH, D_MODEL), lambda i: (0, 0, 0)),
                  pl.BlockSpec((D_MODEL, D_MODEL), lambda i: (0, 0))],
        out_specs=pl.BlockSpec((tm, D_MODEL), row),
        out_shape=jax.ShapeDtypeStruct((m, D_MODEL), F32),
        compiler_params=_cparams(("arbitrary",)),
        name="merge",
    )(x, g, *ys, wg, wb, wo)


def _ffn_body(x_ref, g_ref, h0g_ref, h0u_ref, wug_ref, wuu_ref, cwg_ref, cwu_ref, cbg_ref, cbu_ref, wd_ref,
              o_ref, hg_ref, hu_ref, h2, acc, wkg, wku, histg, histu, *, s, tm, hr):
    j = pl.program_id(1)
    c = pl.program_id(2)
    last_c = pl.num_programs(2) - 1

    @pl.when(c == 0)
    def _():
        h2[...] = _rms(x_ref[...], g_ref[...]).astype(BF16)
        acc[...] = jnp.zeros_like(acc)

    @pl.when(j == 0)
    def _():
        histg[c] = h0g_ref[0]
        histu[c] = h0u_ref[0]

    def conv(wk, hist, wu_ref, cw_ref, cb_ref):
        wk[0:hr, :] = hist[c]
        wk[hr:hr + tm, :] = _dot(h2[...], wu_ref[...])
        out = cb_ref[...]
        for k in range(FFN_CONV):
            off = hr - (FFN_CONV - 1 - k) * s
            out = out + cw_ref[k:k + 1, :] * wk[off:off + tm, :]
        hist[c] = wk[tm:tm + hr, :]
        return out

    yg = conv(wkg, histg, wug_ref, cwg_ref, cbg_ref)
    yu = conv(wku, histu, wuu_ref, cwu_ref, cbu_ref)
    acc[...] += _dot((_silu(yg) * yu).astype(BF16), wd_ref[...])
    hg_ref[0, c] = histg[c]
    hu_ref[0, c] = histu[c]

    @pl.when(c == last_c)
    def _():
        o_ref[...] = x_ref[...] + acc[...]


def _ffn(x, g, h0, wu, cw, cb, wd, *, n_seq, s, tm, cwid):
    m = x.shape[0]
    tiles = m // n_seq // tm
    hr = h0.shape[1]
    nc = D_FF // cwid
    row = lambda n, j, c: (n * tiles + j, 0)
    hist_g = lambda n, j, c: (n, 0, c)
    hist_u = lambda n, j, c: (n, 0, nc + c)
    col_g = lambda n, j, c: (0, c)
    col_u = lambda n, j, c: (0, nc + c)
    hist_shape = jax.ShapeDtypeStruct((n_seq, nc, hr, cwid), F32)
    hist_out = pl.BlockSpec((1, nc, hr, cwid), lambda n, j, c: (n, 0, 0, 0))
    once = dict(pipeline_mode=pl.Buffered(1)) if nc == 1 else {}
    x_new, hg, hu = pl.pallas_call(
        functools.partial(_ffn_body, s=s, tm=tm, hr=hr),
        grid=(n_seq, tiles, nc),
        in_specs=[pl.BlockSpec((tm, D_MODEL), row),
                  pl.BlockSpec((1, D_MODEL), lambda n, j, c: (0, 0)),
                  pl.BlockSpec((1, hr, cwid), hist_g), pl.BlockSpec((1, hr, cwid), hist_u),
                  pl.BlockSpec((D_MODEL, cwid), col_g, **once), pl.BlockSpec((D_MODEL, cwid), col_u, **once),
                  pl.BlockSpec((FFN_CONV, cwid), col_g), pl.BlockSpec((FFN_CONV, cwid), col_u),
                  pl.BlockSpec((1, cwid), col_g), pl.BlockSpec((1, cwid), col_u),
                  pl.BlockSpec((cwid, D_MODEL), lambda n, j, c: (c, 0), **once)],
        out_specs=[pl.BlockSpec((tm, D_MODEL), row), hist_out, hist_out],
        out_shape=[jax.ShapeDtypeStruct((m, D_MODEL), F32), hist_shape, hist_shape],
        scratch_shapes=[pltpu.VMEM((tm, D_MODEL), BF16),
                        pltpu.VMEM((tm, D_MODEL), F32),
                        pltpu.VMEM((hr + tm, cwid), F32),
                        pltpu.VMEM((hr + tm, cwid), F32),
                        pltpu.VMEM((nc, hr, cwid), F32),
                        pltpu.VMEM((nc, hr, cwid), F32)],
        compiler_params=_cparams(("arbitrary", "arbitrary", "arbitrary")),
        name="ffn",
    )(x, g, h0, h0, wu, wu, cw, cw, cb, cb, wd)
    to_rows = lambda a: jnp.swapaxes(a, 1, 2).reshape(n_seq, hr, D_FF)
    return x_new, jnp.concatenate([to_rows(hg), to_rows(hu)], axis=2)


def _final_norm_body(x_ref, g_ref, o_ref):
    o_ref[...] = _rms(x_ref[...], g_ref[...])


def _final_norm(x, g, tm):
    m, d = x.shape
    return pl.pallas_call(
        _final_norm_body,
        grid=(m // tm,),
        in_specs=[pl.BlockSpec((tm, d), lambda i: (i, 0)), pl.BlockSpec((1, d), lambda i: (0, 0))],
        out_specs=pl.BlockSpec((tm, d), lambda i: (i, 0)),
        out_shape=jax.ShapeDtypeStruct((m, d), F32),
        compiler_params=_cparams(("arbitrary",)),
        name="final_norm",
    )(x, g)


def _selection_matrices():
    d = FOX_HEAD_DIM
    selq = np.zeros((FOX_HEADS, 2 * d, WIDTH), np.float32)
    selk = np.zeros((FOX_HEADS, WIDTH, 2 * d), np.float32)
    selv = np.zeros((FOX_HEADS, V_ROWS, WIDTH), np.float32)
    for h in range(FOX_HEADS):
        for i in range(d):
            selq[h, i, h * d + i] = 1.0
            selk[h, h * d + i, i] = 1.0
            selv[h, i, h * d + i] = 1.0
    return (jnp.asarray(selq, BF16), jnp.asarray(selk, BF16), jnp.asarray(selv, BF16))


def _layer_params(l, norm1_g, w_in, cm_norm_g, cm_ws, cm_b, pool_w, pool_scale, fox_bf, ssd_conv_w, ssd_conv_b,
                  ssd_dt_bias, ssd_a_log, ssd_d, ssd_norm_g, w_branch, w_out, norm2_g, ffn_up, ffn_conv_w,
                  ffn_conv_b, ffn_down, n_dec, n_tok):
    w = w_in[l]
    o_f = 2 * WIDTH + WIDTH + 3 * WIDTH
    o_z = o_f + FOX_HEADS
    o_xbc = o_z + WIDTH
    o_dt = o_xbc + SSD_CONV_DIM
    o_gate = o_dt + SSD_HEADS
    w_p = jnp.concatenate([w[:, :o_f], w[:, o_z:o_dt], w[:, o_f:o_z], w[:, o_dt:o_gate],
                           jnp.zeros((D_MODEL, SMALL_W - FOX_HEADS - SSD_HEADS), F32)], axis=1).astype(BF16)
    causal = jnp.tril(jnp.ones((CM_CHUNK, CM_CHUNK), F32))
    ws = cm_ws[l] * causal
    gd = WIDTH // CM_GROUPS
    eye = jnp.eye(n_dec, dtype=F32)
    pd = len(POOL_WINDOWS)
    pw = pool_w[l].astype(F32)
    pool_bd = jnp.zeros((WIDTH, WIDTH), F32)
    pg = WIDTH // pd
    for gi in range(pd):
        pool_bd = pool_bd.at[gi * pg:(gi + 1) * pg, gi * pg:(gi + 1) * pg].set(pw[gi])
    lanes = jnp.arange(SMALL_W)
    head_of_lane = jnp.clip(lanes - FOX_HEADS, 0, SSD_HEADS - 1)
    dt_lane = (lanes >= FOX_HEADS) & (lanes < FOX_HEADS + SSD_HEADS)
    a = -jnp.exp(ssd_a_log[l].astype(F32))
    return dict(
        g1=norm1_g[l][None, :],
        w_p=w_p,
        w_gate=w[:, o_gate:].astype(BF16),
        cm_g=cm_norm_g[l][None, :],
        ws_prompt=ws.astype(BF16),
        cmb_prompt=jnp.repeat(cm_b[l].T, gd, axis=1),
        ws_sample=jnp.stack([jnp.kron(ws[gi, :n_tok, :n_tok], eye) for gi in range(CM_GROUPS)]).astype(BF16),
        cmb_sample=jnp.repeat(jnp.repeat(cm_b[l].T[:n_tok], n_dec, axis=0), gd, axis=1),
        pool_bd=pool_bd.astype(BF16),
        pool_scale=pool_scale[l][None, :].astype(F32),
        bf_row=jnp.pad(fox_bf[l], (0, SMALL_W - FOX_HEADS))[None, :],
        bf_col=jnp.broadcast_to(jnp.tile(jnp.pad(fox_bf[l], (0, HEAD_PAD - FOX_HEADS)), n_tok)[:, None],
                                (n_tok * HEAD_PAD, PAGE_SIZE)),
        conv_w=ssd_conv_w[l], conv_b=ssd_conv_b[l][None, :],
        dtb_row=jnp.where(dt_lane, ssd_dt_bias[l][head_of_lane], 0.0)[None, :],
        a_row=jnp.where(dt_lane, a[head_of_lane], 0.0)[None, :],
        d_row=jnp.repeat(ssd_d[l].astype(F32), SSD_HEAD_DIM)[None, :],
        dtb_bh=jnp.tile(ssd_dt_bias[l], n_dec)[:, None, None],
        a_bh=jnp.tile(a, n_dec)[:, None, None],
        ssd_g=ssd_norm_g[l][None, :],
        w_branch=w_branch[l].astype(BF16),
        w_out=w_out[l].astype(BF16),
        g2=norm2_g[l][None, :],
        ffn_up=ffn_up[l].astype(BF16),
        ffn_cw=ffn_conv_w[l], ffn_cb=ffn_conv_b[l][None, :],
        ffn_down=ffn_down[l].astype(BF16),
    )


def _prompt_layer(x, w, consts, n_seq):
    m = x.shape[0]
    seq_len = m // n_seq
    selq, selk, selv = consts["sel"]
    big = 2048 if seq_len % 2048 == 0 else 512
    mid = 1024 if seq_len % 1024 == 0 else 512
    p, k_new, v_new = _proj(x, w["g1"], w["w_p"], tm=mid)
    y_a, = _mixa(p, w["ws_prompt"], w["cmb_prompt"], w["cm_g"], t=CM_CHUNK, tm=big, with_va=False)
    y_b = _pool(p, U_POOL, jnp.zeros((n_seq, POOL_PAD_STEPS, WIDTH), F32), w["pool_bd"], w["pool_scale"],
                n_seq=n_seq, s=1, tm=big, pos0=0)
    lf, qa, ka, va = _fox_prep(p, w["bf_row"], selq, selk, selv, n_seq=n_seq, tm=mid)
    tq = mid
    y_c = _fox_flash(qa, ka, va, tq=tq, wq=512)
    y_d, ssd_last = _ssd_prompt(p, w["conv_w"], w["conv_b"], w["dtb_row"], w["a_row"], w["d_row"], w["ssd_g"],
                                n_seq=n_seq)
    x = _merge(x, w["g1"], (y_a, y_b, y_c, y_d), w["w_gate"], w["w_branch"], w["w_out"], tm=512)
    x, ffn_hist = _ffn(x, w["g2"], jnp.zeros((n_seq, 8, 2 * D_FF), F32), w["ffn_up"], w["ffn_cw"],
                       w["ffn_cb"], w["ffn_down"], n_seq=n_seq, s=1, tm=512, cwid=D_FF)
    p3 = p.reshape(n_seq, seq_len, P_COLS)
    unit = lambda u: p3[:, :, u * WIDTH:(u + 1) * WIDTH]
    state = dict(
        k=k_new.reshape(n_seq, seq_len, FOX_HEADS, FOX_HEAD_DIM),
        v=v_new.reshape(n_seq, seq_len, FOX_HEADS, FOX_HEAD_DIM),
        logf=lf.reshape(n_seq, seq_len, SMALL_W)[:, :, :FOX_HEADS],
        pool=unit(U_POOL)[:, -POOL_HIST:],
        ssd_conv=p3[:, -(SSD_CONV - 1):, U_XS * WIDTH:U_XS * WIDTH + SSD_CONV_DIM],
        ssd=ssd_last.reshape(n_seq, SSD_HEADS, SSD_HEAD_DIM, SSD_STATE),
        ffn_conv=ffn_hist[:, -(FFN_CONV - 1):],
    )
    return x, state


def _to_time_major(a):
    return jnp.swapaxes(a, 0, 1).reshape((a.shape[0] * a.shape[1],) + a.shape[2:])


def _from_time_major(a, n_b):
    return jnp.swapaxes(a.reshape((a.shape[0] // n_b, n_b) + a.shape[1:]), 0, 1)


def _sample_layer(x, w, consts, layer, past, page_table, n_b, past_len):
    m = x.shape[0]
    n_tok = m // n_b
    p, k_new, v_new = _proj(x, w["g1"], w["w_p"], tm=m)
    unit = lambda u: p[:, u * WIDTH:(u + 1) * WIDTH]
    small = p[:, 10 * WIDTH:]

    y_a, va = _mixa(p, w["ws_sample"], w["cmb_sample"], w["cm_g"], t=m, tm=m, with_va=True)

    pool_new = unit(U_POOL)
    pool_hist = jnp.concatenate([jnp.zeros(((POOL_PAD_STEPS - POOL_HIST) * n_b, WIDTH), F32),
                                 _to_time_major(past["pool"])], axis=0)
    y_b = _pool(pool_new, 0, pool_hist[None], w["pool_bd"], w["pool_scale"],
                n_seq=1, s=n_b, tm=m, pos0=past_len)
    new_pool = jnp.concatenate([past["pool"], _from_time_major(pool_new, n_b)], axis=1)[:, -POOL_HIST:]

    pad_tok = lambda a: jnp.pad(_from_time_major(a, n_b), ((0, 0), (0, TOK_PAD - n_tok), (0, 0)))
    f_raw = _from_time_major(small[:, :FOX_HEADS], n_b)
    fr_lane = jnp.pad(jnp.swapaxes(f_raw, 1, 2),
                      ((0, 0), (0, HEAD_PAD - FOX_HEADS), (0, PAGE_SIZE - n_tok)))
    fr_lane = jnp.tile(fr_lane, (1, n_tok, 1))
    y_c = _fox_decode(layer, page_table, _from_time_major(unit(U_Q), n_b), pad_tok(k_new),
                      pad_tok(v_new), fr_lane, w["bf_col"], past["cache_k"], past["cache_v"],
                      past["cache_lft"])
    y_c = _to_time_major(y_c)
    logf = _log_sigmoid_rows(small, w["bf_row"])[:, :FOX_HEADS]

    xbc_new = p[:, U_XS * WIDTH:U_XS * WIDTH + SSD_CONV_DIM]
    xbc_ext = jnp.concatenate([_to_time_major(past["ssd_conv"]), xbc_new], axis=0)
    act = _conv_silu(xbc_ext, w["conv_w"], w["conv_b"], s=n_b, rows=m)
    xs = act[:, :WIDTH]
    rep_heads = lambda a: jnp.repeat(_from_time_major(a, n_b).reshape(n_b, n_tok, SSD_GROUPS, SSD_STATE),
                                     SSD_HEADS // SSD_GROUPS, axis=2)
    to_bh = lambda a: jnp.transpose(a, (0, 2, 1, 3)).reshape(n_b * SSD_HEADS, n_tok, SSD_STATE)
    x_bh = jnp.transpose(_from_time_major(xs, n_b).reshape(n_b, n_tok, SSD_HEADS, SSD_HEAD_DIM),
                         (0, 2, 3, 1)).reshape(n_b * SSD_HEADS, SSD_HEAD_DIM, n_tok)
    dtr = _from_time_major(small[:, FOX_HEADS:FOX_HEADS + SSD_HEADS], n_b)
    dtr_bh = jnp.swapaxes(dtr, 1, 2).reshape(n_b * SSD_HEADS, 1, n_tok)
    h0 = past["ssd"].reshape(n_b * SSD_HEADS, SSD_HEAD_DIM, SSD_STATE)
    y_bh, h_last = _ssd_step(x_bh, to_bh(rep_heads(act[:, WIDTH:2 * WIDTH])),
                             to_bh(rep_heads(act[:, 2 * WIDTH:])), dtr_bh, w["dtb_bh"], w["a_bh"], h0, blk=32)
    y_s = _to_time_major(jnp.transpose(y_bh.reshape(n_b, SSD_HEADS, SSD_HEAD_DIM, n_tok),
                                       (0, 3, 1, 2)).reshape(n_b, n_tok, WIDTH))
    y_d = _ssd_gate(y_s, xs, unit(U_Z), w["d_row"], w["ssd_g"])

    x = _merge(x, w["g1"], (y_a, y_b, y_c, y_d), w["w_gate"], w["w_branch"], w["w_out"], tm=min(m, 256))
    ffn_h0 = _to_time_major(past["ffn_conv"])[None]
    x, ffn_hist = _ffn(x, w["g2"], ffn_h0, w["ffn_up"], w["ffn_cw"], w["ffn_cb"], w["ffn_down"],
                       n_seq=1, s=n_b, tm=m, cwid=D_FF // 2)
    state = dict(
        k=_from_time_major(k_new, n_b).reshape(n_b, n_tok, FOX_HEADS, FOX_HEAD_DIM),
        v=_from_time_major(v_new, n_b).reshape(n_b, n_tok, FOX_HEADS, FOX_HEAD_DIM),
        logf=_from_time_major(logf, n_b),
        chunk_v=_from_time_major(va, n_b),
        pool=new_pool,
        ssd_conv=_from_time_major(xbc_ext[-(SSD_CONV - 1) * n_b:], n_b),
        ssd=h_last.reshape(n_b, SSD_HEADS, SSD_HEAD_DIM, SSD_STATE),
        ffn_conv=_from_time_major(ffn_hist[0], n_b),
    )
    return x, state


def _log_sigmoid_body(x_ref, b_ref, o_ref):
    o_ref[...] = _log_sigmoid(x_ref[...] + b_ref[...])


def _log_sigmoid_rows(small, bf_row):
    return pl.pallas_call(
        _log_sigmoid_body,
        out_shape=jax.ShapeDtypeStruct(small.shape, F32),
        name="log_forget",
    )(small, bf_row)


def kernel(x_prompt, x_sample, cache_k, cache_v, cache_logf, page_table, state_pool, state_ssd_conv, state_ssd,
           state_ffn_conv, norm1_g, w_in, cm_norm_g, cm_ws, cm_b, pool_w, pool_scale, fox_bf, ssd_conv_w,
           ssd_conv_b, ssd_dt_bias, ssd_a_log, ssd_d, ssd_norm_g, w_branch, w_out, norm2_g, ffn_up, ffn_conv_w,
           ffn_conv_b, ffn_down, final_norm_g):
    n_seq, seq_len, _ = x_prompt.shape
    n_b, n_tok, _ = x_sample.shape
    depth = w_in.shape[0]
    n_pages = page_table.shape[1]
    past_len = n_pages * cache_k.shape[2]
    n_phys = cache_k.shape[1]

    consts = dict(sel=_selection_matrices())
    ck = jnp.transpose(cache_k, (0, 1, 3, 4, 2)).reshape(depth, n_phys, WIDTH, PAGE_SIZE)
    cv = jnp.transpose(cache_v, (0, 1, 3, 4, 2)).reshape(depth, n_phys, WIDTH, PAGE_SIZE)
    clt = jnp.pad(jnp.swapaxes(cache_logf, 2, 3), ((0, 0), (0, 0), (0, HEAD_PAD - FOX_HEADS), (0, 0)))

    xp = x_prompt.reshape(n_seq * seq_len, D_MODEL)
    xs = _to_time_major(x_sample)
    st_p, st_s = [], []
    for l in range(depth):
        w = _layer_params(l, norm1_g, w_in, cm_norm_g, cm_ws, cm_b, pool_w, pool_scale, fox_bf, ssd_conv_w,
                          ssd_conv_b, ssd_dt_bias, ssd_a_log, ssd_d, ssd_norm_g, w_branch, w_out, norm2_g,
                          ffn_up, ffn_conv_w, ffn_conv_b, ffn_down, n_b, n_tok)
        past = dict(cache_k=ck, cache_v=cv, cache_lft=clt, pool=state_pool[l], ssd_conv=state_ssd_conv[l],
                    ssd=state_ssd[l], ffn_conv=state_ffn_conv[l])
        xp, sp = _prompt_layer(xp, w, consts, n_seq)
        xs, ss = _sample_layer(xs, w, consts, l, past, page_table, n_b, past_len)
        st_p.append(sp)
        st_s.append(ss)
    gfin = final_norm_g[None, :]
    y_prompt = _final_norm(xp, gfin, tm=2048 if xp.shape[0] % 2048 == 0 else 512).reshape(n_seq, seq_len, D_MODEL)
    y_sample = _from_time_major(_final_norm(xs, gfin, tm=xs.shape[0]), n_b)

    def stacked(states, name):
        return jnp.stack([s[name] for s in states], axis=0)

    return (y_prompt, y_sample,
            stacked(st_p, "k"), stacked(st_p, "v"), stacked(st_p, "logf"),
            stacked(st_s, "k"), stacked(st_s, "v"), stacked(st_s, "logf"),
            stacked(st_s, "chunk_v"),
            stacked(st_p, "pool"), stacked(st_s, "pool"),
            stacked(st_p, "ssd_conv"), stacked(st_s, "ssd_conv"),
            stacked(st_p, "ssd"), stacked(st_s, "ssd"),
            stacked(st_p, "ffn_conv"), stacked(st_s, "ffn_conv"))
```

```python
import functools
import math

import numpy as np
import jax
import jax.numpy as jnp
from jax import lax
from jax.experimental import pallas as pl
from jax.experimental.pallas import tpu as pltpu

F32 = jnp.float32
BF16 = jnp.bfloat16

D_MODEL = 1024
WIDTH = 256
PAGE_SIZE = 128
CM_GROUPS = 4
CM_CHUNK = 128
POOL_WINDOWS = (2, 4, 8, 16)
POOL_HIST = 15
POOL_PAD_STEPS = 32
FOX_HEADS = 4
FOX_HEAD_DIM = 64
SSD_HEADS = 4
SSD_HEAD_DIM = 64
SSD_GROUPS = 2
SSD_STATE = 128
SSD_CONV = 4
SSD_CHUNK = 128
SSD_CONV_DIM = 768
D_FF = 2816
FFN_CONV = 3
N_BRANCH = 4
EPS = 1e-6
SMALL_W = 128
P_COLS = 10 * WIDTH + SMALL_W
SMALL_BLK = 10 * WIDTH // SMALL_W
U_AU, U_AV, U_POOL, U_Q, U_K, U_V, U_Z, U_XS, U_BM, U_CM = range(10)

VMEM_LIMIT = 56 * 1024 * 1024


def _cparams(sem):
    return pltpu.CompilerParams(dimension_semantics=sem, vmem_limit_bytes=VMEM_LIMIT)


def _sigmoid(x):
    return 1.0 / (1.0 + jnp.exp(-x))


def _silu(x):
    return x * _sigmoid(x)


def _softplus(x):
    return jnp.maximum(x, 0.0) + jnp.log(1.0 + jnp.exp(-jnp.abs(x)))


def _log_sigmoid(x):
    return -_softplus(-x)


def _gelu_tanh(x):
    return 0.5 * x * (1.0 + jnp.tanh(math.sqrt(2.0 / math.pi) * (x + 0.044715 * (x * x * x))))


def _rms(x, g):
    return x * lax.rsqrt(jnp.mean(x * x, axis=-1, keepdims=True) + EPS) * g


def _split3(x):
    hi = x.astype(BF16)
    r1 = x - hi.astype(F32)
    mid = r1.astype(BF16)
    lo = (r1 - mid.astype(F32)).astype(BF16)
    return hi, mid, lo


def _dot(a, b):
    return jnp.dot(a, b, preferred_element_type=F32)


def _dot_nt(a, b):
    return lax.dot_general(a, b, (((1,), (1,)), ((), ())), preferred_element_type=F32)


def _ones_dot_left(ones_bf16, x):
    hi, mid, lo = _split3(x)
    return _dot(ones_bf16, hi) + _dot(ones_bf16, mid) + _dot(ones_bf16, lo)


def _ones_dot_right(x, ones_bf16):
    hi, mid, lo = _split3(x)
    return _dot(hi, ones_bf16) + _dot(mid, ones_bf16) + _dot(lo, ones_bf16)


def _iota(shape, dim):
    return lax.broadcasted_iota(jnp.int32, shape, dim)


def _proj_body(x_ref, g_ref, w_ref, o_ref):
    h = _rms(x_ref[...], g_ref[...]).astype(BF16)
    o_ref[...] = _dot(h, w_ref[...])


def _proj(x, g, w, tm):
    m, d = x.shape
    n = w.shape[1]
    return pl.pallas_call(
        _proj_body,
        grid=(m // tm,),
        in_specs=[pl.BlockSpec((tm, d), lambda i: (i, 0)),
                  pl.BlockSpec((1, d), lambda i: (0, 0)),
                  pl.BlockSpec((d, n), lambda i: (0, 0))],
        out_specs=pl.BlockSpec((tm, n), lambda i: (i, 0)),
        out_shape=jax.ShapeDtypeStruct((m, n), F32),
        compiler_params=_cparams(("arbitrary",)),
        name="proj",
    )(x, g, w)


def _mixa_body(u_ref, v_ref, wm_ref, b_ref, g_ref, y_ref, *maybe_va_ref, t, n_chunks):
    lane_grp = _iota((t, WIDTH), 1) >> 6
    for c in range(n_chunks):
        rows = pl.ds(c * t, t)
        u = _gelu_tanh(u_ref[rows, :])
        va = _rms(_gelu_tanh(v_ref[rows, :]), g_ref[...])
        for va_ref in maybe_va_ref:
            va_ref[rows, :] = va
        vab = va.astype(BF16)
        zero = jnp.zeros_like(vab)
        s = b_ref[...]
        for gi in range(CM_GROUPS):
            s = s + _dot(wm_ref[gi], jnp.where(lane_grp == gi, vab, zero))
        y_ref[rows, :] = u * s


def _mixa(p, wm, bias, g, t, tm, with_va):
    m = p.shape[0]
    n_out = 2 if with_va else 1
    body = functools.partial(_mixa_body, t=t, n_chunks=tm // t)
    return pl.pallas_call(
        body,
        grid=(m // tm,),
        in_specs=[pl.BlockSpec((tm, WIDTH), lambda i: (i, U_AU)),
                  pl.BlockSpec((tm, WIDTH), lambda i: (i, U_AV)),
                  pl.BlockSpec((CM_GROUPS, t, t), lambda i: (0, 0, 0)),
                  pl.BlockSpec((t, WIDTH), lambda i: (0, 0)),
                  pl.BlockSpec((1, WIDTH), lambda i: (0, 0))],
        out_specs=[pl.BlockSpec((tm, WIDTH), lambda i: (i, 0))] * n_out,
        out_shape=[jax.ShapeDtypeStruct((m, WIDTH), F32)] * n_out,
        compiler_params=_cparams(("arbitrary",)),
        name="mixer_a",
    )(p, p, wm, bias, g)


def _pool_body(seq_ref, h0_ref, w_ref, scale_ref, y_ref, e1, e2, e4, e8, *, s, tm, pos0, steps_per_tile):
    j = pl.program_id(1)
    hr = POOL_PAD_STEPS * s
    tot = hr + tm

    @pl.when(j == 0)
    def _():
        e1[0:hr, :] = h0_ref[0]

    e1[hr:tot, :] = seq_ref[...]
    e2[8 * s:tot, :] = e1[8 * s:tot, :] + e1[7 * s:tot - s, :]
    e4[16 * s:tot, :] = e2[16 * s:tot, :] + e2[14 * s:tot - 2 * s, :]
    e8[24 * s:tot, :] = e4[24 * s:tot, :] + e4[20 * s:tot - 4 * s, :]
    cur = e1[hr:tot, :]
    s2 = e2[hr:tot, :]
    s4 = e4[hr:tot, :]
    s8 = e8[hr:tot, :]
    s16 = s8 + e8[hr - 8 * s:tot - 8 * s, :]
    lane_grp = _iota((tm, WIDTH), 1) >> 6
    pos = pos0 + j * steps_per_tile + (_iota((tm, WIDTH), 0) >> (s.bit_length() - 1))
    wsum = jnp.where(lane_grp == 0, s2, jnp.where(lane_grp == 1, s4, jnp.where(lane_grp == 2, s8, s16)))
    win = jnp.where(lane_grp == 0, 2, jnp.where(lane_grp == 1, 4, jnp.where(lane_grp == 2, 8, 16)))
    count = jnp.minimum(pos + 1, win).astype(F32)
    d = wsum / count - cur
    y_ref[...] = _dot(d.astype(BF16), w_ref[...]) * scale_ref[...]
    e1[0:hr, :] = e1[tm:tot, :]


def _pool(seq, col_blk, h0, w_bd, scale, *, n_seq, s, tm, pos0):
    m = seq.shape[0]
    tiles = m // n_seq // tm
    hr = POOL_PAD_STEPS * s
    body = functools.partial(_pool_body, s=s, tm=tm, pos0=pos0, steps_per_tile=tm // s)
    return pl.pallas_call(
        body,
        grid=(n_seq, tiles),
        in_specs=[pl.BlockSpec((tm, WIDTH), lambda n, j: (n * tiles + j, col_blk)),
                  pl.BlockSpec((1, hr, WIDTH), lambda n, j: (n, 0, 0)),
                  pl.BlockSpec((WIDTH, WIDTH), lambda n, j: (0, 0)),
                  pl.BlockSpec((1, WIDTH), lambda n, j: (0, 0))],
        out_specs=pl.BlockSpec((tm, WIDTH), lambda n, j: (n * tiles + j, 0)),
        out_shape=jax.ShapeDtypeStruct((m, WIDTH), F32),
        scratch_shapes=[pltpu.VMEM((hr + tm, WIDTH), F32) for _ in range(4)],
        compiler_params=_cparams(("arbitrary", "arbitrary")),
        name="mixer_b",
    )(seq, h0, w_bd, scale)


LOG2E = math.log2(math.e)


V_ROWS = 2 * FOX_HEAD_DIM


def _fox_prep_body(q_ref, k_ref, v_ref, sm_ref, bf_ref, selq_ref, selk_ref, selv_ref,
                   lf_ref, qa_ref, ka_ref, va_ref, carry, *, tm):
    j = pl.program_id(1)

    @pl.when(j == 0)
    def _():
        carry[...] = jnp.zeros_like(carry)

    logf = _log_sigmoid(sm_ref[...] + bf_ref[...])
    lf_ref[...] = logf
    tril = (_iota((tm, tm), 0) >= _iota((tm, tm), 1)).astype(BF16)
    c = _ones_dot_left(tril, logf) + carry[...]
    carry[...] = c[tm - 1:tm, :]
    c = c * LOG2E
    c_t = jnp.transpose(c)
    qb = (q_ref[...] * (FOX_HEAD_DIM ** -0.5 * LOG2E)).astype(BF16)
    kb = k_ref[...].astype(BF16)
    vb = v_ref[...].astype(BF16)
    d = FOX_HEAD_DIM
    lane = _iota((tm, 2 * d), 1)
    row = _iota((2 * d, tm), 0)

    def pieces(x):
        hi = x.astype(BF16).astype(F32)
        r1 = x - hi
        mid = r1.astype(BF16).astype(F32)
        return hi, mid, (r1 - mid).astype(BF16).astype(F32)

    for h in range(FOX_HEADS):
        hi, mid, lo = pieces(jnp.broadcast_to(c_t[h:h + 1, :], (2 * d, tm)))
        q_extra = jnp.where(row == d, hi, jnp.where(row == d + 1, mid, jnp.where(row == d + 2, lo,
                  jnp.where((row >= d + 3) & (row < d + 6), 1.0, 0.0))))
        qa_ref[0, h] = (_dot_nt(selq_ref[h], qb) + q_extra).astype(BF16)
        hi, mid, lo = pieces(jnp.broadcast_to(c[:, h:h + 1], (tm, 2 * d)))
        k_extra = jnp.where(lane == d + 3, -hi, jnp.where(lane == d + 4, -mid, jnp.where(lane == d + 5, -lo,
                  jnp.where((lane >= d) & (lane < d + 3), 1.0, 0.0))))
        ka_ref[0, h] = (_dot(kb, selk_ref[h]) + k_extra).astype(BF16)
        ones_row = jnp.where(row[:V_ROWS] == d, 1.0, 0.0)
        va_ref[0, h] = (_dot_nt(selv_ref[h], vb) + ones_row).astype(BF16)


def _fox_prep(p, bf_row, selq, selk, selv, *, n_seq, tm):
    m = p.shape[0]
    seq_len = m // n_seq
    tiles = seq_len // tm
    aug = jax.ShapeDtypeStruct((n_seq, FOX_HEADS, seq_len, 2 * FOX_HEAD_DIM), BF16)
    aug_spec = pl.BlockSpec((1, FOX_HEADS, tm, 2 * FOX_HEAD_DIM), lambda n, j: (n, 0, j, 0))
    def transposed(rows):
        return (jax.ShapeDtypeStruct((n_seq, FOX_HEADS, rows, seq_len), BF16),
                pl.BlockSpec((1, FOX_HEADS, rows, tm), lambda n, j: (n, 0, 0, j)),
                pl.BlockSpec((FOX_HEADS, rows, WIDTH), lambda n, j: (0, 0, 0)))

    aug_t, aug_t_spec, selq_spec = transposed(2 * FOX_HEAD_DIM)
    v_t, v_t_spec, selv_spec = transposed(V_ROWS)
    sel_spec = pl.BlockSpec((FOX_HEADS, WIDTH, 2 * FOX_HEAD_DIM), lambda n, j: (0, 0, 0))
    return pl.pallas_call(
        functools.partial(_fox_prep_body, tm=tm),
        grid=(n_seq, tiles),
        in_specs=[pl.BlockSpec((tm, WIDTH), lambda n, j: (n * tiles + j, U_Q)),
                  pl.BlockSpec((tm, WIDTH), lambda n, j: (n * tiles + j, U_K)),
                  pl.BlockSpec((tm, WIDTH), lambda n, j: (n * tiles + j, U_V)),
                  pl.BlockSpec((tm, SMALL_W), lambda n, j: (n * tiles + j, SMALL_BLK)),
                  pl.BlockSpec((1, SMALL_W), lambda n, j: (0, 0)),
                  selq_spec, sel_spec, selv_spec],
        out_specs=[pl.BlockSpec((tm, SMALL_W), lambda n, j: (n * tiles + j, 0)),
                   aug_t_spec, aug_spec, v_t_spec],
        out_shape=[jax.ShapeDtypeStruct((m, SMALL_W), F32), aug_t, aug, v_t],
        scratch_shapes=[pltpu.VMEM((1, SMALL_W), F32)],
        compiler_params=_cparams(("arbitrary", "arbitrary")),
        name="fox_prep",
    )(p, p, p, p, bf_row, selq, selk, selv)


def _fox_flash_body(qi_ref, kj_ref, q_ref, k_ref, v_ref, o_ref, m_ref, acc_ref, *, tq, wq):
    step = pl.program_id(1)
    qi = qi_ref[step]
    kj = kj_ref[step]
    d = FOX_HEAD_DIM

    @pl.when(kj == 0)
    def _():
        m_ref[...] = jnp.full_like(m_ref, -jnp.inf)
        acc_ref[...] = jnp.zeros_like(acc_ref)

    n_split = tq // wq
    chains = [(h, c) for h in range(FOX_HEADS) for c in range(n_split)]

    def update(diagonal):
        n_keys = [(c + 1) * wq if diagonal else tq for _, c in chains]
        scores = [_dot(k_ref[0, h, 0:nk, :], q_ref[0, h, :, c * wq:(c + 1) * wq])
                  for (h, c), nk in zip(chains, n_keys)]
        for i, ((h, c), nk, s) in enumerate(zip(chains, n_keys, scores)):
            if diagonal:
                s = jnp.where(_iota((nk, wq), 0) <= _iota((nk, wq), 1) + c * wq, s, -jnp.inf)
            m_prev = m_ref[i]
            m_new = jnp.maximum(m_prev, jnp.max(s, axis=0, keepdims=True))
            alpha = jnp.exp2(m_prev - m_new)
            pexp = jnp.exp2(s - m_new).astype(BF16)
            acc_ref[i] = alpha * acc_ref[i] + _dot(v_ref[0, h, :, 0:nk], pexp)
            m_ref[i] = m_new

    @pl.when(kj < qi)
    def _():
        update(False)

    @pl.when(kj == qi)
    def _():
        update(True)
        o = []
        for h in range(FOX_HEADS):
            parts = []
            for c in range(n_split):
                acc = acc_ref[h * n_split + c]
                parts.append(acc[0:d, :] / acc[d:d + 1, :])
            o.append(jnp.concatenate(parts, axis=1))
        o_ref[...] = jnp.concatenate([jnp.transpose(jnp.concatenate(o[0:2], axis=0)),
                                      jnp.transpose(jnp.concatenate(o[2:4], axis=0))], axis=1)


def _fox_flash(qa_t, ka, va_t, *, tq, wq):
    n_seq, _, seq_len, w = ka.shape
    nq = seq_len // tq
    qi = np.array([i for i in range(nq) for _ in range(i + 1)], np.int32)
    kj = np.array([j for i in range(nq) for j in range(i + 1)], np.int32)
    q_spec = pl.BlockSpec((1, FOX_HEADS, w, tq), lambda n, s, qi, kj: (n, 0, 0, qi[s]))
    k_spec = pl.BlockSpec((1, FOX_HEADS, tq, w), lambda n, s, qi, kj: (n, 0, kj[s], 0))
    v_spec = pl.BlockSpec((1, FOX_HEADS, V_ROWS, tq), lambda n, s, qi, kj: (n, 0, 0, kj[s]))
    grid_spec = pltpu.PrefetchScalarGridSpec(
        num_scalar_prefetch=2,
        grid=(n_seq, len(qi)),
        in_specs=[q_spec, k_spec, v_spec],
        out_specs=pl.BlockSpec((tq, WIDTH), lambda n, s, qi, kj: (n * nq + qi[s], 0)),
        scratch_shapes=[pltpu.VMEM((FOX_HEADS * (tq // wq), 1, wq), F32),
                        pltpu.VMEM((FOX_HEADS * (tq // wq), V_ROWS, wq), F32)],
    )
    return pl.pallas_call(
        functools.partial(_fox_flash_body, tq=tq, wq=wq),
        grid_spec=grid_spec,
        out_shape=jax.ShapeDtypeStruct((n_seq * seq_len, WIDTH), F32),
        compiler_params=_cparams(("arbitrary", "arbitrary")),
        name="fox_flash",
    )(jnp.asarray(qi), jnp.asarray(kj), qa_t, ka, va_t)


TOK_PAD = 8
HEAD_PAD = 8


DECODE_SEQS = 2
DECODE_SLOTS = 2 * DECODE_SEQS


def _fox_decode_body(pt_ref, q_ref, kn_ref, vn_ref, fr_ref, bfc_ref, ck_hbm, cv_hbm, cl_hbm, o_ref,
                     kbuf, vbuf, lbuf, sems, *, layer, n_seq, n_pages, n_tok):
    b = pl.program_id(0)

    def page_copies(page, slot, pg):
        return [pltpu.make_async_copy(src.at[layer, page], dst.at[slot, pg], sems.at[slot])
                for src, dst in ((ck_hbm, kbuf), (cv_hbm, vbuf), (cl_hbm, lbuf))]

    def fetch(seq):
        @pl.when(seq < n_seq)
        def _():
            for pg in range(n_pages):
                for cp in page_copies(pt_ref[seq, pg], seq % DECODE_SLOTS, pg):
                    cp.start()

    @pl.when(b == 0)
    def _():
        for seq in range(DECODE_SLOTS):
            fetch(seq)

    seqs = [b * DECODE_SEQS + u for u in range(DECODE_SEQS)]
    for seq in seqs:
        for pg in range(n_pages):
            for cp in page_copies(0, seq % DECODE_SLOTS, pg):
                cp.wait()
    for u, seq in enumerate(seqs):
        slot = seq % DECODE_SLOTS
        o_ref[u] = _fox_decode_one(q_ref[u], kn_ref[u], vn_ref[u], fr_ref[u], bfc_ref[...],
                                   [kbuf.at[slot, pg] for pg in range(n_pages)],
                                   [vbuf.at[slot, pg] for pg in range(n_pages)],
                                   [lbuf.at[slot, pg] for pg in range(n_pages)], n_tok=n_tok)
    for seq in seqs:
        fetch(seq + DECODE_SLOTS)


def _fox_decode_one(q, k_new8, v_new8, fr, bfc, k_refs, v_refs, lf_refs, *, n_tok):
    n_pages = len(k_refs)
    d = FOX_HEAD_DIM
    ps = PAGE_SIZE
    qrows = n_tok * HEAD_PAD

    q4 = q * (d ** -0.5)
    own = (_iota((HEAD_PAD, WIDTH), 1) >> 6) == _iota((HEAD_PAD, WIDTH), 0)
    qbd = jnp.concatenate([jnp.where(own, jnp.broadcast_to(q4[t:t + 1, :], (HEAD_PAD, WIDTH)), 0.0)
                           for t in range(n_tok)], axis=0).astype(BF16)

    triu = (_iota((ps, ps), 0) <= _iota((ps, ps), 1)).astype(BF16)
    ones = jnp.ones((ps, ps), BF16)
    lf_new = _log_sigmoid(fr + bfc)
    newcum_lane = _ones_dot_right(lf_new, triu)
    row_t = _iota((qrows, ps), 0) >> 3
    lane_i = _iota((qrows, ps), 1)
    newcum_col = jnp.sum(jnp.where(lane_i == row_t, newcum_lane, 0.0), axis=1, keepdims=True)

    lst = jnp.concatenate([r[...] for r in lf_refs], axis=0)
    incl = _ones_dot_right(lst, triu)
    tot = _ones_dot_right(lst, ones)
    later = jnp.zeros((HEAD_PAD, ps), F32)
    bias = [None] * n_pages
    for pg in reversed(range(n_pages)):
        rows = slice(pg * HEAD_PAD, (pg + 1) * HEAD_PAD)
        g = (tot[rows] - incl[rows]) + later
        bias[pg] = jnp.concatenate([g] * n_tok, axis=0) + newcum_col
        later = later + tot[rows]

    k_new = jnp.concatenate([k_new8, jnp.zeros((ps - TOK_PAD, WIDTH), F32)], axis=0).astype(BF16)
    v_new = jnp.concatenate([v_new8, jnp.zeros((ps - TOK_PAD, WIDTH), F32)], axis=0).astype(BF16)
    s_new = _dot_nt(qbd, k_new) + newcum_col - newcum_lane
    s_new = jnp.where(lane_i <= row_t, s_new, -jnp.inf)
    s_past = [_dot(qbd, k_refs[pg][...].astype(BF16)) + bias[pg] for pg in range(n_pages)]

    m_lane = s_new
    for sp in s_past:
        m_lane = jnp.maximum(m_lane, sp)
    m = jnp.max(m_lane, axis=1, keepdims=True)
    p_new = jnp.exp(s_new - m)
    l_lane = p_new
    acc = _dot(p_new.astype(BF16), v_new)
    for pg in range(n_pages):
        pp = jnp.exp(s_past[pg] - m)
        l_lane = l_lane + pp
        acc = acc + _dot_nt(pp.astype(BF16), v_refs[pg][...].astype(BF16))
    o = acc / jnp.sum(l_lane, axis=1, keepdims=True)
    keep = jnp.concatenate([own] * n_tok, axis=0)
    return jnp.sum(jnp.where(keep, o, 0.0).reshape(n_tok, HEAD_PAD, WIDTH), axis=1)


def _fox_decode(layer, page_table, q, kn8, vn8, fr_lane, bf_col, cache_k, cache_v, cache_lft):
    n_b, n_pages = page_table.shape
    n_tok = q.shape[1]
    qrows = n_tok * HEAD_PAD
    ps = PAGE_SIZE

    assert n_b % DECODE_SEQS == 0
    per_seq = lambda rows, cols: pl.BlockSpec((DECODE_SEQS, rows, cols), lambda b, pt: (b, 0, 0))
    hbm = pl.BlockSpec(memory_space=pl.ANY)
    grid_spec = pltpu.PrefetchScalarGridSpec(
        num_scalar_prefetch=1,
        grid=(n_b // DECODE_SEQS,),
        in_specs=[per_seq(n_tok, WIDTH), per_seq(TOK_PAD, WIDTH), per_seq(TOK_PAD, WIDTH),
                  per_seq(qrows, ps),
                  pl.BlockSpec((qrows, ps), lambda b, pt: (0, 0)),
                  hbm, hbm, hbm],
        out_specs=per_seq(n_tok, WIDTH),
        scratch_shapes=[pltpu.VMEM((DECODE_SLOTS, n_pages, WIDTH, ps), F32),
                        pltpu.VMEM((DECODE_SLOTS, n_pages, WIDTH, ps), F32),
                        pltpu.VMEM((DECODE_SLOTS, n_pages, HEAD_PAD, ps), F32),
                        pltpu.SemaphoreType.DMA((DECODE_SLOTS,))],
    )
    return pl.pallas_call(
        functools.partial(_fox_decode_body, layer=layer, n_seq=n_b, n_pages=n_pages, n_tok=n_tok),
        grid_spec=grid_spec,
        out_shape=jax.ShapeDtypeStruct((n_b, n_tok, WIDTH), F32),
        compiler_params=_cparams(("arbitrary",)),
        name="fox_decode",
    )(page_table, q, kn8, vn8, fr_lane, bf_col, cache_k, cache_v, cache_lft)


def _ssd_prompt_body(xs_ref, bm_ref, cm_ref, z_ref, sm_ref, cw_ref, cb_ref, dtb_ref, arow_ref, drow_ref, g_ref,
                     y_ref, hl_ref, ext, state, *, t, n_seq):
    j = pl.program_id(0)

    @pl.when(j == 0)
    def _():
        ext[:, 0:8, :] = jnp.zeros((n_seq, 8, SSD_CONV_DIM), F32)
        state[...] = jnp.zeros_like(state)

    for s in range(n_seq):
        _ssd_chunk(xs_ref.at[s], bm_ref.at[s], cm_ref.at[s], z_ref.at[s], sm_ref.at[s], cw_ref, cb_ref, dtb_ref,
                   arow_ref, drow_ref, g_ref, y_ref.at[s], ext.at[s], state.at[s], t=t)

    @pl.when(j == pl.num_programs(0) - 1)
    def _():
        hl_ref[...] = state[...]


def _ssd_chunk(xs_ref, bm_ref, cm_ref, z_ref, sm_ref, cw_ref, cb_ref, dtb_ref, arow_ref, drow_ref, g_ref,
               y_ref, ext, state, *, t):
    hd = SSD_HEAD_DIM
    w3 = SSD_CONV_DIM

    ext[8:8 + t, 0:WIDTH] = xs_ref[...]
    ext[8:8 + t, WIDTH:2 * WIDTH] = bm_ref[...]
    ext[8:8 + t, 2 * WIDTH:w3] = cm_ref[...]
    conv = cb_ref[...]
    for k in range(SSD_CONV):
        conv = conv + cw_ref[k:k + 1, :] * ext[5 + k:5 + k + t, :]
    ext[0:8, :] = ext[t:t + 8, :]
    act = _silu(conv)
    xs = act[:, 0:WIDTH]
    bm = act[:, WIDTH:2 * WIDTH].astype(BF16)
    cm = act[:, 2 * WIDTH:w3].astype(BF16)

    dt = _softplus(sm_ref[...] + dtb_ref[...])
    da = dt * arow_ref[...]
    tril = (_iota((t, t), 0) >= _iota((t, t), 1))
    acs = _ones_dot_left(tril.astype(BF16), da)
    acs_t = jnp.transpose(acs)
    lane_head = _iota((t, WIDTH), 1) >> 6
    dt_x = jnp.zeros((t, WIDTH), F32)
    end_x = jnp.zeros((t, WIDTH), F32)
    in_x = jnp.zeros((t, WIDTH), F32)
    dec_rows = []
    for h in range(SSD_HEADS):
        col = acs[:, 4 + h:5 + h]
        lastv = acs[t - 1:t, 4 + h:5 + h]
        dt_x = jnp.where(lane_head == h, dt[:, 4 + h:5 + h], dt_x)
        end_x = jnp.where(lane_head == h, jnp.exp(lastv - col), end_x)
        in_x = jnp.where(lane_head == h, jnp.exp(col), in_x)
        dec_rows.append(jnp.broadcast_to(jnp.exp(lastv), (hd, SSD_STATE)))
    xdt = xs * dt_x
    xdt_b = xdt.astype(BF16)

    y = jnp.zeros((t, WIDTH), F32)
    h_in = state[...]
    xw_t = jnp.transpose(xdt * end_x).astype(BF16)
    new_states = []
    for gi in range(SSD_GROUPS):
        bg = bm[:, gi * SSD_STATE:(gi + 1) * SSD_STATE]
        cg = cm[:, gi * SSD_STATE:(gi + 1) * SSD_STATE]
        cbm = _dot_nt(cg, bg)
        rows = slice(gi * 2 * hd, (gi + 1) * 2 * hd)
        for r in range(SSD_HEADS // SSD_GROUPS):
            h = gi * (SSD_HEADS // SSD_GROUPS) + r
            seg = acs[:, 4 + h:5 + h] - acs_t[4 + h:5 + h, :]
            decay = jnp.exp(jnp.where(tril, seg, -jnp.inf))
            yd = _dot((cbm * decay).astype(BF16), xdt_b)
            y = y + jnp.where(lane_head == h, yd, 0.0)
        y_off = _dot_nt(cg, h_in[rows, :].astype(BF16))
        y_off = jnp.concatenate([y_off, y_off], axis=1)
        grp_lane = _iota((t, WIDTH), 1) >> 7
        y = y + jnp.where(grp_lane == gi, y_off * in_x, 0.0)
        new_states.append(_dot(xw_t[rows, :], bg))
    state[...] = h_in * jnp.concatenate(dec_rows, axis=0) + jnp.concatenate(new_states, axis=0)

    yg = (y + drow_ref[...] * xs) * _silu(z_ref[...])
    y_ref[...] = _rms(yg, g_ref[...])


def _ssd_prompt(p, cw, cb, dtb_row, a_row, d_row, g, *, n_seq):
    m = p.shape[0]
    t = SSD_CHUNK
    seq_len = m // n_seq
    hp = SSD_HEADS * SSD_HEAD_DIM
    p3 = p.reshape(n_seq, seq_len, P_COLS)

    def unit(u):
        return pl.BlockSpec((n_seq, t, WIDTH), lambda j: (0, j, u))

    def const(shape):
        return pl.BlockSpec(shape, lambda j: (0,) * len(shape))

    y, h_last = pl.pallas_call(
        functools.partial(_ssd_prompt_body, t=t, n_seq=n_seq),
        grid=(seq_len // t,),
        in_specs=[unit(U_XS), unit(U_BM), unit(U_CM), unit(U_Z),
                  pl.BlockSpec((n_seq, t, SMALL_W), lambda j: (0, j, SMALL_BLK)),
                  const((SSD_CONV, SSD_CONV_DIM)), const((1, SSD_CONV_DIM)),
                  const((1, SMALL_W)), const((1, SMALL_W)), const((1, WIDTH)), const((1, WIDTH))],
        out_specs=[pl.BlockSpec((n_seq, t, WIDTH), lambda j: (0, j, 0)),
                   const((n_seq, hp, SSD_STATE))],
        out_shape=[jax.ShapeDtypeStruct((n_seq, seq_len, WIDTH), F32),
                   jax.ShapeDtypeStruct((n_seq, hp, SSD_STATE), F32)],
        scratch_shapes=[pltpu.VMEM((n_seq, 8 + t, SSD_CONV_DIM), F32),
                        pltpu.VMEM((n_seq, hp, SSD_STATE), F32)],
        compiler_params=_cparams(("arbitrary",)),
        name="ssd_prompt",
    )(p3, p3, p3, p3, p3, cw, cb, dtb_row, a_row, d_row, g)
    return y.reshape(m, WIDTH), h_last


def _conv_silu_body(ext_ref, w_ref, b_ref, o_ref, *, s, taps, rows):
    acc = b_ref[...]
    for k in range(taps):
        acc = acc + w_ref[k:k + 1, :] * ext_ref[k * s:k * s + rows, :]
    o_ref[...] = _silu(acc)


def _conv_silu(ext, w, b, *, s, rows):
    taps = w.shape[0]
    c = ext.shape[1]
    return pl.pallas_call(
        functools.partial(_conv_silu_body, s=s, taps=taps, rows=rows),
        out_shape=jax.ShapeDtypeStruct((rows, c), F32),
        compiler_params=pltpu.CompilerParams(vmem_limit_bytes=VMEM_LIMIT),
        name="conv_silu",
    )(ext, w, b)


def _ssd_step_body(x_ref, b_ref, c_ref, dtr_ref, dtb_ref, a_ref, h0_ref, y_ref, hl_ref, *, n_tok):
    dt = _softplus(dtr_ref[...] + dtb_ref[...])
    da = dt * a_ref[...]
    h = h0_ref[...]
    lane_t = _iota(y_ref.shape, 2)
    y = jnp.zeros(y_ref.shape, F32)
    for t in range(n_tok):
        xdt = x_ref[:, :, t:t + 1] * dt[:, :, t:t + 1]
        h = h * jnp.exp(da[:, :, t:t + 1]) + xdt * b_ref[:, t:t + 1, :]
        yt = jnp.sum(h * c_ref[:, t:t + 1, :], axis=2, keepdims=True)
        y = jnp.where(lane_t == t, yt, y)
    y_ref[...] = y
    hl_ref[...] = h


def _ssd_step(x_bh, b_bh, c_bh, dtr_bh, dtb_bh, a_bh, h0, *, blk):
    bh, hd, n_tok = x_bh.shape
    ns = h0.shape[2]

    def spec(shape):
        return pl.BlockSpec((blk,) + shape, lambda i: (i, 0, 0))

    return pl.pallas_call(
        functools.partial(_ssd_step_body, n_tok=n_tok),
        grid=(bh // blk,),
        in_specs=[spec((hd, n_tok)), spec((n_tok, ns)), spec((n_tok, ns)), spec((1, n_tok)),
                  spec((1, 1)), spec((1, 1)), spec((hd, ns))],
        out_specs=[spec((hd, n_tok)), spec((hd, ns))],
        out_shape=[jax.ShapeDtypeStruct((bh, hd, n_tok), F32),
                   jax.ShapeDtypeStruct((bh, hd, ns), F32)],
        compiler_params=_cparams(("arbitrary",)),
        name="ssd_step",
    )(x_bh, b_bh, c_bh, dtr_bh, dtb_bh, a_bh, h0)


def _ssd_gate_body(y_ref, xs_ref, z_ref, d_ref, g_ref, o_ref):
    yg = (y_ref[...] + d_ref[...] * xs_ref[...]) * _silu(z_ref[...])
    o_ref[...] = _rms(yg, g_ref[...])


def _ssd_gate(y, xs, z, d_row, g):
    return pl.pallas_call(
        _ssd_gate_body,
        out_shape=jax.ShapeDtypeStruct(y.shape, F32),
        name="ssd_gate",
    )(y, xs, z, d_row, g)


def _merge_body(x_ref, g_ref, ya_ref, yb_ref, yc_ref, yd_ref, wg_ref, wb_ref, wo_ref, o_ref):
    x = x_ref[...]
    h = _rms(x, g_ref[...]).astype(BF16)
    merged = jnp.zeros(x.shape, F32)
    for i, y_ref in enumerate((ya_ref, yb_ref, yc_ref, yd_ref)):
        gate = _sigmoid(_dot(h, wg_ref[:, i * D_MODEL:(i + 1) * D_MODEL]))
        merged = merged + gate * _dot(y_ref[...].astype(BF16), wb_ref[i])
    o_ref[...] = x + _dot(merged.astype(BF16), wo_ref[...])


def _merge(x, g, ys, wg, wb, wo, tm):
    m = x.shape[0]
    row = lambda i: (i, 0)
    y_spec = pl.BlockSpec((tm, WIDTH), row)
    return pl.pallas_call(
        _merge_body,
        grid=(m // tm,),
        in_specs=[pl.BlockSpec((tm, D_MODEL), row),
                  pl.BlockSpec((1, D_MODEL), lambda i: (0, 0)),
                  y_spec, y_spec, y_spec, y_spec,
                  pl.BlockSpec((D_MODEL, N_BRANCH * D_MODEL), lambda i: (0, 0)),
                  pl.BlockSpec((N_BRANCH, WIDTH, D_MODEL), lambda i: (0, 0, 0)),
                  pl.BlockSpec((D_MODEL, D_MODEL), lambda i: (0, 0))],
        out_specs=pl.BlockSpec((tm, D_MODEL), row),
        out_shape=jax.ShapeDtypeStruct((m, D_MODEL), F32),
        compiler_params=_cparams(("arbitrary",)),
        name="merge",
    )(x, g, *ys, wg, wb, wo)


def _ffn_body(x_ref, g_ref, h0g_ref, h0u_ref, wug_ref, wuu_ref, cwg_ref, cwu_ref, cbg_ref, cbu_ref, wd_ref,
              o_ref, hg_ref, hu_ref, h2, acc, wkg, wku, histg, histu, *, s, tm, hr):
    j = pl.program_id(1)
    c = pl.program_id(2)
    last_c = pl.num_programs(2) - 1

    @pl.when(c == 0)
    def _():
        h2[...] = _rms(x_ref[...], g_ref[...]).astype(BF16)
        acc[...] = jnp.zeros_like(acc)

    @pl.when(j == 0)
    def _():
        histg[c] = h0g_ref[0]
        histu[c] = h0u_ref[0]

    def conv(wk, hist, wu_ref, cw_ref, cb_ref):
        wk[0:hr, :] = hist[c]
        wk[hr:hr + tm, :] = _dot(h2[...], wu_ref[...])
        out = cb_ref[...]
        for k in range(FFN_CONV):
            off = hr - (FFN_CONV - 1 - k) * s
            out = out + cw_ref[k:k + 1, :] * wk[off:off + tm, :]
        hist[c] = wk[tm:tm + hr, :]
        return out

    yg = conv(wkg, histg, wug_ref, cwg_ref, cbg_ref)
    yu = conv(wku, histu, wuu_ref, cwu_ref, cbu_ref)
    acc[...] += _dot((_silu(yg) * yu).astype(BF16), wd_ref[...])
    hg_ref[0, c] = histg[c]
    hu_ref[0, c] = histu[c]

    @pl.when(c == last_c)
    def _():
        o_ref[...] = x_ref[...] + acc[...]


def _ffn(x, g, h0, wu, cw, cb, wd, *, n_seq, s, tm, cwid):
    m = x.shape[0]
    tiles = m // n_seq // tm
    hr = h0.shape[1]
    nc = D_FF // cwid
    row = lambda n, j, c: (n * tiles + j, 0)
    hist_g = lambda n, j, c: (n, 0, c)
    hist_u = lambda n, j, c: (n, 0, nc + c)
    col_g = lambda n, j, c: (0, c)
    col_u = lambda n, j, c: (0, nc + c)
    hist_shape = jax.ShapeDtypeStruct((n_seq, nc, hr, cwid), F32)
    hist_out = pl.BlockSpec((1, nc, hr, cwid), lambda n, j, c: (n, 0, 0, 0))
    once = dict(pipeline_mode=pl.Buffered(1)) if nc == 1 else {}
    x_new, hg, hu = pl.pallas_call(
        functools.partial(_ffn_body, s=s, tm=tm, hr=hr),
        grid=(n_seq, tiles, nc),
        in_specs=[pl.BlockSpec((tm, D_MODEL), row),
                  pl.BlockSpec((1, D_MODEL), lambda n, j, c: (0, 0)),
                  pl.BlockSpec((1, hr, cwid), hist_g), pl.BlockSpec((1, hr, cwid), hist_u),
                  pl.BlockSpec((D_MODEL, cwid), col_g, **once), pl.BlockSpec((D_MODEL, cwid), col_u, **once),
                  pl.BlockSpec((FFN_CONV, cwid), col_g), pl.BlockSpec((FFN_CONV, cwid), col_u),
                  pl.BlockSpec((1, cwid), col_g), pl.BlockSpec((1, cwid), col_u),
                  pl.BlockSpec((cwid, D_MODEL), lambda n, j, c: (c, 0), **once)],
        out_specs=[pl.BlockSpec((tm, D_MODEL), row), hist_out, hist_out],
        out_shape=[jax.ShapeDtypeStruct((m, D_MODEL), F32), hist_shape, hist_shape],
        scratch_shapes=[pltpu.VMEM((tm, D_MODEL), BF16),
                        pltpu.VMEM((tm, D_MODEL), F32),
                        pltpu.VMEM((hr + tm, cwid), F32),
                        pltpu.VMEM((hr + tm, cwid), F32),
                        pltpu.VMEM((nc, hr, cwid), F32),
                        pltpu.VMEM((nc, hr, cwid), F32)],
        compiler_params=_cparams(("arbitrary", "arbitrary", "arbitrary")),
        name="ffn",
    )(x, g, h0, h0, wu, wu, cw, cw, cb, cb, wd)
    to_rows = lambda a: jnp.swapaxes(a, 1, 2).reshape(n_seq, hr, D_FF)
    return x_new, jnp.concatenate([to_rows(hg), to_rows(hu)], axis=2)


def _final_norm_body(x_ref, g_ref, o_ref):
    o_ref[...] = _rms(x_ref[...], g_ref[...])


def _final_norm(x, g, tm):
    m, d = x.shape
    return pl.pallas_call(
        _final_norm_body,
        grid=(m // tm,),
        in_specs=[pl.BlockSpec((tm, d), lambda i: (i, 0)), pl.BlockSpec((1, d), lambda i: (0, 0))],
        out_specs=pl.BlockSpec((tm, d), lambda i: (i, 0)),
        out_shape=jax.ShapeDtypeStruct((m, d), F32),
        compiler_params=_cparams(("arbitrary",)),
        name="final_norm",
    )(x, g)


def _selection_matrices():
    d = FOX_HEAD_DIM
    selq = np.zeros((FOX_HEADS, 2 * d, WIDTH), np.float32)
    selk = np.zeros((FOX_HEADS, WIDTH, 2 * d), np.float32)
    selv = np.zeros((FOX_HEADS, V_ROWS, WIDTH), np.float32)
    for h in range(FOX_HEADS):
        for i in range(d):
            selq[h, i, h * d + i] = 1.0
            selk[h, h * d + i, i] = 1.0
            selv[h, i, h * d + i] = 1.0
    return (jnp.asarray(selq, BF16), jnp.asarray(selk, BF16), jnp.asarray(selv, BF16))


def _layer_params(l, norm1_g, w_in, cm_norm_g, cm_ws, cm_b, pool_w, pool_scale, fox_bf, ssd_conv_w, ssd_conv_b,
                  ssd_dt_bias, ssd_a_log, ssd_d, ssd_norm_g, w_branch, w_out, norm2_g, ffn_up, ffn_conv_w,
                  ffn_conv_b, ffn_down, n_dec, n_tok):
    w = w_in[l]
    o_f = 2 * WIDTH + WIDTH + 3 * WIDTH
    o_z = o_f + FOX_HEADS
    o_xbc = o_z + WIDTH
    o_dt = o_xbc + SSD_CONV_DIM
    o_gate = o_dt + SSD_HEADS
    w_p = jnp.concatenate([w[:, :o_f], w[:, o_z:o_dt], w[:, o_f:o_z], w[:, o_dt:o_gate],
                           jnp.zeros((D_MODEL, SMALL_W - FOX_HEADS - SSD_HEADS), F32)], axis=1).astype(BF16)
    causal = jnp.tril(jnp.ones((CM_CHUNK, CM_CHUNK), F32))
    ws = cm_ws[l] * causal
    gd = WIDTH // CM_GROUPS
    eye = jnp.eye(n_dec, dtype=F32)
    pd = len(POOL_WINDOWS)
    pw = pool_w[l].astype(F32)
    pool_bd = jnp.zeros((WIDTH, WIDTH), F32)
    pg = WIDTH // pd
    for gi in range(pd):
        pool_bd = pool_bd.at[gi * pg:(gi + 1) * pg, gi * pg:(gi + 1) * pg].set(pw[gi])
    lanes = jnp.arange(SMALL_W)
    head_of_lane = jnp.clip(lanes - FOX_HEADS, 0, SSD_HEADS - 1)
    dt_lane = (lanes >= FOX_HEADS) & (lanes < FOX_HEADS + SSD_HEADS)
    a = -jnp.exp(ssd_a_log[l].astype(F32))
    return dict(
        g1=norm1_g[l][None, :],
        w_p=w_p,
        w_gate=w[:, o_gate:].astype(BF16),
        cm_g=cm_norm_g[l][None, :],
        ws_prompt=ws.astype(BF16),
        cmb_prompt=jnp.repeat(cm_b[l].T, gd, axis=1),
        ws_sample=jnp.stack([jnp.kron(ws[gi, :n_tok, :n_tok], eye) for gi in range(CM_GROUPS)]).astype(BF16),
        cmb_sample=jnp.repeat(jnp.repeat(cm_b[l].T[:n_tok], n_dec, axis=0), gd, axis=1),
        pool_bd=pool_bd.astype(BF16),
        pool_scale=pool_scale[l][None, :].astype(F32),
        bf_row=jnp.pad(fox_bf[l], (0, SMALL_W - FOX_HEADS))[None, :],
        bf_col=jnp.broadcast_to(jnp.tile(jnp.pad(fox_bf[l], (0, HEAD_PAD - FOX_HEADS)), n_tok)[:, None],
                                (n_tok * HEAD_PAD, PAGE_SIZE)),
        conv_w=ssd_conv_w[l], conv_b=ssd_conv_b[l][None, :],
        dtb_row=jnp.where(dt_lane, ssd_dt_bias[l][head_of_lane], 0.0)[None, :],
        a_row=jnp.where(dt_lane, a[head_of_lane], 0.0)[None, :],
        d_row=jnp.repeat(ssd_d[l].astype(F32), SSD_HEAD_DIM)[None, :],
        dtb_bh=jnp.tile(ssd_dt_bias[l], n_dec)[:, None, None],
        a_bh=jnp.tile(a, n_dec)[:, None, None],
        ssd_g=ssd_norm_g[l][None, :],
        w_branch=w_branch[l].astype(BF16),
        w_out=w_out[l].astype(BF16),
        g2=norm2_g[l][None, :],
        ffn_up=ffn_up[l].astype(BF16),
        ffn_cw=ffn_conv_w[l], ffn_cb=ffn_conv_b[l][None, :],
        ffn_down=ffn_down[l].astype(BF16),
    )


def _prompt_layer(x, w, consts, n_seq):
    m = x.shape[0]
    seq_len = m // n_seq
    selq, selk, selv = consts["sel"]
    big = 2048 if seq_len % 2048 == 0 else 512
    mid = 1024 if seq_len % 1024 == 0 else 512
    p = _proj(x, w["g1"], w["w_p"], tm=mid)
    y_a, = _mixa(p, w["ws_prompt"], w["cmb_prompt"], w["cm_g"], t=CM_CHUNK, tm=big, with_va=False)
    y_b = _pool(p, U_POOL, jnp.zeros((n_seq, POOL_PAD_STEPS, WIDTH), F32), w["pool_bd"], w["pool_scale"],
                n_seq=n_seq, s=1, tm=big, pos0=0)
    lf, qa, ka, va = _fox_prep(p, w["bf_row"], selq, selk, selv, n_seq=n_seq, tm=mid)
    tq = mid
    y_c = _fox_flash(qa, ka, va, tq=tq, wq=512)
    y_d, ssd_last = _ssd_prompt(p, w["conv_w"], w["conv_b"], w["dtb_row"], w["a_row"], w["d_row"], w["ssd_g"],
                                n_seq=n_seq)
    x = _merge(x, w["g1"], (y_a, y_b, y_c, y_d), w["w_gate"], w["w_branch"], w["w_out"], tm=512)
    x, ffn_hist = _ffn(x, w["g2"], jnp.zeros((n_seq, 8, 2 * D_FF), F32), w["ffn_up"], w["ffn_cw"],
                       w["ffn_cb"], w["ffn_down"], n_seq=n_seq, s=1, tm=512, cwid=D_FF)
    p3 = p.reshape(n_seq, seq_len, P_COLS)
    unit = lambda u: p3[:, :, u * WIDTH:(u + 1) * WIDTH]
    state = dict(
        k=unit(U_K).reshape(n_seq, seq_len, FOX_HEADS, FOX_HEAD_DIM),
        v=unit(U_V).reshape(n_seq, seq_len, FOX_HEADS, FOX_HEAD_DIM),
        logf=lf.reshape(n_seq, seq_len, SMALL_W)[:, :, :FOX_HEADS],
        pool=unit(U_POOL)[:, -POOL_HIST:],
        ssd_conv=p3[:, -(SSD_CONV - 1):, U_XS * WIDTH:U_XS * WIDTH + SSD_CONV_DIM],
        ssd=ssd_last.reshape(n_seq, SSD_HEADS, SSD_HEAD_DIM, SSD_STATE),
        ffn_conv=ffn_hist[:, -(FFN_CONV - 1):],
    )
    return x, state


def _to_time_major(a):
    return jnp.swapaxes(a, 0, 1).reshape((a.shape[0] * a.shape[1],) + a.shape[2:])


def _from_time_major(a, n_b):
    return jnp.swapaxes(a.reshape((a.shape[0] // n_b, n_b) + a.shape[1:]), 0, 1)


def _sample_layer(x, w, consts, layer, past, page_table, n_b, past_len):
    m = x.shape[0]
    n_tok = m // n_b
    p = _proj(x, w["g1"], w["w_p"], tm=m)
    unit = lambda u: p[:, u * WIDTH:(u + 1) * WIDTH]
    small = p[:, 10 * WIDTH:]

    y_a, va = _mixa(p, w["ws_sample"], w["cmb_sample"], w["cm_g"], t=m, tm=m, with_va=True)

    pool_new = unit(U_POOL)
    pool_hist = jnp.concatenate([jnp.zeros(((POOL_PAD_STEPS - POOL_HIST) * n_b, WIDTH), F32),
                                 _to_time_major(past["pool"])], axis=0)
    y_b = _pool(pool_new, 0, pool_hist[None], w["pool_bd"], w["pool_scale"],
                n_seq=1, s=n_b, tm=m, pos0=past_len)
    new_pool = jnp.concatenate([past["pool"], _from_time_major(pool_new, n_b)], axis=1)[:, -POOL_HIST:]

    pad_tok = lambda a: jnp.pad(_from_time_major(a, n_b), ((0, 0), (0, TOK_PAD - n_tok), (0, 0)))
    f_raw = _from_time_major(small[:, :FOX_HEADS], n_b)
    fr_lane = jnp.pad(jnp.swapaxes(f_raw, 1, 2),
                      ((0, 0), (0, HEAD_PAD - FOX_HEADS), (0, PAGE_SIZE - n_tok)))
    fr_lane = jnp.tile(fr_lane, (1, n_tok, 1))
    y_c = _fox_decode(layer, page_table, _from_time_major(unit(U_Q), n_b), pad_tok(unit(U_K)),
                      pad_tok(unit(U_V)), fr_lane, w["bf_col"], past["cache_k"], past["cache_v"],
                      past["cache_lft"])
    y_c = _to_time_major(y_c)
    logf = _log_sigmoid_rows(small, w["bf_row"])[:, :FOX_HEADS]

    xbc_new = p[:, U_XS * WIDTH:U_XS * WIDTH + SSD_CONV_DIM]
    xbc_ext = jnp.concatenate([_to_time_major(past["ssd_conv"]), xbc_new], axis=0)
    act = _conv_silu(xbc_ext, w["conv_w"], w["conv_b"], s=n_b, rows=m)
    xs = act[:, :WIDTH]
    rep_heads = lambda a: jnp.repeat(_from_time_major(a, n_b).reshape(n_b, n_tok, SSD_GROUPS, SSD_STATE),
                                     SSD_HEADS // SSD_GROUPS, axis=2)
    to_bh = lambda a: jnp.transpose(a, (0, 2, 1, 3)).reshape(n_b * SSD_HEADS, n_tok, SSD_STATE)
    x_bh = jnp.transpose(_from_time_major(xs, n_b).reshape(n_b, n_tok, SSD_HEADS, SSD_HEAD_DIM),
                         (0, 2, 3, 1)).reshape(n_b * SSD_HEADS, SSD_HEAD_DIM, n_tok)
    dtr = _from_time_major(small[:, FOX_HEADS:FOX_HEADS + SSD_HEADS], n_b)
    dtr_bh = jnp.swapaxes(dtr, 1, 2).reshape(n_b * SSD_HEADS, 1, n_tok)
    h0 = past["ssd"].reshape(n_b * SSD_HEADS, SSD_HEAD_DIM, SSD_STATE)
    y_bh, h_last = _ssd_step(x_bh, to_bh(rep_heads(act[:, WIDTH:2 * WIDTH])),
                             to_bh(rep_heads(act[:, 2 * WIDTH:])), dtr_bh, w["dtb_bh"], w["a_bh"], h0, blk=32)
    y_s = _to_time_major(jnp.transpose(y_bh.reshape(n_b, SSD_HEADS, SSD_HEAD_DIM, n_tok),
                                       (0, 3, 1, 2)).reshape(n_b, n_tok, WIDTH))
    y_d = _ssd_gate(y_s, xs, unit(U_Z), w["d_row"], w["ssd_g"])

    x = _merge(x, w["g1"], (y_a, y_b, y_c, y_d), w["w_gate"], w["w_branch"], w["w_out"], tm=min(m, 256))
    ffn_h0 = _to_time_major(past["ffn_conv"])[None]
    x, ffn_hist = _ffn(x, w["g2"], ffn_h0, w["ffn_up"], w["ffn_cw"], w["ffn_cb"], w["ffn_down"],
                       n_seq=1, s=n_b, tm=m, cwid=D_FF // 2)
    state = dict(
        k=_from_time_major(unit(U_K), n_b).reshape(n_b, n_tok, FOX_HEADS, FOX_HEAD_DIM),
        v=_from_time_major(unit(U_V), n_b).reshape(n_b, n_tok, FOX_HEADS, FOX_HEAD_DIM),
        logf=_from_time_major(logf, n_b),
        chunk_v=_from_time_major(va, n_b),
        pool=new_pool,
        ssd_conv=_from_time_major(xbc_ext[-(SSD_CONV - 1) * n_b:], n_b),
        ssd=h_last.reshape(n_b, SSD_HEADS, SSD_HEAD_DIM, SSD_STATE),
        ffn_conv=_from_time_major(ffn_hist[0], n_b),
    )
    return x, state


def _log_sigmoid_body(x_ref, b_ref, o_ref):
    o_ref[...] = _log_sigmoid(x_ref[...] + b_ref[...])


def _log_sigmoid_rows(small, bf_row):
    return pl.pallas_call(
        _log_sigmoid_body,
        out_shape=jax.ShapeDtypeStruct(small.shape, F32),
        name="log_forget",
    )(small, bf_row)


def kernel(x_prompt, x_sample, cache_k, cache_v, cache_logf, page_table, state_pool, state_ssd_conv, state_ssd,
           state_ffn_conv, norm1_g, w_in, cm_norm_g, cm_ws, cm_b, pool_w, pool_scale, fox_bf, ssd_conv_w,
           ssd_conv_b, ssd_dt_bias, ssd_a_log, ssd_d, ssd_norm_g, w_branch, w_out, norm2_g, ffn_up, ffn_conv_w,
           ffn_conv_b, ffn_down, final_norm_g):
    n_seq, seq_len, _ = x_prompt.shape
    n_b, n_tok, _ = x_sample.shape
    depth = w_in.shape[0]
    n_pages = page_table.shape[1]
    past_len = n_pages * cache_k.shape[2]
    n_phys = cache_k.shape[1]

    consts = dict(sel=_selection_matrices())
    ck = jnp.transpose(cache_k, (0, 1, 3, 4, 2)).reshape(depth, n_phys, WIDTH, PAGE_SIZE)
    cv = jnp.transpose(cache_v, (0, 1, 3, 4, 2)).reshape(depth, n_phys, WIDTH, PAGE_SIZE)
    clt = jnp.pad(jnp.swapaxes(cache_logf, 2, 3), ((0, 0), (0, 0), (0, HEAD_PAD - FOX_HEADS), (0, 0)))

    xp = x_prompt.reshape(n_seq * seq_len, D_MODEL)
    xs = _to_time_major(x_sample)
    st_p, st_s = [], []
    for l in range(depth):
        w = _layer_params(l, norm1_g, w_in, cm_norm_g, cm_ws, cm_b, pool_w, pool_scale, fox_bf, ssd_conv_w,
                          ssd_conv_b, ssd_dt_bias, ssd_a_log, ssd_d, ssd_norm_g, w_branch, w_out, norm2_g,
                          ffn_up, ffn_conv_w, ffn_conv_b, ffn_down, n_b, n_tok)
        past = dict(cache_k=ck, cache_v=cv, cache_lft=clt, pool=state_pool[l], ssd_conv=state_ssd_conv[l],
                    ssd=state_ssd[l], ffn_conv=state_ffn_conv[l])
        xp, sp = _prompt_layer(xp, w, consts, n_seq)
        xs, ss = _sample_layer(xs, w, consts, l, past, page_table, n_b, past_len)
        st_p.append(sp)
        st_s.append(ss)
    gfin = final_norm_g[None, :]
    y_prompt = _final_norm(xp, gfin, tm=2048 if xp.shape[0] % 2048 == 0 else 512).reshape(n_seq, seq_len, D_MODEL)
    y_sample = _from_time_major(_final_norm(xs, gfin, tm=xs.shape[0]), n_b)

    def stacked(states, name):
        return jnp.stack([s[name] for s in states], axis=0)

    return (y_prompt, y_sample,
            stacked(st_p, "k"), stacked(st_p, "v"), stacked(st_p, "logf"),
            stacked(st_s, "k"), stacked(st_s, "v"), stacked(st_s, "logf"),
            stacked(st_s, "chunk_v"),
            stacked(st_p, "pool"), stacked(st_s, "pool"),
            stacked(st_p, "ssd_conv"), stacked(st_s, "ssd_conv"),
            stacked(st_p, "ssd"), stacked(st_s, "ssd"),
            stacked(st_p, "ffn_conv"), stacked(st_s, "ffn_conv"))
```

```python
import functools
import math

import numpy as np
import jax
import jax.numpy as jnp
from jax import lax
from jax.experimental import pallas as pl
from jax.experimental.pallas import tpu as pltpu

F32 = jnp.float32
BF16 = jnp.bfloat16

D_MODEL = 1024
WIDTH = 256
PAGE_SIZE = 128
CM_GROUPS = 4
CM_CHUNK = 128
POOL_WINDOWS = (2, 4, 8, 16)
POOL_HIST = 15
POOL_PAD_STEPS = 32
FOX_HEADS = 4
FOX_HEAD_DIM = 64
SSD_HEADS = 4
SSD_HEAD_DIM = 64
SSD_GROUPS = 2
SSD_STATE = 128
SSD_CONV = 4
SSD_CHUNK = 128
SSD_CONV_DIM = 768
D_FF = 2816
FFN_CONV = 3
N_BRANCH = 4
EPS = 1e-6
SMALL_W = 128
P_COLS = 10 * WIDTH + SMALL_W
SMALL_BLK = 10 * WIDTH // SMALL_W
U_AU, U_AV, U_POOL, U_Q, U_K, U_V, U_Z, U_XS, U_BM, U_CM = range(10)

VMEM_LIMIT = 56 * 1024 * 1024


def _cparams(sem):
    return pltpu.CompilerParams(dimension_semantics=sem, vmem_limit_bytes=VMEM_LIMIT)


def _sigmoid(x):
    return 1.0 / (1.0 + jnp.exp(-x))


def _silu(x):
    return x * _sigmoid(x)


def _softplus(x):
    return jnp.maximum(x, 0.0) + jnp.log(1.0 + jnp.exp(-jnp.abs(x)))


def _log_sigmoid(x):
    return -_softplus(-x)


def _gelu_tanh(x):
    return 0.5 * x * (1.0 + jnp.tanh(math.sqrt(2.0 / math.pi) * (x + 0.044715 * (x * x * x))))


def _rms(x, g):
    return x * lax.rsqrt(jnp.mean(x * x, axis=-1, keepdims=True) + EPS) * g


def _split3(x):
    hi = x.astype(BF16)
    r1 = x - hi.astype(F32)
    mid = r1.astype(BF16)
    lo = (r1 - mid.astype(F32)).astype(BF16)
    return hi, mid, lo


def _dot(a, b):
    return jnp.dot(a, b, preferred_element_type=F32)


def _dot_nt(a, b):
    return lax.dot_general(a, b, (((1,), (1,)), ((), ())), preferred_element_type=F32)


def _ones_dot_left(ones_bf16, x):
    hi, mid, lo = _split3(x)
    return _dot(ones_bf16, hi) + _dot(ones_bf16, mid) + _dot(ones_bf16, lo)


def _ones_dot_right(x, ones_bf16):
    hi, mid, lo = _split3(x)
    return _dot(hi, ones_bf16) + _dot(mid, ones_bf16) + _dot(lo, ones_bf16)


def _iota(shape, dim):
    return lax.broadcasted_iota(jnp.int32, shape, dim)


def _proj_body(x_ref, g_ref, w_ref, o_ref):
    h = _rms(x_ref[...], g_ref[...]).astype(BF16)
    o_ref[...] = _dot(h, w_ref[...])


def _proj(x, g, w, tm):
    m, d = x.shape
    n = w.shape[1]
    return pl.pallas_call(
        _proj_body,
        grid=(m // tm,),
        in_specs=[pl.BlockSpec((tm, d), lambda i: (i, 0)),
                  pl.BlockSpec((1, d), lambda i: (0, 0)),
                  pl.BlockSpec((d, n), lambda i: (0, 0))],
        out_specs=pl.BlockSpec((tm, n), lambda i: (i, 0)),
        out_shape=jax.ShapeDtypeStruct((m, n), F32),
        compiler_params=_cparams(("arbitrary",)),
        name="proj",
    )(x, g, w)


def _mixa_body(u_ref, v_ref, wm_ref, b_ref, g_ref, y_ref, *maybe_va_ref, t, n_chunks):
    lane_grp = _iota((t, WIDTH), 1) >> 6
    for c in range(n_chunks):
        rows = pl.ds(c * t, t)
        u = _gelu_tanh(u_ref[rows, :])
        va = _rms(_gelu_tanh(v_ref[rows, :]), g_ref[...])
        for va_ref in maybe_va_ref:
            va_ref[rows, :] = va
        vab = va.astype(BF16)
        zero = jnp.zeros_like(vab)
        s = b_ref[...]
        for gi in range(CM_GROUPS):
            s = s + _dot(wm_ref[gi], jnp.where(lane_grp == gi, vab, zero))
        y_ref[rows, :] = u * s


def _mixa(p, wm, bias, g, t, tm, with_va):
    m = p.shape[0]
    n_out = 2 if with_va else 1
    body = functools.partial(_mixa_body, t=t, n_chunks=tm // t)
    return pl.pallas_call(
        body,
        grid=(m // tm,),
        in_specs=[pl.BlockSpec((tm, WIDTH), lambda i: (i, U_AU)),
                  pl.BlockSpec((tm, WIDTH), lambda i: (i, U_AV)),
                  pl.BlockSpec((CM_GROUPS, t, t), lambda i: (0, 0, 0)),
                  pl.BlockSpec((t, WIDTH), lambda i: (0, 0)),
                  pl.BlockSpec((1, WIDTH), lambda i: (0, 0))],
        out_specs=[pl.BlockSpec((tm, WIDTH), lambda i: (i, 0))] * n_out,
        out_shape=[jax.ShapeDtypeStruct((m, WIDTH), F32)] * n_out,
        compiler_params=_cparams(("arbitrary",)),
        name="mixer_a",
    )(p, p, wm, bias, g)


def _pool_body(seq_ref, h0_ref, w_ref, scale_ref, y_ref, e1, e2, e4, e8, *, s, tm, pos0, steps_per_tile):
    j = pl.program_id(1)
    hr = POOL_PAD_STEPS * s
    tot = hr + tm

    @pl.when(j == 0)
    def _():
        e1[0:hr, :] = h0_ref[0]

    e1[hr:tot, :] = seq_ref[...]
    e2[8 * s:tot, :] = e1[8 * s:tot, :] + e1[7 * s:tot - s, :]
    e4[16 * s:tot, :] = e2[16 * s:tot, :] + e2[14 * s:tot - 2 * s, :]
    e8[24 * s:tot, :] = e4[24 * s:tot, :] + e4[20 * s:tot - 4 * s, :]
    cur = e1[hr:tot, :]
    s2 = e2[hr:tot, :]
    s4 = e4[hr:tot, :]
    s8 = e8[hr:tot, :]
    s16 = s8 + e8[hr - 8 * s:tot - 8 * s, :]
    lane_grp = _iota((tm, WIDTH), 1) >> 6
    pos = pos0 + j * steps_per_tile + (_iota((tm, WIDTH), 0) >> (s.bit_length() - 1))
    wsum = jnp.where(lane_grp == 0, s2, jnp.where(lane_grp == 1, s4, jnp.where(lane_grp == 2, s8, s16)))
    win = jnp.where(lane_grp == 0, 2, jnp.where(lane_grp == 1, 4, jnp.where(lane_grp == 2, 8, 16)))
    count = jnp.minimum(pos + 1, win).astype(F32)
    d = wsum / count - cur
    y_ref[...] = _dot(d.astype(BF16), w_ref[...]) * scale_ref[...]
    e1[0:hr, :] = e1[tm:tot, :]


def _pool(seq, col_blk, h0, w_bd, scale, *, n_seq, s, tm, pos0):
    m = seq.shape[0]
    tiles = m // n_seq // tm
    hr = POOL_PAD_STEPS * s
    body = functools.partial(_pool_body, s=s, tm=tm, pos0=pos0, steps_per_tile=tm // s)
    return pl.pallas_call(
        body,
        grid=(n_seq, tiles),
        in_specs=[pl.BlockSpec((tm, WIDTH), lambda n, j: (n * tiles + j, col_blk)),
                  pl.BlockSpec((1, hr, WIDTH), lambda n, j: (n, 0, 0)),
                  pl.BlockSpec((WIDTH, WIDTH), lambda n, j: (0, 0)),
                  pl.BlockSpec((1, WIDTH), lambda n, j: (0, 0))],
        out_specs=pl.BlockSpec((tm, WIDTH), lambda n, j: (n * tiles + j, 0)),
        out_shape=jax.ShapeDtypeStruct((m, WIDTH), F32),
        scratch_shapes=[pltpu.VMEM((hr + tm, WIDTH), F32) for _ in range(4)],
        compiler_params=_cparams(("arbitrary", "arbitrary")),
        name="mixer_b",
    )(seq, h0, w_bd, scale)


LOG2E = math.log2(math.e)


V_ROWS = 2 * FOX_HEAD_DIM


def _fox_prep_body(q_ref, k_ref, v_ref, sm_ref, bf_ref, selq_ref, selk_ref, selv_ref,
                   lf_ref, qa_ref, ka_ref, va_ref, carry, *, tm):
    j = pl.program_id(1)

    @pl.when(j == 0)
    def _():
        carry[...] = jnp.zeros_like(carry)

    logf = _log_sigmoid(sm_ref[...] + bf_ref[...])
    lf_ref[...] = logf
    tril = (_iota((tm, tm), 0) >= _iota((tm, tm), 1)).astype(BF16)
    c = _ones_dot_left(tril, logf) + carry[...]
    carry[...] = c[tm - 1:tm, :]
    c = c * LOG2E
    c_t = jnp.transpose(c)
    qb = (q_ref[...] * (FOX_HEAD_DIM ** -0.5 * LOG2E)).astype(BF16)
    kb = k_ref[...].astype(BF16)
    vb = v_ref[...].astype(BF16)
    d = FOX_HEAD_DIM
    lane = _iota((tm, 2 * d), 1)
    row = _iota((2 * d, tm), 0)

    def pieces(x):
        hi = x.astype(BF16).astype(F32)
        r1 = x - hi
        mid = r1.astype(BF16).astype(F32)
        return hi, mid, (r1 - mid).astype(BF16).astype(F32)

    for h in range(FOX_HEADS):
        hi, mid, lo = pieces(jnp.broadcast_to(c_t[h:h + 1, :], (2 * d, tm)))
        q_extra = jnp.where(row == d, hi, jnp.where(row == d + 1, mid, jnp.where(row == d + 2, lo,
                  jnp.where((row >= d + 3) & (row < d + 6), 1.0, 0.0))))
        qa_ref[0, h] = (_dot_nt(selq_ref[h], qb) + q_extra).astype(BF16)
        hi, mid, lo = pieces(jnp.broadcast_to(c[:, h:h + 1], (tm, 2 * d)))
        k_extra = jnp.where(lane == d + 3, -hi, jnp.where(lane == d + 4, -mid, jnp.where(lane == d + 5, -lo,
                  jnp.where((lane >= d) & (lane < d + 3), 1.0, 0.0))))
        ka_ref[0, h] = (_dot(kb, selk_ref[h]) + k_extra).astype(BF16)
        ones_row = jnp.where(row[:V_ROWS] == d, 1.0, 0.0)
        va_ref[0, h] = (_dot_nt(selv_ref[h], vb) + ones_row).astype(BF16)


def _fox_prep(p, bf_row, selq, selk, selv, *, n_seq, tm):
    m = p.shape[0]
    seq_len = m // n_seq
    tiles = seq_len // tm
    aug = jax.ShapeDtypeStruct((n_seq, FOX_HEADS, seq_len, 2 * FOX_HEAD_DIM), BF16)
    aug_spec = pl.BlockSpec((1, FOX_HEADS, tm, 2 * FOX_HEAD_DIM), lambda n, j: (n, 0, j, 0))
    def transposed(rows):
        return (jax.ShapeDtypeStruct((n_seq, FOX_HEADS, rows, seq_len), BF16),
                pl.BlockSpec((1, FOX_HEADS, rows, tm), lambda n, j: (n, 0, 0, j)),
                pl.BlockSpec((FOX_HEADS, rows, WIDTH), lambda n, j: (0, 0, 0)))

    aug_t, aug_t_spec, selq_spec = transposed(2 * FOX_HEAD_DIM)
    v_t, v_t_spec, selv_spec = transposed(V_ROWS)
    sel_spec = pl.BlockSpec((FOX_HEADS, WIDTH, 2 * FOX_HEAD_DIM), lambda n, j: (0, 0, 0))
    return pl.pallas_call(
        functools.partial(_fox_prep_body, tm=tm),
        grid=(n_seq, tiles),
        in_specs=[pl.BlockSpec((tm, WIDTH), lambda n, j: (n * tiles + j, U_Q)),
                  pl.BlockSpec((tm, WIDTH), lambda n, j: (n * tiles + j, U_K)),
                  pl.BlockSpec((tm, WIDTH), lambda n, j: (n * tiles + j, U_V)),
                  pl.BlockSpec((tm, SMALL_W), lambda n, j: (n * tiles + j, SMALL_BLK)),
                  pl.BlockSpec((1, SMALL_W), lambda n, j: (0, 0)),
                  selq_spec, sel_spec, selv_spec],
        out_specs=[pl.BlockSpec((tm, SMALL_W), lambda n, j: (n * tiles + j, 0)),
                   aug_t_spec, aug_spec, v_t_spec],
        out_shape=[jax.ShapeDtypeStruct((m, SMALL_W), F32), aug_t, aug, v_t],
        scratch_shapes=[pltpu.VMEM((1, SMALL_W), F32)],
        compiler_params=_cparams(("arbitrary", "arbitrary")),
        name="fox_prep",
    )(p, p, p, p, bf_row, selq, selk, selv)


def _fox_flash_body(qi_ref, kj_ref, q_ref, k_ref, v_ref, o_ref, m_ref, acc_ref, *, tq, wq):
    step = pl.program_id(1)
    qi = qi_ref[step]
    kj = kj_ref[step]
    d = FOX_HEAD_DIM

    @pl.when(kj == 0)
    def _():
        m_ref[...] = jnp.full_like(m_ref, -jnp.inf)
        acc_ref[...] = jnp.zeros_like(acc_ref)

    n_split = tq // wq
    chains = [(h, c) for h in range(FOX_HEADS) for c in range(n_split)]

    def update(diagonal):
        n_keys = [(c + 1) * wq if diagonal else tq for _, c in chains]
        scores = [_dot(k_ref[0, h, 0:nk, :], q_ref[0, h, :, c * wq:(c + 1) * wq])
                  for (h, c), nk in zip(chains, n_keys)]
        for i, ((h, c), nk, s) in enumerate(zip(chains, n_keys, scores)):
            if diagonal:
                s = jnp.where(_iota((nk, wq), 0) <= _iota((nk, wq), 1) + c * wq, s, -jnp.inf)
            m_prev = m_ref[i]
            m_new = jnp.maximum(m_prev, jnp.max(s, axis=0, keepdims=True))
            alpha = jnp.exp2(m_prev - m_new)
            pexp = jnp.exp2(s - m_new).astype(BF16)
            acc_ref[i] = alpha * acc_ref[i] + _dot(v_ref[0, h, :, 0:nk], pexp)
            m_ref[i] = m_new

    @pl.when(kj < qi)
    def _():
        update(False)

    @pl.when(kj == qi)
    def _():
        update(True)
        o = []
        for h in range(FOX_HEADS):
            parts = []
            for c in range(n_split):
                acc = acc_ref[h * n_split + c]
                parts.append(acc[0:d, :] / acc[d:d + 1, :])
            o.append(jnp.concatenate(parts, axis=1))
        o_ref[...] = jnp.concatenate([jnp.transpose(jnp.concatenate(o[0:2], axis=0)),
                                      jnp.transpose(jnp.concatenate(o[2:4], axis=0))], axis=1)


def _fox_flash(qa_t, ka, va_t, *, tq, wq):
    n_seq, _, seq_len, w = ka.shape
    nq = seq_len // tq
    qi = np.array([i for i in range(nq) for _ in range(i + 1)], np.int32)
    kj = np.array([j for i in range(nq) for j in range(i + 1)], np.int32)
    q_spec = pl.BlockSpec((1, FOX_HEADS, w, tq), lambda n, s, qi, kj: (n, 0, 0, qi[s]))
    k_spec = pl.BlockSpec((1, FOX_HEADS, tq, w), lambda n, s, qi, kj: (n, 0, kj[s], 0))
    v_spec = pl.BlockSpec((1, FOX_HEADS, V_ROWS, tq), lambda n, s, qi, kj: (n, 0, 0, kj[s]))
    grid_spec = pltpu.PrefetchScalarGridSpec(
        num_scalar_prefetch=2,
        grid=(n_seq, len(qi)),
        in_specs=[q_spec, k_spec, v_spec],
        out_specs=pl.BlockSpec((tq, WIDTH), lambda n, s, qi, kj: (n * nq + qi[s], 0)),
        scratch_shapes=[pltpu.VMEM((FOX_HEADS * (tq // wq), 1, wq), F32),
                        pltpu.VMEM((FOX_HEADS * (tq // wq), V_ROWS, wq), F32)],
    )
    return pl.pallas_call(
        functools.partial(_fox_flash_body, tq=tq, wq=wq),
        grid_spec=grid_spec,
        out_shape=jax.ShapeDtypeStruct((n_seq * seq_len, WIDTH), F32),
        compiler_params=_cparams(("arbitrary", "arbitrary")),
        name="fox_flash",
    )(jnp.asarray(qi), jnp.asarray(kj), qa_t, ka, va_t)


TOK_PAD = 8
HEAD_PAD = 8


DECODE_SEQS = 2
DECODE_SLOTS = 2 * DECODE_SEQS


def _fox_decode_body(pt_ref, q_ref, kn_ref, vn_ref, fr_ref, bfc_ref, ck_hbm, cv_hbm, cl_hbm, o_ref,
                     kbuf, vbuf, lbuf, sems, *, layer, n_seq, n_pages, n_tok):
    b = pl.program_id(0)

    def page_copies(page, slot, pg):
        return [pltpu.make_async_copy(src.at[layer, page], dst.at[slot, pg], sems.at[slot])
                for src, dst in ((ck_hbm, kbuf), (cv_hbm, vbuf), (cl_hbm, lbuf))]

    def fetch(seq):
        @pl.when(seq < n_seq)
        def _():
            for pg in range(n_pages):
                for cp in page_copies(pt_ref[seq, pg], seq % DECODE_SLOTS, pg):
                    cp.start()

    @pl.when(b == 0)
    def _():
        for seq in range(DECODE_SLOTS):
            fetch(seq)

    seqs = [b * DECODE_SEQS + u for u in range(DECODE_SEQS)]
    for seq in seqs:
        for pg in range(n_pages):
            for cp in page_copies(0, seq % DECODE_SLOTS, pg):
                cp.wait()
    for u, seq in enumerate(seqs):
        slot = seq % DECODE_SLOTS
        o_ref[u] = _fox_decode_one(q_ref[u], kn_ref[u], vn_ref[u], fr_ref[u], bfc_ref[...],
                                   [kbuf.at[slot, pg] for pg in range(n_pages)],
                                   [vbuf.at[slot, pg] for pg in range(n_pages)],
                                   [lbuf.at[slot, pg] for pg in range(n_pages)], n_tok=n_tok)
    for seq in seqs:
        fetch(seq + DECODE_SLOTS)


def _fox_decode_one(q, k_new8, v_new8, fr, bfc, k_refs, v_refs, lf_refs, *, n_tok):
    n_pages = len(k_refs)
    d = FOX_HEAD_DIM
    ps = PAGE_SIZE
    qrows = n_tok * HEAD_PAD

    q4 = q * (d ** -0.5)
    own = (_iota((HEAD_PAD, WIDTH), 1) >> 6) == _iota((HEAD_PAD, WIDTH), 0)
    qbd = jnp.concatenate([jnp.where(own, jnp.broadcast_to(q4[t:t + 1, :], (HEAD_PAD, WIDTH)), 0.0)
                           for t in range(n_tok)], axis=0).astype(BF16)

    triu = (_iota((ps, ps), 0) <= _iota((ps, ps), 1)).astype(BF16)
    ones = jnp.ones((ps, ps), BF16)
    lf_new = _log_sigmoid(fr + bfc)
    newcum_lane = _ones_dot_right(lf_new, triu)
    row_t = _iota((qrows, ps), 0) >> 3
    lane_i = _iota((qrows, ps), 1)
    newcum_col = jnp.sum(jnp.where(lane_i == row_t, newcum_lane, 0.0), axis=1, keepdims=True)

    lst = jnp.concatenate([r[...] for r in lf_refs], axis=0)
    incl = _ones_dot_right(lst, triu)
    tot = _ones_dot_right(lst, ones)
    later = jnp.zeros((HEAD_PAD, ps), F32)
    bias = [None] * n_pages
    for pg in reversed(range(n_pages)):
        rows = slice(pg * HEAD_PAD, (pg + 1) * HEAD_PAD)
        g = (tot[rows] - incl[rows]) + later
        bias[pg] = jnp.concatenate([g] * n_tok, axis=0) + newcum_col
        later = later + tot[rows]

    k_new = jnp.concatenate([k_new8, jnp.zeros((ps - TOK_PAD, WIDTH), F32)], axis=0).astype(BF16)
    v_new = jnp.concatenate([v_new8, jnp.zeros((ps - TOK_PAD, WIDTH), F32)], axis=0).astype(BF16)
    s_new = _dot_nt(qbd, k_new) + newcum_col - newcum_lane
    s_new = jnp.where(lane_i <= row_t, s_new, -jnp.inf)
    s_past = [_dot(qbd, k_refs[pg][...].astype(BF16)) + bias[pg] for pg in range(n_pages)]

    m_lane = s_new
    for sp in s_past:
        m_lane = jnp.maximum(m_lane, sp)
    m = jnp.max(m_lane, axis=1, keepdims=True)
    p_new = jnp.exp(s_new - m)
    l_lane = p_new
    acc = _dot(p_new.astype(BF16), v_new)
    for pg in range(n_pages):
        pp = jnp.exp(s_past[pg] - m)
        l_lane = l_lane + pp
        acc = acc + _dot_nt(pp.astype(BF16), v_refs[pg][...].astype(BF16))
    o = acc / jnp.sum(l_lane, axis=1, keepdims=True)
    keep = jnp.concatenate([own] * n_tok, axis=0)
    return jnp.sum(jnp.where(keep, o, 0.0).reshape(n_tok, HEAD_PAD, WIDTH), axis=1)


def _fox_decode(layer, page_table, q, kn8, vn8, fr_lane, bf_col, cache_k, cache_v, cache_lft):
    n_b, n_pages = page_table.shape
    n_tok = q.shape[1]
    qrows = n_tok * HEAD_PAD
    ps = PAGE_SIZE

    assert n_b % DECODE_SEQS == 0
    per_seq = lambda rows, cols: pl.BlockSpec((DECODE_SEQS, rows, cols), lambda b, pt: (b, 0, 0))
    hbm = pl.BlockSpec(memory_space=pl.ANY)
    grid_spec = pltpu.PrefetchScalarGridSpec(
        num_scalar_prefetch=1,
        grid=(n_b // DECODE_SEQS,),
        in_specs=[per_seq(n_tok, WIDTH), per_seq(TOK_PAD, WIDTH), per_seq(TOK_PAD, WIDTH),
                  per_seq(qrows, ps),
                  pl.BlockSpec((qrows, ps), lambda b, pt: (0, 0)),
                  hbm, hbm, hbm],
        out_specs=per_seq(n_tok, WIDTH),
        scratch_shapes=[pltpu.VMEM((DECODE_SLOTS, n_pages, WIDTH, ps), F32),
                        pltpu.VMEM((DECODE_SLOTS, n_pages, WIDTH, ps), F32),
                        pltpu.VMEM((DECODE_SLOTS, n_pages, HEAD_PAD, ps), F32),
                        pltpu.SemaphoreType.DMA((DECODE_SLOTS,))],
    )
    return pl.pallas_call(
        functools.partial(_fox_decode_body, layer=layer, n_seq=n_b, n_pages=n_pages, n_tok=n_tok),
        grid_spec=grid_spec,
        out_shape=jax.ShapeDtypeStruct((n_b, n_tok, WIDTH), F32),
        compiler_params=_cparams(("arbitrary",)),
        name="fox_decode",
    )(page_table, q, kn8, vn8, fr_lane, bf_col, cache_k, cache_v, cache_lft)


def _ssd_prompt_body(xs_ref, bm_ref, cm_ref, z_ref, sm_ref, cw_ref, cb_ref, dtb_ref, arow_ref, drow_ref, g_ref,
                     y_ref, hl_ref, ext, state, *, t, n_seq):
    j = pl.program_id(0)

    @pl.when(j == 0)
    def _():
        ext[:, 0:8, :] = jnp.zeros((n_seq, 8, SSD_CONV_DIM), F32)
        state[...] = jnp.zeros_like(state)

    for s in range(n_seq):
        _ssd_chunk(xs_ref.at[s], bm_ref.at[s], cm_ref.at[s], z_ref.at[s], sm_ref.at[s], cw_ref, cb_ref, dtb_ref,
                   arow_ref, drow_ref, g_ref, y_ref.at[s], ext.at[s], state.at[s], t=t)

    @pl.when(j == pl.num_programs(0) - 1)
    def _():
        hl_ref[...] = state[...]


def _ssd_chunk(xs_ref, bm_ref, cm_ref, z_ref, sm_ref, cw_ref, cb_ref, dtb_ref, arow_ref, drow_ref, g_ref,
               y_ref, ext, state, *, t):
    hd = SSD_HEAD_DIM
    w3 = SSD_CONV_DIM

    ext[8:8 + t, 0:WIDTH] = xs_ref[...]
    ext[8:8 + t, WIDTH:2 * WIDTH] = bm_ref[...]
    ext[8:8 + t, 2 * WIDTH:w3] = cm_ref[...]
    conv = cb_ref[...]
    for k in range(SSD_CONV):
        conv = conv + cw_ref[k:k + 1, :] * ext[5 + k:5 + k + t, :]
    ext[0:8, :] = ext[t:t + 8, :]
    act = _silu(conv)
    xs = act[:, 0:WIDTH]
    bm = act[:, WIDTH:2 * WIDTH].astype(BF16)
    cm = act[:, 2 * WIDTH:w3].astype(BF16)

    dt = _softplus(sm_ref[...] + dtb_ref[...])
    da = dt * arow_ref[...]
    tril = (_iota((t, t), 0) >= _iota((t, t), 1))
    acs = _ones_dot_left(tril.astype(BF16), da)
    acs_t = jnp.transpose(acs)
    lane_head = _iota((t, WIDTH), 1) >> 6
    dt_x = jnp.zeros((t, WIDTH), F32)
    end_x = jnp.zeros((t, WIDTH), F32)
    in_x = jnp.zeros((t, WIDTH), F32)
    dec_rows = []
    for h in range(SSD_HEADS):
        col = acs[:, 4 + h:5 + h]
        lastv = acs[t - 1:t, 4 + h:5 + h]
        dt_x = jnp.where(lane_head == h, dt[:, 4 + h:5 + h], dt_x)
        end_x = jnp.where(lane_head == h, jnp.exp(lastv - col), end_x)
        in_x = jnp.where(lane_head == h, jnp.exp(col), in_x)
        dec_rows.append(jnp.broadcast_to(jnp.exp(lastv), (hd, SSD_STATE)))
    xdt = xs * dt_x
    xdt_b = xdt.astype(BF16)

    y = jnp.zeros((t, WIDTH), F32)
    h_in = state[...]
    xw_t = jnp.transpose(xdt * end_x).astype(BF16)
    new_states = []
    for gi in range(SSD_GROUPS):
        bg = bm[:, gi * SSD_STATE:(gi + 1) * SSD_STATE]
        cg = cm[:, gi * SSD_STATE:(gi + 1) * SSD_STATE]
        cbm = _dot_nt(cg, bg)
        rows = slice(gi * 2 * hd, (gi + 1) * 2 * hd)
        for r in range(SSD_HEADS // SSD_GROUPS):
            h = gi * (SSD_HEADS // SSD_GROUPS) + r
            seg = acs[:, 4 + h:5 + h] - acs_t[4 + h:5 + h, :]
            decay = jnp.exp(jnp.where(tril, seg, -jnp.inf))
            yd = _dot((cbm * decay).astype(BF16), xdt_b)
            y = y + jnp.where(lane_head == h, yd, 0.0)
        y_off = _dot_nt(cg, h_in[rows, :].astype(BF16))
        y_off = jnp.concatenate([y_off, y_off], axis=1)
        grp_lane = _iota((t, WIDTH), 1) >> 7
        y = y + jnp.where(grp_lane == gi, y_off * in_x, 0.0)
        new_states.append(_dot(xw_t[rows, :], bg))
    state[...] = h_in * jnp.concatenate(dec_rows, axis=0) + jnp.concatenate(new_states, axis=0)

    yg = (y + drow_ref[...] * xs) * _silu(z_ref[...])
    y_ref[...] = _rms(yg, g_ref[...])


def _ssd_prompt(p, cw, cb, dtb_row, a_row, d_row, g, *, n_seq):
    m = p.shape[0]
    t = SSD_CHUNK
    seq_len = m // n_seq
    hp = SSD_HEADS * SSD_HEAD_DIM
    p3 = p.reshape(n_seq, seq_len, P_COLS)

    def unit(u):
        return pl.BlockSpec((n_seq, t, WIDTH), lambda j: (0, j, u))

    def const(shape):
        return pl.BlockSpec(shape, lambda j: (0,) * len(shape))

    y, h_last = pl.pallas_call(
        functools.partial(_ssd_prompt_body, t=t, n_seq=n_seq),
        grid=(seq_len // t,),
        in_specs=[unit(U_XS), unit(U_BM), unit(U_CM), unit(U_Z),
                  pl.BlockSpec((n_seq, t, SMALL_W), lambda j: (0, j, SMALL_BLK)),
                  const((SSD_CONV, SSD_CONV_DIM)), const((1, SSD_CONV_DIM)),
                  const((1, SMALL_W)), const((1, SMALL_W)), const((1, WIDTH)), const((1, WIDTH))],
        out_specs=[pl.BlockSpec((n_seq, t, WIDTH), lambda j: (0, j, 0)),
                   const((n_seq, hp, SSD_STATE))],
        out_shape=[jax.ShapeDtypeStruct((n_seq, seq_len, WIDTH), F32),
                   jax.ShapeDtypeStruct((n_seq, hp, SSD_STATE), F32)],
        scratch_shapes=[pltpu.VMEM((n_seq, 8 + t, SSD_CONV_DIM), F32),
                        pltpu.VMEM((n_seq, hp, SSD_STATE), F32)],
        compiler_params=_cparams(("arbitrary",)),
        name="ssd_prompt",
    )(p3, p3, p3, p3, p3, cw, cb, dtb_row, a_row, d_row, g)
    return y.reshape(m, WIDTH), h_last


def _conv_silu_body(ext_ref, w_ref, b_ref, o_ref, *, s, taps, rows):
    acc = b_ref[...]
    for k in range(taps):
        acc = acc + w_ref[k:k + 1, :] * ext_ref[k * s:k * s + rows, :]
    o_ref[...] = _silu(acc)


def _conv_silu(ext, w, b, *, s, rows):
    taps = w.shape[0]
    c = ext.shape[1]
    return pl.pallas_call(
        functools.partial(_conv_silu_body, s=s, taps=taps, rows=rows),
        out_shape=jax.ShapeDtypeStruct((rows, c), F32),
        compiler_params=pltpu.CompilerParams(vmem_limit_bytes=VMEM_LIMIT),
        name="conv_silu",
    )(ext, w, b)


def _ssd_step_body(x_ref, b_ref, c_ref, dtr_ref, dtb_ref, a_ref, h0_ref, y_ref, hl_ref, *, n_tok):
    dt = _softplus(dtr_ref[...] + dtb_ref[...])
    da = dt * a_ref[...]
    h = h0_ref[...]
    lane_t = _iota(y_ref.shape, 2)
    y = jnp.zeros(y_ref.shape, F32)
    for t in range(n_tok):
        xdt = x_ref[:, :, t:t + 1] * dt[:, :, t:t + 1]
        h = h * jnp.exp(da[:, :, t:t + 1]) + xdt * b_ref[:, t:t + 1, :]
        yt = jnp.sum(h * c_ref[:, t:t + 1, :], axis=2, keepdims=True)
        y = jnp.where(lane_t == t, yt, y)
    y_ref[...] = y
    hl_ref[...] = h


def _ssd_step(x_bh, b_bh, c_bh, dtr_bh, dtb_bh, a_bh, h0, *, blk):
    bh, hd, n_tok = x_bh.shape
    ns = h0.shape[2]

    def spec(shape):
        return pl.BlockSpec((blk,) + shape, lambda i: (i, 0, 0))

    return pl.pallas_call(
        functools.partial(_ssd_step_body, n_tok=n_tok),
        grid=(bh // blk,),
        in_specs=[spec((hd, n_tok)), spec((n_tok, ns)), spec((n_tok, ns)), spec((1, n_tok)),
                  spec((1, 1)), spec((1, 1)), spec((hd, ns))],
        out_specs=[spec((hd, n_tok)), spec((hd, ns))],
        out_shape=[jax.ShapeDtypeStruct((bh, hd, n_tok), F32),
                   jax.ShapeDtypeStruct((bh, hd, ns), F32)],
        compiler_params=_cparams(("arbitrary",)),
        name="ssd_step",
    )(x_bh, b_bh, c_bh, dtr_bh, dtb_bh, a_bh, h0)


def _ssd_gate_body(y_ref, xs_ref, z_ref, d_ref, g_ref, o_ref):
    yg = (y_ref[...] + d_ref[...] * xs_ref[...]) * _silu(z_ref[...])
    o_ref[...] = _rms(yg, g_ref[...])


def _ssd_gate(y, xs, z, d_row, g):
    return pl.pallas_call(
        _ssd_gate_body,
        out_shape=jax.ShapeDtypeStruct(y.shape, F32),
        name="ssd_gate",
    )(y, xs, z, d_row, g)


def _merge_body(x_ref, g_ref, g2_ref, ya_ref, yb_ref, yc_ref, yd_ref, wg_ref, wb_ref, wo_ref, o_ref, h2_ref):
    x = x_ref[...]
    h = _rms(x, g_ref[...]).astype(BF16)
    merged = jnp.zeros(x.shape, F32)
    for i, y_ref in enumerate((ya_ref, yb_ref, yc_ref, yd_ref)):
        gate = _sigmoid(_dot(h, wg_ref[:, i * D_MODEL:(i + 1) * D_MODEL]))
        merged = merged + gate * _dot(y_ref[...].astype(BF16), wb_ref[i])
    x_new = x + _dot(merged.astype(BF16), wo_ref[...])
    o_ref[...] = x_new
    h2_ref[...] = _rms(x_new, g2_ref[...]).astype(BF16)


def _merge(x, g, g2, ys, wg, wb, wo, tm):
    m = x.shape[0]
    row = lambda i: (i, 0)
    y_spec = pl.BlockSpec((tm, WIDTH), row)
    vec = pl.BlockSpec((1, D_MODEL), lambda i: (0, 0))
    return pl.pallas_call(
        _merge_body,
        grid=(m // tm,),
        in_specs=[pl.BlockSpec((tm, D_MODEL), row), vec, vec,
                  y_spec, y_spec, y_spec, y_spec,
                  pl.BlockSpec((D_MODEL, N_BRANCH * D_MODEL), lambda i: (0, 0)),
                  pl.BlockSpec((N_BRANCH, WIDTH, D_MODEL), lambda i: (0, 0, 0)),
                  pl.BlockSpec((D_MODEL, D_MODEL), lambda i: (0, 0))],
        out_specs=[pl.BlockSpec((tm, D_MODEL), row), pl.BlockSpec((tm, D_MODEL), row)],
        out_shape=[jax.ShapeDtypeStruct((m, D_MODEL), F32), jax.ShapeDtypeStruct((m, D_MODEL), BF16)],
        compiler_params=_cparams(("arbitrary",)),
        name="merge",
    )(x, g, g2, *ys, wg, wb, wo)


def _ffn_body(x_ref, g_ref, h0g_ref, h0u_ref, wug_ref, wuu_ref, cwg_ref, cwu_ref, cbg_ref, cbu_ref, wd_ref,
              o_ref, hg_ref, hu_ref, h2, acc, wkg, wku, histg, histu, *, s, tm, hr):
    j = pl.program_id(1)
    c = pl.program_id(2)
    last_c = pl.num_programs(2) - 1

    @pl.when(c == 0)
    def _():
        h2[...] = g_ref[...]
        acc[...] = jnp.zeros_like(acc)

    @pl.when(j == 0)
    def _():
        histg[c] = h0g_ref[0]
        histu[c] = h0u_ref[0]

    def conv(wk, hist, wu_ref, cw_ref, cb_ref):
        wk[0:hr, :] = hist[c]
        wk[hr:hr + tm, :] = _dot(h2[...], wu_ref[...])
        out = cb_ref[...]
        for k in range(FFN_CONV):
            off = hr - (FFN_CONV - 1 - k) * s
            out = out + cw_ref[k:k + 1, :] * wk[off:off + tm, :]
        hist[c] = wk[tm:tm + hr, :]
        return out

    yg = conv(wkg, histg, wug_ref, cwg_ref, cbg_ref)
    yu = conv(wku, histu, wuu_ref, cwu_ref, cbu_ref)
    acc[...] += _dot((_silu(yg) * yu).astype(BF16), wd_ref[...])
    hg_ref[0, c] = histg[c]
    hu_ref[0, c] = histu[c]

    @pl.when(c == last_c)
    def _():
        o_ref[...] = x_ref[...] + acc[...]


def _ffn(x, g, h0, wu, cw, cb, wd, *, n_seq, s, tm, cwid):
    m = x.shape[0]
    tiles = m // n_seq // tm
    hr = h0.shape[1]
    nc = D_FF // cwid
    row = lambda n, j, c: (n * tiles + j, 0)
    hist_g = lambda n, j, c: (n, 0, c)
    hist_u = lambda n, j, c: (n, 0, nc + c)
    col_g = lambda n, j, c: (0, c)
    col_u = lambda n, j, c: (0, nc + c)
    hist_shape = jax.ShapeDtypeStruct((n_seq, nc, hr, cwid), F32)
    hist_out = pl.BlockSpec((1, nc, hr, cwid), lambda n, j, c: (n, 0, 0, 0))
    once = dict(pipeline_mode=pl.Buffered(1)) if nc == 1 else {}
    x_new, hg, hu = pl.pallas_call(
        functools.partial(_ffn_body, s=s, tm=tm, hr=hr),
        grid=(n_seq, tiles, nc),
        in_specs=[pl.BlockSpec((tm, D_MODEL), row),
                  pl.BlockSpec((tm, D_MODEL), row),
                  pl.BlockSpec((1, hr, cwid), hist_g), pl.BlockSpec((1, hr, cwid), hist_u),
                  pl.BlockSpec((D_MODEL, cwid), col_g, **once), pl.BlockSpec((D_MODEL, cwid), col_u, **once),
                  pl.BlockSpec((FFN_CONV, cwid), col_g), pl.BlockSpec((FFN_CONV, cwid), col_u),
                  pl.BlockSpec((1, cwid), col_g), pl.BlockSpec((1, cwid), col_u),
                  pl.BlockSpec((cwid, D_MODEL), lambda n, j, c: (c, 0), **once)],
        out_specs=[pl.BlockSpec((tm, D_MODEL), row), hist_out, hist_out],
        out_shape=[jax.ShapeDtypeStruct((m, D_MODEL), F32), hist_shape, hist_shape],
        scratch_shapes=[pltpu.VMEM((tm, D_MODEL), BF16),
                        pltpu.VMEM((tm, D_MODEL), F32),
                        pltpu.VMEM((hr + tm, cwid), F32),
                        pltpu.VMEM((hr + tm, cwid), F32),
                        pltpu.VMEM((nc, hr, cwid), F32),
                        pltpu.VMEM((nc, hr, cwid), F32)],
        compiler_params=_cparams(("arbitrary", "arbitrary", "arbitrary")),
        name="ffn",
    )(x, g, h0, h0, wu, wu, cw, cw, cb, cb, wd)
    to_rows = lambda a: jnp.swapaxes(a, 1, 2).reshape(n_seq, hr, D_FF)
    return x_new, jnp.concatenate([to_rows(hg), to_rows(hu)], axis=2)


def _final_norm_body(x_ref, g_ref, o_ref):
    o_ref[...] = _rms(x_ref[...], g_ref[...])


def _final_norm(x, g, tm):
    m, d = x.shape
    return pl.pallas_call(
        _final_norm_body,
        grid=(m // tm,),
        in_specs=[pl.BlockSpec((tm, d), lambda i: (i, 0)), pl.BlockSpec((1, d), lambda i: (0, 0))],
        out_specs=pl.BlockSpec((tm, d), lambda i: (i, 0)),
        out_shape=jax.ShapeDtypeStruct((m, d), F32),
        compiler_params=_cparams(("arbitrary",)),
        name="final_norm",
    )(x, g)


def _selection_matrices():
    d = FOX_HEAD_DIM
    selq = np.zeros((FOX_HEADS, 2 * d, WIDTH), np.float32)
    selk = np.zeros((FOX_HEADS, WIDTH, 2 * d), np.float32)
    selv = np.zeros((FOX_HEADS, V_ROWS, WIDTH), np.float32)
    for h in range(FOX_HEADS):
        for i in range(d):
            selq[h, i, h * d + i] = 1.0
            selk[h, h * d + i, i] = 1.0
            selv[h, i, h * d + i] = 1.0
    return (jnp.asarray(selq, BF16), jnp.asarray(selk, BF16), jnp.asarray(selv, BF16))


def _layer_params(l, norm1_g, w_in, cm_norm_g, cm_ws, cm_b, pool_w, pool_scale, fox_bf, ssd_conv_w, ssd_conv_b,
                  ssd_dt_bias, ssd_a_log, ssd_d, ssd_norm_g, w_branch, w_out, norm2_g, ffn_up, ffn_conv_w,
                  ffn_conv_b, ffn_down, n_dec, n_tok):
    w = w_in[l]
    o_f = 2 * WIDTH + WIDTH + 3 * WIDTH
    o_z = o_f + FOX_HEADS
    o_xbc = o_z + WIDTH
    o_dt = o_xbc + SSD_CONV_DIM
    o_gate = o_dt + SSD_HEADS
    w_p = jnp.concatenate([w[:, :o_f], w[:, o_z:o_dt], w[:, o_f:o_z], w[:, o_dt:o_gate],
                           jnp.zeros((D_MODEL, SMALL_W - FOX_HEADS - SSD_HEADS), F32)], axis=1).astype(BF16)
    causal = jnp.tril(jnp.ones((CM_CHUNK, CM_CHUNK), F32))
    ws = cm_ws[l] * causal
    gd = WIDTH // CM_GROUPS
    eye = jnp.eye(n_dec, dtype=F32)
    pd = len(POOL_WINDOWS)
    pw = pool_w[l].astype(F32)
    pool_bd = jnp.zeros((WIDTH, WIDTH), F32)
    pg = WIDTH // pd
    for gi in range(pd):
        pool_bd = pool_bd.at[gi * pg:(gi + 1) * pg, gi * pg:(gi + 1) * pg].set(pw[gi])
    lanes = jnp.arange(SMALL_W)
    head_of_lane = jnp.clip(lanes - FOX_HEADS, 0, SSD_HEADS - 1)
    dt_lane = (lanes >= FOX_HEADS) & (lanes < FOX_HEADS + SSD_HEADS)
    a = -jnp.exp(ssd_a_log[l].astype(F32))
    return dict(
        g1=norm1_g[l][None, :],
        w_p=w_p,
        w_gate=w[:, o_gate:].astype(BF16),
        cm_g=cm_norm_g[l][None, :],
        ws_prompt=ws.astype(BF16),
        cmb_prompt=jnp.repeat(cm_b[l].T, gd, axis=1),
        ws_sample=jnp.stack([jnp.kron(ws[gi, :n_tok, :n_tok], eye) for gi in range(CM_GROUPS)]).astype(BF16),
        cmb_sample=jnp.repeat(jnp.repeat(cm_b[l].T[:n_tok], n_dec, axis=0), gd, axis=1),
        pool_bd=pool_bd.astype(BF16),
        pool_scale=pool_scale[l][None, :].astype(F32),
        bf_row=jnp.pad(fox_bf[l], (0, SMALL_W - FOX_HEADS))[None, :],
        bf_col=jnp.broadcast_to(jnp.tile(jnp.pad(fox_bf[l], (0, HEAD_PAD - FOX_HEADS)), n_tok)[:, None],
                                (n_tok * HEAD_PAD, PAGE_SIZE)),
        conv_w=ssd_conv_w[l], conv_b=ssd_conv_b[l][None, :],
        dtb_row=jnp.where(dt_lane, ssd_dt_bias[l][head_of_lane], 0.0)[None, :],
        a_row=jnp.where(dt_lane, a[head_of_lane], 0.0)[None, :],
        d_row=jnp.repeat(ssd_d[l].astype(F32), SSD_HEAD_DIM)[None, :],
        dtb_bh=jnp.tile(ssd_dt_bias[l], n_dec)[:, None, None],
        a_bh=jnp.tile(a, n_dec)[:, None, None],
        ssd_g=ssd_norm_g[l][None, :],
        w_branch=w_branch[l].astype(BF16),
        w_out=w_out[l].astype(BF16),
        g2=norm2_g[l][None, :],
        ffn_up=ffn_up[l].astype(BF16),
        ffn_cw=ffn_conv_w[l], ffn_cb=ffn_conv_b[l][None, :],
        ffn_down=ffn_down[l].astype(BF16),
    )


def _prompt_layer(x, w, consts, n_seq):
    m = x.shape[0]
    seq_len = m // n_seq
    selq, selk, selv = consts["sel"]
    big = 2048 if seq_len % 2048 == 0 else 512
    mid = 1024 if seq_len % 1024 == 0 else 512
    p = _proj(x, w["g1"], w["w_p"], tm=mid)
    y_a, = _mixa(p, w["ws_prompt"], w["cmb_prompt"], w["cm_g"], t=CM_CHUNK, tm=big, with_va=False)
    y_b = _pool(p, U_POOL, jnp.zeros((n_seq, POOL_PAD_STEPS, WIDTH), F32), w["pool_bd"], w["pool_scale"],
                n_seq=n_seq, s=1, tm=big, pos0=0)
    lf, qa, ka, va = _fox_prep(p, w["bf_row"], selq, selk, selv, n_seq=n_seq, tm=mid)
    tq = mid
    y_c = _fox_flash(qa, ka, va, tq=tq, wq=512)
    y_d, ssd_last = _ssd_prompt(p, w["conv_w"], w["conv_b"], w["dtb_row"], w["a_row"], w["d_row"], w["ssd_g"],
                                n_seq=n_seq)
    x, h2 = _merge(x, w["g1"], w["g2"], (y_a, y_b, y_c, y_d), w["w_gate"], w["w_branch"], w["w_out"], tm=512)
    x, ffn_hist = _ffn(x, h2, jnp.zeros((n_seq, 8, 2 * D_FF), F32), w["ffn_up"], w["ffn_cw"],
                       w["ffn_cb"], w["ffn_down"], n_seq=n_seq, s=1, tm=512, cwid=D_FF)
    p3 = p.reshape(n_seq, seq_len, P_COLS)
    unit = lambda u: p3[:, :, u * WIDTH:(u + 1) * WIDTH]
    state = dict(
        k=unit(U_K).reshape(n_seq, seq_len, FOX_HEADS, FOX_HEAD_DIM),
        v=unit(U_V).reshape(n_seq, seq_len, FOX_HEADS, FOX_HEAD_DIM),
        logf=lf.reshape(n_seq, seq_len, SMALL_W)[:, :, :FOX_HEADS],
        pool=unit(U_POOL)[:, -POOL_HIST:],
        ssd_conv=p3[:, -(SSD_CONV - 1):, U_XS * WIDTH:U_XS * WIDTH + SSD_CONV_DIM],
        ssd=ssd_last.reshape(n_seq, SSD_HEADS, SSD_HEAD_DIM, SSD_STATE),
        ffn_conv=ffn_hist[:, -(FFN_CONV - 1):],
    )
    return x, state


def _to_time_major(a):
    return jnp.swapaxes(a, 0, 1).reshape((a.shape[0] * a.shape[1],) + a.shape[2:])


def _from_time_major(a, n_b):
    return jnp.swapaxes(a.reshape((a.shape[0] // n_b, n_b) + a.shape[1:]), 0, 1)


def _sample_layer(x, w, consts, layer, past, page_table, n_b, past_len):
    m = x.shape[0]
    n_tok = m // n_b
    p = _proj(x, w["g1"], w["w_p"], tm=m)
    unit = lambda u: p[:, u * WIDTH:(u + 1) * WIDTH]
    small = p[:, 10 * WIDTH:]

    y_a, va = _mixa(p, w["ws_sample"], w["cmb_sample"], w["cm_g"], t=m, tm=m, with_va=True)

    pool_new = unit(U_POOL)
    pool_hist = jnp.concatenate([jnp.zeros(((POOL_PAD_STEPS - POOL_HIST) * n_b, WIDTH), F32),
                                 _to_time_major(past["pool"])], axis=0)
    y_b = _pool(pool_new, 0, pool_hist[None], w["pool_bd"], w["pool_scale"],
                n_seq=1, s=n_b, tm=m, pos0=past_len)
    new_pool = jnp.concatenate([past["pool"], _from_time_major(pool_new, n_b)], axis=1)[:, -POOL_HIST:]

    pad_tok = lambda a: jnp.pad(_from_time_major(a, n_b), ((0, 0), (0, TOK_PAD - n_tok), (0, 0)))
    f_raw = _from_time_major(small[:, :FOX_HEADS], n_b)
    fr_lane = jnp.pad(jnp.swapaxes(f_raw, 1, 2),
                      ((0, 0), (0, HEAD_PAD - FOX_HEADS), (0, PAGE_SIZE - n_tok)))
    fr_lane = jnp.tile(fr_lane, (1, n_tok, 1))
    y_c = _fox_decode(layer, page_table, _from_time_major(unit(U_Q), n_b), pad_tok(unit(U_K)),
                      pad_tok(unit(U_V)), fr_lane, w["bf_col"], past["cache_k"], past["cache_v"],
                      past["cache_lft"])
    y_c = _to_time_major(y_c)
    logf = _log_sigmoid_rows(small, w["bf_row"])[:, :FOX_HEADS]

    xbc_new = p[:, U_XS * WIDTH:U_XS * WIDTH + SSD_CONV_DIM]
    xbc_ext = jnp.concatenate([_to_time_major(past["ssd_conv"]), xbc_new], axis=0)
    act = _conv_silu(xbc_ext, w["conv_w"], w["conv_b"], s=n_b, rows=m)
    xs = act[:, :WIDTH]
    rep_heads = lambda a: jnp.repeat(_from_time_major(a, n_b).reshape(n_b, n_tok, SSD_GROUPS, SSD_STATE),
                                     SSD_HEADS // SSD_GROUPS, axis=2)
    to_bh = lambda a: jnp.transpose(a, (0, 2, 1, 3)).reshape(n_b * SSD_HEADS, n_tok, SSD_STATE)
    x_bh = jnp.transpose(_from_time_major(xs, n_b).reshape(n_b, n_tok, SSD_HEADS, SSD_HEAD_DIM),
                         (0, 2, 3, 1)).reshape(n_b * SSD_HEADS, SSD_HEAD_DIM, n_tok)
    dtr = _from_time_major(small[:, FOX_HEADS:FOX_HEADS + SSD_HEADS], n_b)
    dtr_bh = jnp.swapaxes(dtr, 1, 2).reshape(n_b * SSD_HEADS, 1, n_tok)
    h0 = past["ssd"].reshape(n_b * SSD_HEADS, SSD_HEAD_DIM, SSD_STATE)
    y_bh, h_last = _ssd_step(x_bh, to_bh(rep_heads(act[:, WIDTH:2 * WIDTH])),
                             to_bh(rep_heads(act[:, 2 * WIDTH:])), dtr_bh, w["dtb_bh"], w["a_bh"], h0, blk=32)
    y_s = _to_time_major(jnp.transpose(y_bh.reshape(n_b, SSD_HEADS, SSD_HEAD_DIM, n_tok),
                                       (0, 3, 1, 2)).reshape(n_b, n_tok, WIDTH))
    y_d = _ssd_gate(y_s, xs, unit(U_Z), w["d_row"], w["ssd_g"])

    x, h2 = _merge(x, w["g1"], w["g2"], (y_a, y_b, y_c, y_d), w["w_gate"], w["w_branch"], w["w_out"],
                   tm=min(m, 256))
    ffn_h0 = _to_time_major(past["ffn_conv"])[None]
    x, ffn_hist = _ffn(x, h2, ffn_h0, w["ffn_up"], w["ffn_cw"], w["ffn_cb"], w["ffn_down"],
                       n_seq=1, s=n_b, tm=m, cwid=D_FF // 2)
    state = dict(
        k=_from_time_major(unit(U_K), n_b).reshape(n_b, n_tok, FOX_HEADS, FOX_HEAD_DIM),
        v=_from_time_major(unit(U_V), n_b).reshape(n_b, n_tok, FOX_HEADS, FOX_HEAD_DIM),
        logf=_from_time_major(logf, n_b),
        chunk_v=_from_time_major(va, n_b),
        pool=new_pool,
        ssd_conv=_from_time_major(xbc_ext[-(SSD_CONV - 1) * n_b:], n_b),
        ssd=h_last.reshape(n_b, SSD_HEADS, SSD_HEAD_DIM, SSD_STATE),
        ffn_conv=_from_time_major(ffn_hist[0], n_b),
    )
    return x, state


def _log_sigmoid_body(x_ref, b_ref, o_ref):
    o_ref[...] = _log_sigmoid(x_ref[...] + b_ref[...])


def _log_sigmoid_rows(small, bf_row):
    return pl.pallas_call(
        _log_sigmoid_body,
        out_shape=jax.ShapeDtypeStruct(small.shape, F32),
        name="log_forget",
    )(small, bf_row)


def kernel(x_prompt, x_sample, cache_k, cache_v, cache_logf, page_table, state_pool, state_ssd_conv, state_ssd,
           state_ffn_conv, norm1_g, w_in, cm_norm_g, cm_ws, cm_b, pool_w, pool_scale, fox_bf, ssd_conv_w,
           ssd_conv_b, ssd_dt_bias, ssd_a_log, ssd_d, ssd_norm_g, w_branch, w_out, norm2_g, ffn_up, ffn_conv_w,
           ffn_conv_b, ffn_down, final_norm_g):
    n_seq, seq_len, _ = x_prompt.shape
    n_b, n_tok, _ = x_sample.shape
    depth = w_in.shape[0]
    n_pages = page_table.shape[1]
    past_len = n_pages * cache_k.shape[2]
    n_phys = cache_k.shape[1]

    consts = dict(sel=_selection_matrices())
    ck = jnp.transpose(cache_k, (0, 1, 3, 4, 2)).reshape(depth, n_phys, WIDTH, PAGE_SIZE)
    cv = jnp.transpose(cache_v, (0, 1, 3, 4, 2)).reshape(depth, n_phys, WIDTH, PAGE_SIZE)
    clt = jnp.pad(jnp.swapaxes(cache_logf, 2, 3), ((0, 0), (0, 0), (0, HEAD_PAD - FOX_HEADS), (0, 0)))

    xp = x_prompt.reshape(n_seq * seq_len, D_MODEL)
    xs = _to_time_major(x_sample)
    st_p, st_s = [], []
    for l in range(depth):
        w = _layer_params(l, norm1_g, w_in, cm_norm_g, cm_ws, cm_b, pool_w, pool_scale, fox_bf, ssd_conv_w,
                          ssd_conv_b, ssd_dt_bias, ssd_a_log, ssd_d, ssd_norm_g, w_branch, w_out, norm2_g,
                          ffn_up, ffn_conv_w, ffn_conv_b, ffn_down, n_b, n_tok)
        past = dict(cache_k=ck, cache_v=cv, cache_lft=clt, pool=state_pool[l], ssd_conv=state_ssd_conv[l],
                    ssd=state_ssd[l], ffn_conv=state_ffn_conv[l])
        xp, sp = _prompt_layer(xp, w, consts, n_seq)
        xs, ss = _sample_layer(xs, w, consts, l, past, page_table, n_b, past_len)
        st_p.append(sp)
        st_s.append(ss)
    gfin = final_norm_g[None, :]
    y_prompt = _final_norm(xp, gfin, tm=2048 if xp.shape[0] % 2048 == 0 else 512).reshape(n_seq, seq_len, D_MODEL)
    y_sample = _from_time_major(_final_norm(xs, gfin, tm=xs.shape[0]), n_b)

    def stacked(states, name):
        return jnp.stack([s[name] for s in states], axis=0)

    return (y_prompt, y_sample,
            stacked(st_p, "k"), stacked(st_p, "v"), stacked(st_p, "logf"),
            stacked(st_s, "k"), stacked(st_s, "v"), stacked(st_s, "logf"),
            stacked(st_s, "chunk_v"),
            stacked(st_p, "pool"), stacked(st_s, "pool"),
            stacked(st_p, "ssd_conv"), stacked(st_s, "ssd_conv"),
            stacked(st_p, "ssd"), stacked(st_s, "ssd"),
            stacked(st_p, "ffn_conv"), stacked(st_s, "ffn_conv"))
```
